```python
import jax
import jax.numpy as jnp
from jax import lax
import numpy as np

D_MODEL = 2048
BATCH = 32
SEQ = 256
DEPTH = 2
DEC_BATCH = 8
DEC_SEQ = 2048
PAST_LEN = 512

GRID_W = 64
HEAD_DIM = 64
EPS = 1e-6
NEG_INF = -1e30
ROPE_THETA = 10000.0
ROPE_AXIS_DIM = HEAD_DIM // 2
ROPE_PAIRS = ROPE_AXIS_DIM // 2
CTX_BLOCK = 128

A_HEADS = 8
A_KV_HEADS = 2
A_GROUP = A_HEADS // A_KV_HEADS
A_WINDOW = 128
A_BLOCK = 128
B_HEADS = 8
NA_ROWS_MAX = 8
NA_COLS = 16
NA_QCOLS = 16
NA_STRIP = 2 * NA_COLS
C_HEADS = 4
C_DK = 128
C_DV = 256
C_RANK = 16
C_TAU = 16.0
C_CHUNK = 64

A_WIDTH = A_HEADS * HEAD_DIM
B_WIDTH = B_HEADS * HEAD_DIM
C_WIDTH = C_HEADS * C_DV
D_MIX = A_WIDTH + B_WIDTH + C_WIDTH
IN_SIZES = (A_WIDTH, A_KV_HEADS * HEAD_DIM, A_KV_HEADS * HEAD_DIM,
            B_WIDTH, B_WIDTH, B_WIDTH,
            C_HEADS * C_DK, C_HEADS * C_DK, C_WIDTH, C_WIDTH, C_RANK, C_RANK)
N_IN = sum(IN_SIZES)

N_GROUPS = 4
E_PER_GROUP = 4
N_EXPERTS = N_GROUPS * E_PER_GROUP
TOP_K = 2
D_EXPERT = 512

kernel_name = 'hybrid_diffusion_prefix_trunk_step'


def rmsnorm(x, g):
    xf = x.astype(jnp.float32)
    y = xf * lax.rsqrt(jnp.mean(xf * xf, axis=-1, keepdims=True) + EPS)
    return (y * g.astype(jnp.float32)).astype(x.dtype)


def adaln(cond, w_mod, b_mod):
    m = jax.nn.silu(cond) @ w_mod + b_mod
    return jnp.split(m[..., None, :], 6, axis=-1)


def axial_rope(x):
    n = x.shape[1]
    t = jnp.arange(n)
    inv = 1.0 / (ROPE_THETA ** (jnp.arange(ROPE_PAIRS, dtype=jnp.float32) * 2.0 / ROPE_AXIS_DIM))
    ang_r = (t // GRID_W).astype(jnp.float32)[:, None] * inv
    ang_c = (t % GRID_W).astype(jnp.float32)[:, None] * inv
    bshape = (n,) + (1,) * (x.ndim - 3) + (ROPE_PAIRS,)

    def rot(u, ang):
        cs = jnp.cos(ang).reshape(bshape).astype(u.dtype)
        sn = jnp.sin(ang).reshape(bshape).astype(u.dtype)
        u1, u2 = jnp.split(u, 2, axis=-1)
        return jnp.concatenate([u1 * cs - u2 * sn, u1 * sn + u2 * cs], axis=-1)

    return jnp.concatenate([rot(x[..., :ROPE_AXIS_DIM], ang_r), rot(x[..., ROPE_AXIS_DIM:], ang_c)], axis=-1)


def softmax_sink(s, sink):
    m = jnp.maximum(jnp.max(s, axis=-1, keepdims=True), sink)
    e = jnp.exp(s - m)
    return e / (jnp.sum(e, axis=-1, keepdims=True) + jnp.exp(sink - m))


def context_attention(q, k, v, sink):
    bsz, n, hk, g, d = q.shape
    nb = n // CTX_BLOCK
    scale = d ** -0.5
    qb = jnp.moveaxis(q.reshape(bsz, nb, CTX_BLOCK, hk, g, d), 1, 0)

    def block(qi):
        s = jnp.einsum('bqhgd,bkhd->bhgqk', qi, k).astype(jnp.float32) * scale
        if sink is None:
            p = jax.nn.softmax(s, axis=-1)
        else:
            p = softmax_sink(s, sink.astype(jnp.float32)[None, :, :, None, None])
        return jnp.einsum('bhgqk,bkhd->bqhgd', p.astype(v.dtype), v)

    o = lax.map(block, qb)
    return jnp.moveaxis(o, 0, 1).reshape(bsz, n, hk * g * d)


def window_attention_latent(q, k, v, k_ctx, v_ctx, sink):
    bsz, n, hk, g, d = q.shape
    nb = n // A_BLOCK
    nk = 3 * A_BLOCK
    scale = d ** -0.5
    qb = q.reshape(bsz, nb, A_BLOCK, hk, g, d)
    pad = ((0, 0), (A_BLOCK, A_BLOCK), (0, 0), (0, 0))

    def band(t):
        tp = jnp.pad(t, pad).reshape(bsz, nb + 2, A_BLOCK, hk, d)
        return jnp.concatenate([tp[:, :-2], tp[:, 1:-1], tp[:, 2:]], axis=2)

    kb, vb = band(k), band(v)
    q_pos = jnp.arange(n).reshape(nb, A_BLOCK)
    k_pos = (jnp.arange(nb) * A_BLOCK - A_BLOCK)[:, None] + jnp.arange(nk)[None, :]
    mask = ((jnp.abs(q_pos[:, :, None] - k_pos[:, None, :]) <= A_WINDOW)
            & (k_pos[:, None, :] >= 0) & (k_pos[:, None, :] < n))
    s_loc = jnp.einsum('bnqhgd,bnkhd->bnhgqk', qb, kb).astype(jnp.float32) * scale
    s_loc = jnp.where(mask[None, :, None, None], s_loc, NEG_INF)
    s_ctx = jnp.einsum('bnqhgd,blhd->bnhgql', qb, k_ctx).astype(jnp.float32) * scale
    s = jnp.concatenate([s_loc, s_ctx], axis=-1)
    p = softmax_sink(s, sink.reshape(hk, g).astype(jnp.float32)[None, None, :, :, None, None]).astype(v.dtype)
    o = (jnp.einsum('bnhgqk,bnkhd->bnqhgd', p[..., :nk], vb)
         + jnp.einsum('bnhgql,blhd->bnqhgd', p[..., nk:], v_ctx))
    return o.reshape(bsz, n, hk * g * d)


def neighbourhood_attention_latent(q, k, v, k_ctx, v_ctx, rpb):
    bsz, n, h, d = q.shape
    rows = n // GRID_W
    kr = min(NA_ROWS_MAX, rows)
    ncb = GRID_W // NA_QCOLS
    nk = kr * NA_STRIP
    scale = d ** -0.5
    r = jnp.arange(rows)
    start_r = jnp.clip(r - kr // 2, 0, rows - kr)
    key_rows = start_r[:, None] + jnp.arange(kr)[None, :]
    j = jnp.arange(ncb)
    strip0 = jnp.clip(j * NA_QCOLS - NA_COLS // 2, 0, GRID_W - NA_STRIP)
    key_cols = strip0[:, None] + jnp.arange(NA_STRIP)[None, :]
    q_cols = j[:, None] * NA_QCOLS + jnp.arange(NA_QCOLS)[None, :]
    start_c = jnp.clip(q_cols - NA_COLS // 2, 0, GRID_W - NA_COLS)
    idx = (key_rows[:, None, :, None] * GRID_W + key_cols[None, :, None, :]).reshape(rows, ncb, nk)
    kg = jnp.take(k, idx, axis=1)
    vg = jnp.take(v, idx, axis=1)
    qg = q.reshape(bsz, rows, ncb, NA_QCOLS, h, d)
    col_ok = ((key_cols[:, None, :] >= start_c[:, :, None])
              & (key_cols[:, None, :] < start_c[:, :, None] + NA_COLS))
    mask = jnp.broadcast_to(col_ok[:, :, None, :], (ncb, NA_QCOLS, kr, NA_STRIP)).reshape(ncb, NA_QCOLS, nk)
    dr = key_rows - r[:, None] + (NA_ROWS_MAX - 1)
    dc = jnp.clip(key_cols[:, None, :] - q_cols[:, :, None] + (NA_COLS - 1), 0, 2 * NA_COLS - 2)
    bias = rpb[:, dr[:, None, None, :, None], dc[None, :, :, None, :]].reshape(h, rows, ncb, NA_QCOLS, nk)
    s_loc = (jnp.einsum('brjqhd,brjkhd->brjhqk', qg, kg).astype(jnp.float32) * scale
             + jnp.transpose(bias, (1, 2, 0, 3, 4)).astype(jnp.float32)[None])
    s_loc = jnp.where(mask[None, None, :, None], s_loc, NEG_INF)
    s_ctx = jnp.einsum('brjqhd,blhd->brjhql', qg, k_ctx).astype(jnp.float32) * scale
    p = jax.nn.softmax(jnp.concatenate([s_loc, s_ctx], axis=-1), axis=-1).astype(v.dtype)
    o = (jnp.einsum('brjhqk,brjkhd->brjqhd', p[..., :nk], vg)
         + jnp.einsum('brjhql,blhd->brjqhd', p[..., nk:], v_ctx))
    return o.reshape(bsz, n, h * d)


def gla_chunked(q, k, v, log_a, s0):
    bsz, n, h, dk = q.shape
    dv = v.shape[-1]
    nc = n // C_CHUNK
    f32 = jnp.float32
    qc = q.astype(f32).reshape(bsz, nc, C_CHUNK, h, dk)
    kc = k.astype(f32).reshape(bsz, nc, C_CHUNK, h, dk)
    vc = v.astype(f32).reshape(bsz, nc, C_CHUNK, h, dv)
    b = jnp.cumsum(log_a.reshape(bsz, nc, C_CHUNK, h, dk), axis=2)
    b_last = b[:, :, -1:]
    q_dec = qc * jnp.exp(b)
    k_inv = kc * jnp.exp(-b)
    k_end = kc * jnp.exp(b_last - b)
    ds = jnp.einsum('bnchk,bnchv->nbhkv', k_end, vc)
    decay = jnp.moveaxis(jnp.exp(b_last[:, :, 0]), 1, 0)

    def step(s, inp):
        dec, d_s = inp
        return dec[..., None] * s + d_s, s

    s_final, s_in = lax.scan(step, s0.astype(f32), (decay, ds))
    inter = jnp.einsum('bnchk,nbhkv->bnchv', q_dec, s_in)
    att = jnp.einsum('bnthk,bnshk->bnhts', q_dec, k_inv)
    att = jnp.where(jnp.tril(jnp.ones((C_CHUNK, C_CHUNK), dtype=bool)), att, 0.0)
    intra = jnp.einsum('bnhts,bnshv->bnthv', att, vc)
    o = (inter + intra).reshape(bsz, n, h, dv).astype(q.dtype)
    return o, s_final


def gla_bidirectional(q, k, v, la_f, la_b, s0_f, s0_b):
    o_f, s_f = gla_chunked(q, k, v, la_f, s0_f)
    fl = lambda t: jnp.flip(t, axis=1)
    o_b, s_b = gla_chunked(fl(q), fl(k), fl(v), fl(la_b), s0_b)
    return o_f + fl(o_b), s_f, s_b


def mixer_inputs(h, w_in, qn_a, kn_a, qn_b, kn_b, w_gk_f, b_gk_f, w_gk_b, b_gk_b):
    bsz, n, _ = h.shape
    z = h @ w_in
    parts = []
    off = 0
    for size in IN_SIZES:
        parts.append(z[..., off:off + size])
        off += size
    aq, ak, av, bq, bk, bv, cq, ck, cv, cg, crf, crb = parts
    aq = rmsnorm(aq.reshape(bsz, n, A_KV_HEADS, A_GROUP, HEAD_DIM), qn_a)
    ak = rmsnorm(ak.reshape(bsz, n, A_KV_HEADS, HEAD_DIM), kn_a)
    av = av.reshape(bsz, n, A_KV_HEADS, HEAD_DIM)
    bq = rmsnorm(bq.reshape(bsz, n, B_HEADS, HEAD_DIM), qn_b)
    bk = rmsnorm(bk.reshape(bsz, n, B_HEADS, HEAD_DIM), kn_b)
    bv = bv.reshape(bsz, n, B_HEADS, HEAD_DIM)
    cq = cq.reshape(bsz, n, C_HEADS, C_DK) * (C_DK ** -0.5)
    ck = ck.reshape(bsz, n, C_HEADS, C_DK)
    cv = cv.reshape(bsz, n, C_HEADS, C_DV)
    la_f = jax.nn.log_sigmoid((crf @ w_gk_f + b_gk_f).astype(jnp.float32)).reshape(bsz, n, C_HEADS, C_DK) / C_TAU
    la_b = jax.nn.log_sigmoid((crb @ w_gk_b + b_gk_b).astype(jnp.float32)).reshape(bsz, n, C_HEADS, C_DK) / C_TAU
    return aq, ak, av, bq, bk, bv, cq, ck, cv, cg, la_f, la_b


def mixer_output(oa, ob, oc, cg, gla_g, w_out):
    bsz, n = oa.shape[:2]
    oc = rmsnorm(oc, gla_g) * jax.nn.silu(cg.reshape(bsz, n, C_HEADS, C_DV))
    return jnp.concatenate([oa, ob, oc.reshape(bsz, n, C_WIDTH)], axis=-1) @ w_out


def hier_moe(h, w_group, w_router, w1, w3, w2):
    bsz, n, d = h.shape
    t = h.reshape(bsz * n, d)
    ntok = t.shape[0]
    gl = (t @ w_group).astype(jnp.float32)
    g_idx = jnp.argmax(gl, axis=-1)
    g_w = jnp.take_along_axis(jax.nn.softmax(gl, axis=-1), g_idx[:, None], axis=-1)
    el = (t @ w_router).astype(jnp.float32).reshape(ntok, N_GROUPS, E_PER_GROUP)
    el_sel = jnp.take_along_axis(el, g_idx[:, None, None], axis=1)[:, 0]
    top_v, top_i = lax.top_k(el_sel, TOP_K)
    top_w = jax.nn.softmax(top_v, axis=-1) * g_w
    expert_id = g_idx[:, None] * E_PER_GROUP + top_i
    gates = jnp.sum(jax.nn.one_hot(expert_id, N_EXPERTS, dtype=jnp.float32) * top_w[..., None], axis=1)
    hid = jax.nn.silu(jnp.einsum('td,edf->tef', t, w1)) * jnp.einsum('td,edf->tef', t, w3)
    y = jnp.einsum('tef,te,efd->td', hid, gates.astype(t.dtype), w2)
    return y.reshape(bsz, n, d)


def context_layer(x, c_ctx, n1, n2, w_mod, b_mod, mix_w, sink, gla_g, w_out, moe_w):
    sh1, sc1, g1, sh2, sc2, g2 = adaln(c_ctx, w_mod, b_mod)
    h = rmsnorm(x, n1) * (1 + sc1) + sh1
    aq, ak, av, bq, bk, bv, cq, ck, cv, cg, la_f, la_b = mixer_inputs(h, *mix_w)
    oa = context_attention(aq, ak, av, sink.reshape(A_KV_HEADS, A_GROUP))
    ob = context_attention(bq[:, :, :, None], bk, bv, None)
    zero = jnp.zeros((x.shape[0], C_HEADS, C_DK, C_DV), jnp.float32)
    oc, s_f, s_b = gla_bidirectional(cq, ck, cv, la_f, la_b, zero, zero)
    x = x + g1 * mixer_output(oa, ob, oc, cg, gla_g, w_out)
    h = rmsnorm(x, n2) * (1 + sc2) + sh2
    x = x + g2 * hier_moe(h, *moe_w)
    return x, (ak, av, bk, bv, s_f.astype(x.dtype), s_b.astype(x.dtype))


def latent_layer(x, cond, n1, n2, w_mod, b_mod, mix_w, sink, rpb, gla_g, w_out, moe_w,
                 ka, va, kb, vb, sf, sb):
    sh1, sc1, g1, sh2, sc2, g2 = adaln(cond, w_mod, b_mod)
    h = rmsnorm(x, n1) * (1 + sc1) + sh1
    aq, ak, av, bq, bk, bv, cq, ck, cv, cg, la_f, la_b = mixer_inputs(h, *mix_w)
    oa = window_attention_latent(axial_rope(aq), axial_rope(ak), av, ka, va, sink)
    ob = neighbourhood_attention_latent(bq, bk, bv, kb, vb, rpb)
    oc, _, _ = gla_bidirectional(cq, ck, cv, la_f, la_b, sf, sb)
    x = x + g1 * mixer_output(oa, ob, oc, cg, gla_g, w_out)
    h = rmsnorm(x, n2) * (1 + sc2) + sh2
    return x + g2 * hier_moe(h, *moe_w)


def setup_inputs(seed: int = 0) -> dict:
    key = jax.random.key(seed)
    ks = jax.random.split(key, 40)
    f32 = jnp.float32
    nrm = lambda k, shape, s: jax.random.normal(k, shape, f32) * s
    D = D_MODEL
    return {
        'x_prompt': nrm(ks[0], (BATCH, SEQ, D), 1.0),
        'x_sample': nrm(ks[1], (DEC_BATCH, DEC_SEQ, D), 1.0),
        'c': nrm(ks[2], (DEC_BATCH, D), 1.0),
        'cache_a_k': nrm(ks[3], (DEC_BATCH, DEPTH, PAST_LEN, A_KV_HEADS, HEAD_DIM), 1.0),
        'cache_a_v': nrm(ks[4], (DEC_BATCH, DEPTH, PAST_LEN, A_KV_HEADS, HEAD_DIM), 1.0),
        'cache_b_k': nrm(ks[5], (DEC_BATCH, DEPTH, PAST_LEN, B_HEADS, HEAD_DIM), 1.0),
        'cache_b_v': nrm(ks[6], (DEC_BATCH, DEPTH, PAST_LEN, B_HEADS, HEAD_DIM), 1.0),
        'state_c_fwd': nrm(ks[7], (DEC_BATCH, DEPTH, C_HEADS, C_DK, C_DV), 0.5),
        'state_c_bwd': nrm(ks[8], (DEC_BATCH, DEPTH, C_HEADS, C_DK, C_DV), 0.5),
        'c_ctx': nrm(ks[9], (D,), 1.0),
        'norm1_g': 1.0 + nrm(ks[10], (DEPTH, D), 0.02),
        'norm2_g': 1.0 + nrm(ks[11], (DEPTH, D), 0.02),
        'w_mod': nrm(ks[12], (DEPTH, D, 6 * D), 0.5 * D ** -0.5),
        'b_mod': nrm(ks[13], (DEPTH, 6 * D), 0.02),
        'w_in': nrm(ks[14], (DEPTH, D, N_IN), D ** -0.5),
        'q_norm_a': 1.0 + nrm(ks[15], (DEPTH, HEAD_DIM), 0.02),
        'k_norm_a': 1.0 + nrm(ks[16], (DEPTH, HEAD_DIM), 0.02),
        'sink_a': nrm(ks[17], (DEPTH, A_HEADS), 0.5),
        'q_norm_b': 1.0 + nrm(ks[18], (DEPTH, HEAD_DIM), 0.02),
        'k_norm_b': 1.0 + nrm(ks[19], (DEPTH, HEAD_DIM), 0.02),
        'rpb_b': nrm(ks[20], (DEPTH, B_HEADS, 2 * NA_ROWS_MAX - 1, 2 * NA_COLS - 1), 0.1),
        'w_gk_up_f': nrm(ks[21], (DEPTH, C_RANK, C_HEADS * C_DK), C_RANK ** -0.5),
        'b_gk_f': nrm(ks[22], (DEPTH, C_HEADS * C_DK), 0.1),
        'w_gk_up_b': nrm(ks[23], (DEPTH, C_RANK, C_HEADS * C_DK), C_RANK ** -0.5),
        'b_gk_b': nrm(ks[24], (DEPTH, C_HEADS * C_DK), 0.1),
        'gla_norm_g': 1.0 + nrm(ks[25], (DEPTH, C_DV), 0.02),
        'w_out': nrm(ks[26], (DEPTH, D_MIX, D), D_MIX ** -0.5),
        'w_group': nrm(ks[27], (DEPTH, D, N_GROUPS), D ** -0.5),
        'w_router': nrm(ks[28], (DEPTH, D, N_EXPERTS), D ** -0.5),
        'w1': nrm(ks[29], (DEPTH, N_EXPERTS, D, D_EXPERT), D ** -0.5),
        'w3': nrm(ks[30], (DEPTH, N_EXPERTS, D, D_EXPERT), D ** -0.5),
        'w2': nrm(ks[31], (DEPTH, N_EXPERTS, D_EXPERT, D), D_EXPERT ** -0.5),
    }


def reference(x_prompt, x_sample, c, cache_a_k, cache_a_v, cache_b_k, cache_b_v,
              state_c_fwd, state_c_bwd, c_ctx, norm1_g, norm2_g, w_mod, b_mod, w_in,
              q_norm_a, k_norm_a, sink_a, q_norm_b, k_norm_b, rpb_b,
              w_gk_up_f, b_gk_f, w_gk_up_b, b_gk_b, gla_norm_g, w_out,
              w_group, w_router, w1, w3, w2):
    xp = x_prompt
    xs = x_sample
    l_ak, l_av, l_bk, l_bv, l_sf, l_sb = [], [], [], [], [], []
    for l in range(DEPTH):
        mix_w = (w_in[l], q_norm_a[l], k_norm_a[l], q_norm_b[l], k_norm_b[l],
                 w_gk_up_f[l], b_gk_f[l], w_gk_up_b[l], b_gk_b[l])
        moe_w = (w_group[l], w_router[l], w1[l], w3[l], w2[l])
        xp, (ak, av, bk, bv, sf, sb) = context_layer(
            xp, c_ctx, norm1_g[l], norm2_g[l], w_mod[l], b_mod[l], mix_w,
            sink_a[l], gla_norm_g[l], w_out[l], moe_w)
        l_ak.append(ak)
        l_av.append(av)
        l_bk.append(bk)
        l_bv.append(bv)
        l_sf.append(sf)
        l_sb.append(sb)
        xs = latent_layer(
            xs, c, norm1_g[l], norm2_g[l], w_mod[l], b_mod[l], mix_w, sink_a[l], rpb_b[l],
            gla_norm_g[l], w_out[l], moe_w,
            cache_a_k[:, l], cache_a_v[:, l], cache_b_k[:, l], cache_b_v[:, l],
            state_c_fwd[:, l], state_c_bwd[:, l])
    new_a_k = jnp.stack(l_ak, axis=1)
    new_a_v = jnp.stack(l_av, axis=1)
    new_b_k = jnp.stack(l_bk, axis=1)
    new_b_v = jnp.stack(l_bv, axis=1)
    new_c_fwd = jnp.stack(l_sf, axis=1)
    new_c_bwd = jnp.stack(l_sb, axis=1)
    return (xp, xs, new_a_k, new_a_v, new_b_k, new_b_v, new_c_fwd, new_c_bwd)
```

```python
import functools

import jax
import jax.numpy as jnp
from jax import lax
from jax.experimental import pallas as pl
from jax.experimental.pallas import tpu as pltpu

F32 = jnp.float32
BF16 = jnp.bfloat16
I32 = jnp.int32

D_MODEL = 2048
HEAD_DIM = 64
EPS = 1e-6
NEG_INF = -1e30
ROPE_THETA = 10000.0
GRID_W = 64
GRID_ROWS = 32
A_WINDOW = 128
NA_ROWS = 8
NA_COLS = 16
C_HEADS = 4
C_DK = 128
C_DV = 256
C_RANK = 16
C_CHUNK = 64
N_GROUPS = 4
E_PER_GROUP = 4
N_EXPERTS = 16
D_EXPERT = 512
LANES = 128
N_IN = 5408
N_IN_PAD = 5632
COL_AQ, COL_AKV, COL_BQ, COL_BK, COL_BV = 0, 512, 768, 1280, 1792
COL_CQ, COL_CK, COL_CV, COL_CG, COL_CR = 2304, 2816, 3328, 4352, 5376
VMEM_LIMIT = 56 * 1024 * 1024
TM_PROJ = 256
TM_MOE = 256

_TRANS_B = (((1,), (1,)), ((), ()))
_TRANS_A = (((0,), (0,)), ((), ()))


def _sds(shape, dtype=F32):
    return jax.ShapeDtypeStruct(shape, dtype)


def _params(*sem):
    return pltpu.CompilerParams(dimension_semantics=sem, vmem_limit_bytes=VMEM_LIMIT)


def _resident(shape, index_map):
    return pl.BlockSpec(shape, index_map, pipeline_mode=pl.Buffered(1))


def _adaln_body(c_ref, w_ref, b_ref, o_ref):
    c = c_ref[...]
    s = c * (1.0 / (1.0 + jnp.exp(-c)))
    o_ref[...] = jnp.dot(s.astype(BF16), w_ref[...].astype(BF16), preferred_element_type=F32) + b_ref[...]


def _adaln(cond, w_mod, b_mod, layer):
    d, n = w_mod.shape[1], w_mod.shape[2]
    tn = 1024
    return pl.pallas_call(
        _adaln_body,
        grid=(n // tn,),
        in_specs=[pl.BlockSpec((16, d), lambda j: (0, 0)),
                  pl.BlockSpec((None, d, tn), lambda j: (layer, 0, j)),
                  pl.BlockSpec((None, 1, tn), lambda j: (layer, 0, j))],
        out_specs=pl.BlockSpec((16, tn), lambda j: (0, j)),
        out_shape=_sds((16, n)),
        compiler_params=_params("arbitrary"),
        name="adaln",
    )(cond, w_mod, b_mod.reshape(b_mod.shape[0], 1, n))


def _head_norm(z, gain, bd):
    w = z.shape[1]
    ss = jnp.dot((z * z).astype(BF16), bd[:w, :w], preferred_element_type=F32)
    return z * lax.rsqrt(ss * (1.0 / HEAD_DIM) + EPS) * gain


def _rope(z, cos, sin_signed, lane):
    lower = (lane % 32) < 16
    partner = jnp.where(lower, pltpu.roll(z, LANES - 16, 1), pltpu.roll(z, 16, 1))
    return z * cos + partner * sin_signed


def _inproj_body(x_ref, sh_ref, sc_ref, n1_ref, w_ref, bd_ref, gqa_ref, gka_ref, gqb_ref, gkb_ref,
                 cos_ref, sin_ref, wgk_ref, bgk_ref,
                 aq_ref, ak_ref, av_ref, bq_ref, bk_ref, bv_ref, cq_ref, ck_ref, cv_ref, cg_ref, la_ref,
                 *, rope):
    x = x_ref[...]
    h = x * lax.rsqrt(jnp.mean(x * x, axis=-1, keepdims=True) + EPS) * n1_ref[...]
    h = h * (1.0 + sc_ref[...]) + sh_ref[...]
    hb = h.astype(BF16)
    bd = bd_ref[...]
    lane = lax.broadcasted_iota(I32, (x.shape[0], LANES), 1)
    upper = lane >= HEAD_DIM

    def proj(c0, n):
        return jnp.dot(hb, w_ref[:, c0:c0 + n], preferred_element_type=F32)

    def rot(s):
        return _rope(s, cos_ref[...], sin_ref[...], lane) if rope else s

    z = _head_norm(proj(COL_AQ, 512), gqa_ref[...], bd)
    for p in range(4):
        s = rot(z[:, LANES * p:LANES * (p + 1)]) * (HEAD_DIM ** -0.5)
        r = pltpu.roll(s, HEAD_DIM, 1)
        if p // 2 == 0:
            e0, e1 = jnp.where(upper, 0.0, s), jnp.where(upper, 0.0, r)
        else:
            e0, e1 = jnp.where(upper, r, 0.0), jnp.where(upper, s, 0.0)
        aq_ref[2 * p] = e0.astype(BF16)
        aq_ref[2 * p + 1] = e1.astype(BF16)
    z = proj(COL_AKV, 256)
    ak_ref[...] = rot(_head_norm(z[:, :LANES], gka_ref[...], bd)).astype(ak_ref.dtype)
    av_ref[...] = z[:, LANES:].astype(av_ref.dtype)
    z = _head_norm(proj(COL_BQ, 512), gqb_ref[...], bd) * (HEAD_DIM ** -0.5)
    for p in range(4):
        s = z[:, LANES * p:LANES * (p + 1)]
        bq_ref[2 * p] = jnp.where(upper, 0.0, s).astype(BF16)
        bq_ref[2 * p + 1] = jnp.where(upper, s, 0.0).astype(BF16)
    z = _head_norm(proj(COL_BK, 512), gkb_ref[...], bd)
    for p in range(4):
        bk_ref[p] = z[:, LANES * p:LANES * (p + 1)].astype(bk_ref.dtype)
    z = proj(COL_BV, 512)
    for p in range(4):
        bv_ref[p] = z[:, LANES * p:LANES * (p + 1)].astype(bv_ref.dtype)
    cq_ref[...] = (proj(COL_CQ, 512) * (C_DK ** -0.5)).astype(BF16)
    ck_ref[...] = proj(COL_CK, 512).astype(BF16)
    cv_ref[...] = proj(COL_CV, 1024).astype(BF16)
    cg_ref[...] = proj(COL_CG, 1024).astype(BF16)
    pre = jnp.dot(proj(COL_CR, 256).astype(BF16), wgk_ref[...], preferred_element_type=F32) + bgk_ref[...]
    la_ref[...] = (jnp.minimum(pre, 0.0) - jnp.log(1.0 + jnp.exp(-jnp.abs(pre)))) * (1.0 / 16.0)


def _inproj(x, mods, mod_row_of_tile, n1, w_in_b, consts, layer, *, rope, kv_dtype):
    t, d = x.shape
    tm = TM_PROJ
    seq_tiles = (GRID_W * GRID_ROWS) // tm
    tok = lambda i: (i, 0)
    full = lambda i: (0, 0)
    lay = lambda i: (layer, 0, 0)
    in_specs = [
        pl.BlockSpec((tm, d), tok),
        pl.BlockSpec((None, 1, d), lambda i: (mod_row_of_tile(i), 0, 0)),
        pl.BlockSpec((None, 1, d), lambda i: (mod_row_of_tile(i), 0, 1)),
        pl.BlockSpec((None, 1, d), lay),
        _resident((None, d, N_IN_PAD), lay),
        _resident((512, 512), full),
        pl.BlockSpec((None, 1, 512), lay), pl.BlockSpec((None, 1, LANES), lay),
        pl.BlockSpec((None, 1, 512), lay), pl.BlockSpec((None, 1, 512), lay),
        pl.BlockSpec((tm, LANES), lambda i: (i % seq_tiles, 0)),
        pl.BlockSpec((tm, LANES), lambda i: (i % seq_tiles, 0)),
        _resident((None, 256, 1024), lay),
        pl.BlockSpec((None, 1, 1024), lay),
    ]
    slab8 = pl.BlockSpec((8, tm, LANES), lambda i: (0, i, 0))
    slab4 = pl.BlockSpec((4, tm, LANES), lambda i: (0, i, 0))
    out_specs = [slab8, pl.BlockSpec((tm, LANES), tok), pl.BlockSpec((tm, LANES), tok),
                 slab8, slab4, slab4,
                 pl.BlockSpec((tm, 512), tok), pl.BlockSpec((tm, 512), tok),
                 pl.BlockSpec((tm, 1024), tok), pl.BlockSpec((tm, 1024), tok), pl.BlockSpec((tm, 1024), tok)]
    out_shape = [_sds((8, t, LANES), BF16), _sds((t, LANES), kv_dtype), _sds((t, LANES), kv_dtype),
                 _sds((8, t, LANES), BF16), _sds((4, t, LANES), kv_dtype), _sds((4, t, LANES), kv_dtype),
                 _sds((t, 512), BF16), _sds((t, 512), BF16), _sds((t, 1024), BF16), _sds((t, 1024), BF16),
                 _sds((t, 1024), F32)]
    return pl.pallas_call(
        functools.partial(_inproj_body, rope=rope),
        grid=(t // tm,), in_specs=in_specs, out_specs=out_specs, out_shape=out_shape,
        compiler_params=_params("arbitrary"), name="inproj",
    )(x, mods, mods, n1, w_in_b, consts["bd"], consts["gqa"], consts["gka"], consts["gqb"], consts["gkb"],
      consts["cos"], consts["sin"], consts["wgk"], consts["bgk"])


def _attn_body(*refs, kind, tq, kwin, n_seq, layer):
    is_a = kind in ("a_ctx", "a_lat")
    prefix = kind in ("a_lat", "b_lat")
    it = iter(refs)
    q_ref, k_ref, v_ref = next(it), next(it), next(it)
    kc_ref = vc_ref = sink_ref = tp_ref = None
    if prefix:
        kc_ref, vc_ref = next(it), next(it)
    if is_a:
        sink_ref = next(it)
    if kind == "b_lat":
        tp_ref = next(it)
    o_ref = next(it)
    i = pl.program_id(1)

    if kind == "a_lat":
        ws = jnp.clip(i * tq - A_WINDOW, 0, n_seq - kwin)
        ws = pl.multiple_of(ws, LANES)
        qpos = i * tq + lax.broadcasted_iota(I32, (tq, kwin), 0)
        kpos = ws + lax.broadcasted_iota(I32, (tq, kwin), 1)
        allowed = jnp.abs(qpos - kpos) <= A_WINDOW
    elif kind == "b_lat":
        r0 = i * (tq // GRID_W)
        k0 = jnp.clip(r0 - NA_ROWS // 2, 0, GRID_ROWS - kwin // GRID_W)
        ws = pl.multiple_of(k0 * GRID_W, GRID_W)
        qrow = r0 + lax.broadcasted_iota(I32, (tq, kwin), 0) // GRID_W
        krow = k0 + lax.broadcasted_iota(I32, (tq, kwin), 1) // GRID_W
        start = jnp.clip(qrow - NA_ROWS // 2, 0, GRID_ROWS - NA_ROWS)
        allowed = (krow >= start) & (krow < start + NA_ROWS)
        row_mask = jnp.where(allowed, 0.0, NEG_INF)
    else:
        ws = 0
    lane = lax.broadcasted_iota(I32, (tq, LANES), 1)
    upper = lane >= HEAD_DIM

    def head(h, ks, p):
        q = q_ref[h]
        kl = k_ref[ks, pl.ds(ws, kwin), :].astype(BF16)
        vl = v_ref[ks, pl.ds(ws, kwin), :].astype(BF16)
        s = lax.dot_general(q, kl, _TRANS_B, preferred_element_type=F32)
        if kind == "a_lat":
            s = jnp.where(allowed, s, NEG_INF)
        elif kind == "b_lat":
            rows = []
            for qr in range(tq // GRID_W):
                u0 = k0 - r0 - qr + 15
                rows.append(jnp.concatenate([tp_ref[h, u0 + 2 * m] for m in range(kwin // LANES)], axis=1))
            s = s + jnp.concatenate(rows, axis=0) + row_mask
        m = jnp.max(s, axis=-1, keepdims=True)
        if prefix:
            sc = lax.dot_general(q, kc_ref[ks], _TRANS_B, preferred_element_type=F32)
            m = jnp.maximum(m, jnp.max(sc, axis=-1, keepdims=True))
        if is_a:
            sink = sink_ref[layer, h]
            m = jnp.maximum(m, sink)
        e = jnp.exp(s - m)
        den = jnp.sum(e, axis=-1, keepdims=True)
        o = jnp.dot(e.astype(BF16), vl, preferred_element_type=F32)
        if prefix:
            ec = jnp.exp(sc - m)
            den = den + jnp.sum(ec, axis=-1, keepdims=True)
            o = o + jnp.dot(ec.astype(BF16), vc_ref[ks], preferred_element_type=F32)
        if is_a:
            den = den + jnp.exp(sink - m)
        return o * (1.0 / den)

    def pair(p):
        ks = 0 if is_a else p
        o0 = head(2 * p, ks, p)
        o1 = head(2 * p + 1, ks, p)
        if is_a:
            if p // 2 == 0:
                o1 = pltpu.roll(o1, HEAD_DIM, 1)
            else:
                o0 = pltpu.roll(o0, HEAD_DIM, 1)
        o_ref[p] = jnp.where(upper, o1, o0).astype(o_ref.dtype)

    if kind == "b_lat":
        def body(p, c):
            pair(p)
            return c
        lax.fori_loop(0, 4, body, 0)
    else:
        for p in range(4):
            pair(p)


def _attention(kind, q, k, v, n_seq, *, prefix=None, sink=None, tp=None, layer=0):
    t = q.shape[1]
    nb = t // n_seq
    s_k = k.shape[0]
    tq, kwin = {"a_ctx": (n_seq, n_seq), "b_ctx": (n_seq, n_seq),
                "a_lat": (A_WINDOW, 3 * A_WINDOW), "b_lat": (512, 1024)}[kind]
    nq = n_seq // tq
    in_specs = [pl.BlockSpec((8, tq, LANES), lambda b, i: (0, b * nq + i, 0)),
                pl.BlockSpec((s_k, n_seq, LANES), lambda b, i: (0, b, 0)),
                pl.BlockSpec((s_k, n_seq, LANES), lambda b, i: (0, b, 0))]
    args = [q, k, v]
    if prefix is not None:
        kc, vc = prefix
        spec = pl.BlockSpec((None,) + kc.shape[1:], lambda b, i: (b, 0, 0, 0))
        in_specs += [spec, spec]
        args += [kc, vc]
    if sink is not None:
        in_specs.append(pl.BlockSpec(memory_space=pltpu.SMEM))
        args.append(sink)
    if tp is not None:
        in_specs.append(_resident(tp.shape, lambda b, i: (0, 0, 0, 0)))
        args.append(tp)
    return pl.pallas_call(
        functools.partial(_attn_body, kind=kind, tq=tq, kwin=kwin, n_seq=n_seq, layer=layer),
        grid=(nb, nq), in_specs=in_specs,
        out_specs=pl.BlockSpec((4, tq, LANES), lambda b, i: (0, b * nq + i, 0)),
        out_shape=_sds((4, t, LANES), BF16),
        compiler_params=_params("arbitrary", "arbitrary"), name="attn_" + kind,
    )(*args)


def _gla_body(*refs, n_seq, init, emit_state):
    it = iter(refs)
    q_ref, k_ref, v_ref, g_ref, laf_ref, lab_ref, gn_ref = (next(it) for _ in range(7))
    s0f_ref = s0b_ref = sf_ref = sb_ref = None
    if init:
        s0f_ref, s0b_ref = next(it), next(it)
    o_ref = next(it)
    if emit_state:
        sf_ref, sb_ref = next(it), next(it)
    acc_ref, st_ref = next(it), next(it)
    nc = n_seq // C_CHUNK
    ri = lax.broadcasted_iota(I32, (C_CHUNK, C_CHUNK), 0)
    ci = lax.broadcasted_iota(I32, (C_CHUNK, C_CHUNK), 1)

    def direction(la_ref, s0_ref, s_out_ref, fwd):
        keep = (ci <= ri) if fwd else (ci >= ri)
        tri = jnp.where(keep, 1.0, 0.0).astype(BF16)
        st_ref[...] = s0_ref[...].T if init else jnp.zeros((C_DV, C_DK), F32)

        def chunk(j, carry):
            c = j if fwd else nc - 1 - j
            sl = pl.ds(pl.multiple_of(c * C_CHUNK, C_CHUNK), C_CHUNK)
            la = la_ref[sl, :]
            hi = la.astype(BF16)
            lo = (la - hi.astype(F32)).astype(BF16)
            b = jnp.dot(tri, hi, preferred_element_type=F32) + jnp.dot(tri, lo, preferred_element_type=F32)
            bl = b[C_CHUNK - 1:C_CHUNK, :] if fwd else b[0:1, :]
            q = q_ref[sl, :].astype(F32)
            k = k_ref[sl, :].astype(F32)
            v = v_ref[sl, :]
            qd = (q * jnp.exp(b)).astype(BF16)
            ki = (k * jnp.exp(-b)).astype(BF16)
            ke = (k * jnp.exp(bl - b)).astype(BF16)
            st = st_ref[...]
            inter = lax.dot_general(qd, st.astype(BF16), _TRANS_B, preferred_element_type=F32)
            att = lax.dot_general(qd, ki, _TRANS_B, preferred_element_type=F32)
            att = jnp.where(keep, att, 0.0)
            o = inter + jnp.dot(att.astype(BF16), v, preferred_element_type=F32)
            st_ref[...] = st * jnp.exp(bl) + lax.dot_general(v, ke, _TRANS_A, preferred_element_type=F32)
            if fwd:
                acc_ref[sl, :] = o
            else:
                tot = acc_ref[sl, :] + o
                y = tot * lax.rsqrt(jnp.mean(tot * tot, axis=-1, keepdims=True) + EPS) * gn_ref[...]
                g = g_ref[sl, :].astype(F32)
                o_ref[sl, :] = (y * (g * (1.0 / (1.0 + jnp.exp(-g))))).astype(o_ref.dtype)
            return carry

        lax.fori_loop(0, nc, chunk, 0)
        if emit_state:
            s_out_ref[...] = st_ref[...].T

    direction(laf_ref, s0f_ref, sf_ref, True)
    direction(lab_ref, s0b_ref, sb_ref, False)


def _gla(cq, ck, cv, cg, la, gn, n_seq, layer, *, s0=None, emit_state):
    t = cq.shape[0]
    nb = t // n_seq
    tok = lambda b, h: (b, h)
    in_specs = [pl.BlockSpec((n_seq, C_DK), tok), pl.BlockSpec((n_seq, C_DK), tok),
                pl.BlockSpec((n_seq, C_DV), tok), pl.BlockSpec((n_seq, C_DV), tok),
                pl.BlockSpec((n_seq, C_DK), tok), pl.BlockSpec((n_seq, C_DK), lambda b, h: (b, C_HEADS + h)),
                pl.BlockSpec((None, 1, C_DV), lambda b, h: (layer, 0, 0))]
    args = [cq, ck, cv, cg, la, la, gn]
    if s0 is not None:
        spec = pl.BlockSpec((None, None, None, C_DK, C_DV), lambda b, h: (b, layer, h, 0, 0))
        in_specs += [spec, spec]
        args += list(s0)
    out_specs = [pl.BlockSpec((n_seq, C_DV), tok)]
    out_shape = [_sds((t, C_HEADS * C_DV), BF16)]
    if emit_state:
        spec = pl.BlockSpec((None, None, C_DK, C_DV), lambda b, h: (b, h, 0, 0))
        out_specs += [spec, spec]
        out_shape += [_sds((nb, C_HEADS, C_DK, C_DV)), _sds((nb, C_HEADS, C_DK, C_DV))]
    return pl.pallas_call(
        functools.partial(_gla_body, n_seq=n_seq, init=s0 is not None, emit_state=emit_state),
        grid=(nb, C_HEADS), in_specs=in_specs, out_specs=out_specs, out_shape=out_shape,
        scratch_shapes=[pltpu.VMEM((n_seq, C_DV), F32), pltpu.VMEM((C_DV, C_DK), F32)],
        compiler_params=_params("arbitrary", "arbitrary"), name="gla",
    )(*args)


def _outproj_body(oa_ref, ob_ref, oc_ref, x_ref, g1_ref, sh_ref, sc_ref, n2_ref, w_ref, wr_ref,
                  x1_ref, h2_ref, r_ref):
    mix = jnp.concatenate([oa_ref[p] for p in range(4)] + [ob_ref[p] for p in range(4)] + [oc_ref[...]], axis=1)
    y = jnp.dot(mix, w_ref[...], preferred_element_type=F32)
    x1 = x_ref[...] + g1_ref[...] * y
    x1_ref[...] = x1
    h = x1 * lax.rsqrt(jnp.mean(x1 * x1, axis=-1, keepdims=True) + EPS) * n2_ref[...]
    h = h * (1.0 + sc_ref[...]) + sh_ref[...]
    h2_ref[...] = h
    lg = jnp.dot(h.astype(BF16), wr_ref[...], preferred_element_type=F32)
    lane = lax.broadcasted_iota(I32, lg.shape, 1)
    big = jnp.int32(LANES)
    is_g = lane < N_GROUPS
    gmax = jnp.max(jnp.where(is_g, lg, -jnp.inf), axis=-1, keepdims=True)
    gidx = jnp.min(jnp.where(is_g & (lg == gmax), lane, big), axis=-1, keepdims=True)
    gw = 1.0 / jnp.sum(jnp.where(is_g, jnp.exp(lg - gmax), 0.0), axis=-1, keepdims=True)
    lo = N_GROUPS + E_PER_GROUP * gidx
    sel = (lane >= lo) & (lane < lo + E_PER_GROUP)
    v1 = jnp.max(jnp.where(sel, lg, -jnp.inf), axis=-1, keepdims=True)
    i1 = jnp.min(jnp.where(sel & (lg == v1), lane, big), axis=-1, keepdims=True)
    sel2 = sel & (lane != i1)
    v2 = jnp.max(jnp.where(sel2, lg, -jnp.inf), axis=-1, keepdims=True)
    i2 = jnp.min(jnp.where(sel2 & (lg == v2), lane, big), axis=-1, keepdims=True)
    e21 = jnp.exp(v2 - v1)
    w1 = gw / (1.0 + e21)
    w2 = gw * e21 / (1.0 + e21)
    out = jnp.where(lane == 0, (i1 - N_GROUPS).astype(F32),
                    jnp.where(lane == 1, (i2 - N_GROUPS).astype(F32),
                              jnp.where(lane == 2, w1, jnp.where(lane == 3, w2, 0.0))))
    r_ref[...] = out


def _outproj(oa, ob, oc, x, mods, mod_row_of_tile, n2, w_out_b, wr_b, layer):
    t, d = x.shape
    tm = TM_PROJ
    tok = lambda i: (i, 0)
    lay = lambda i: (layer, 0, 0)
    mod = lambda k: pl.BlockSpec((None, 1, d), lambda i: (mod_row_of_tile(i), 0, k))
    slab4 = pl.BlockSpec((4, tm, LANES), lambda i: (0, i, 0))
    return pl.pallas_call(
        _outproj_body,
        grid=(t // tm,),
        in_specs=[slab4, slab4, pl.BlockSpec((tm, 1024), tok), pl.BlockSpec((tm, d), tok),
                  mod(2), mod(3), mod(4), pl.BlockSpec((None, 1, d), lay),
                  _resident((None, d, d), lay), _resident((None, d, LANES), lay)],
        out_specs=[pl.BlockSpec((tm, d), tok), pl.BlockSpec((tm, d), tok), pl.BlockSpec((tm, LANES), tok)],
        out_shape=[_sds((t, d)), _sds((t, d)), _sds((t, LANES))],
        compiler_params=_params("arbitrary"), name="outproj",
    )(oa, ob, oc, x, mods, mods, mods, n2, w_out_b, wr_b)


def _moe_body(te_ref, nt_ref, src_ref, srcn_ref, dst_ref, h_ref, w1_ref, w3_ref, w2_ref, y_ref,
              xbuf, ybuf, gsem, ssem, *, n_tiles):
    tm = xbuf.shape[1]
    r = pl.program_id(0)
    nt = nt_ref[0]
    slot = r % 2

    def gather_rows(idx_ref, s):
        def issue(j, c):
            pltpu.make_async_copy(h_ref.at[pl.ds(idx_ref[0, 0, j], 1)], xbuf.at[s, pl.ds(j, 1)], gsem.at[s]).start()
            return c
        lax.fori_loop(0, tm, issue, 0, unroll=8)

    def all_rows_in(s):
        return pltpu.make_async_copy(h_ref.at[pl.ds(0, tm)], xbuf.at[s], gsem.at[s])

    def all_rows_out(s):
        return pltpu.make_async_copy(ybuf.at[s], y_ref.at[pl.ds(0, tm)], ssem.at[s])

    @pl.when(r == 0)
    def _():
        ybuf[...] = jnp.zeros(ybuf.shape, F32)
        n_pairs = y_ref.shape[0] - 2 * tm
        for s in range(2):
            spare = pltpu.make_async_copy(ybuf.at[s], y_ref.at[pl.ds(n_pairs + s * tm, tm)], ssem.at[s])
            spare.start()
            spare.wait()
        gather_rows(src_ref, 0)

    @pl.when(r + 1 < nt)
    def _():
        gather_rows(srcn_ref, 1 - slot)

    @pl.when(r < nt)
    def _():
        all_rows_in(slot).wait()

        @pl.when(r >= 2)
        def _():
            all_rows_out(slot).wait()

        x = xbuf[slot].astype(BF16)
        h1 = jnp.dot(x, w1_ref[...], preferred_element_type=F32)
        h3 = jnp.dot(x, w3_ref[...], preferred_element_type=F32)
        hid = (h1 * (1.0 / (1.0 + jnp.exp(-h1))) * h3).astype(BF16)
        ybuf[slot] = jnp.dot(hid, w2_ref[...], preferred_element_type=F32)

        def issue(j, c):
            pltpu.make_async_copy(ybuf.at[slot, pl.ds(j, 1)], y_ref.at[pl.ds(dst_ref[0, 0, j], 1)], ssem.at[slot]).start()
            return c
        lax.fori_loop(0, tm, issue, 0, unroll=8)

    @pl.when(r == n_tiles - 1)
    def _():
        @pl.when(nt >= 1)
        def _():
            all_rows_out((nt - 1) % 2).wait()

        @pl.when(nt >= 2)
        def _():
            all_rows_out(nt % 2).wait()


def _moe(h2, route, w1b, w3b, w2b, layer):
    t, d = h2.shape
    tm = TM_MOE
    npair = 2 * t
    n_rows = npair + N_EXPERTS * tm
    n_tiles = n_rows // tm
    ef = route[:, :2].astype(I32).T.reshape(-1)
    order = jnp.argsort(ef, stable=True).astype(I32)
    counts = jnp.sum((ef[:, None] == jnp.arange(N_EXPERTS, dtype=I32)[None, :]).astype(I32), axis=0)
    gstart = jnp.cumsum(counts) - counts
    pcount = ((counts + tm - 1) // tm) * tm
    pend = jnp.cumsum(pcount)
    pstart = pend - pcount
    row = jnp.arange(n_rows, dtype=I32)
    er = jnp.minimum(jnp.sum((row[:, None] >= pend[None, :]).astype(I32), axis=1), N_EXPERTS - 1)
    j = row - pstart[er]
    valid = (j < counts[er]) & (row < pend[-1])
    pair = order[jnp.clip(gstart[er] + j, 0, npair - 1)]
    src = jnp.where(valid, pair % t, 0).reshape(n_tiles, 1, tm)
    dump = npair + ((row // tm) % 2) * tm + row % tm
    dst = jnp.where(valid, pair, dump).reshape(n_tiles, 1, tm)
    nt = (pend[-1] // tm).reshape(1)
    tile_e = er[::tm]
    tile_e = jnp.where(jnp.arange(n_tiles) < nt[0], tile_e, tile_e[jnp.maximum(nt[0] - 1, 0)])

    smem = lambda f: pl.BlockSpec((1, 1, tm), f, memory_space=pltpu.SMEM)
    grid_spec = pltpu.PrefetchScalarGridSpec(
        num_scalar_prefetch=2, grid=(n_tiles,),
        in_specs=[smem(lambda r, te, n: (r, 0, 0)),
                  smem(lambda r, te, n: (jnp.minimum(r + 1, n_tiles - 1), 0, 0)),
                  smem(lambda r, te, n: (r, 0, 0)),
                  pl.BlockSpec(memory_space=pl.ANY),
                  pl.BlockSpec((None, None, d, D_EXPERT), lambda r, te, n: (layer, te[r], 0, 0)),
                  pl.BlockSpec((None, None, d, D_EXPERT), lambda r, te, n: (layer, te[r], 0, 0)),
                  pl.BlockSpec((None, None, D_EXPERT, d), lambda r, te, n: (layer, te[r], 0, 0))],
        out_specs=pl.BlockSpec(memory_space=pl.ANY),
        scratch_shapes=[pltpu.VMEM((2, tm, d), F32), pltpu.VMEM((2, tm, d), F32),
                        pltpu.SemaphoreType.DMA((2,)), pltpu.SemaphoreType.DMA((2,))])
    return pl.pallas_call(
        functools.partial(_moe_body, n_tiles=n_tiles),
        grid_spec=grid_spec, out_shape=_sds((npair + 2 * tm, d)),
        compiler_params=_params("arbitrary"), name="moe",
    )(tile_e, nt, src, src, dst, h2, w1b, w3b, w2b)


def _combine_body(x1_ref, y0_ref, y1_ref, r_ref, g2_ref, o_ref):
    r = r_ref[...]
    lane = lax.broadcasted_iota(I32, r.shape, 1)
    w1 = jnp.sum(jnp.where(lane == 2, r, 0.0), axis=-1, keepdims=True)
    w2 = jnp.sum(jnp.where(lane == 3, r, 0.0), axis=-1, keepdims=True)
    o_ref[...] = x1_ref[...] + g2_ref[...] * (w1 * y0_ref[...] + w2 * y1_ref[...])


def _combine(x1, yg, route, mods, mod_row_of_tile):
    t, d = x1.shape
    tm = TM_PROJ
    nt = t // tm
    tok = lambda i: (i, 0)
    return pl.pallas_call(
        _combine_body,
        grid=(nt,),
        in_specs=[pl.BlockSpec((tm, d), tok), pl.BlockSpec((tm, d), tok),
                  pl.BlockSpec((tm, d), lambda i: (nt + i, 0)), pl.BlockSpec((tm, LANES), tok),
                  pl.BlockSpec((None, 1, d), lambda i: (mod_row_of_tile(i), 0, 5))],
        out_specs=pl.BlockSpec((tm, d), tok), out_shape=_sds((t, d)),
        compiler_params=_params("arbitrary"), name="combine",
    )(x1, yg, yg, route, mods)


def _rope_tables():
    t = jnp.arange(GRID_W * GRID_ROWS)
    pairs = HEAD_DIM // 4
    inv = 1.0 / (ROPE_THETA ** (jnp.arange(pairs, dtype=F32) * 2.0 / (HEAD_DIM // 2)))
    ang_r = (t // GRID_W).astype(F32)[:, None] * inv
    ang_c = (t % GRID_W).astype(F32)[:, None] * inv
    cos = jnp.concatenate([jnp.cos(ang_r)] * 2 + [jnp.cos(ang_c)] * 2, axis=1)
    sin = jnp.concatenate([-jnp.sin(ang_r), jnp.sin(ang_r), -jnp.sin(ang_c), jnp.sin(ang_c)], axis=1)
    return jnp.tile(cos, (1, 2)), jnp.tile(sin, (1, 2))


def _bias_pair_tiles(rpb_l):
    qc = jnp.arange(GRID_W)[:, None]
    kc = jnp.arange(GRID_W)[None, :]
    start_c = jnp.clip(qc - NA_COLS // 2, 0, GRID_W - NA_COLS)
    col_ok = (kc >= start_c) & (kc < start_c + NA_COLS)
    dc = jnp.clip(kc - qc + (NA_COLS - 1), 0, 2 * NA_COLS - 2)
    d = jnp.arange(-8, 23)
    tile = rpb_l[:, jnp.clip(d, 0, 2 * NA_ROWS - 2)][:, :, dc]
    ok = col_ok[None, None] & ((d >= 0) & (d <= 2 * NA_ROWS - 2))[None, :, None, None]
    tile = jnp.where(ok, tile, NEG_INF).astype(F32)
    return jnp.concatenate([tile[:, :-1], tile[:, 1:]], axis=-1)


def kernel(x_prompt, x_sample, c, cache_a_k, cache_a_v, cache_b_k, cache_b_v, state_c_fwd, state_c_bwd, c_ctx, norm1_g, norm2_g, w_mod, b_mod, w_in, q_norm_a, k_norm_a, sink_a, q_norm_b, k_norm_b, rpb_b, w_gk_up_f, b_gk_f, w_gk_up_b, b_gk_b, gla_norm_g, w_out, w_group, w_router, w1, w3, w2):
    depth = w_in.shape[0]
    nb_c, l_c, d = x_prompt.shape
    nb_s, n_s, _ = x_sample.shape
    past = cache_a_k.shape[2]
    tm = TM_PROJ

    w_in_b = jnp.pad(w_in, ((0, 0), (0, 0), (0, N_IN_PAD - N_IN))).astype(BF16)
    w_out_b = w_out.astype(BF16)
    wr_b = jnp.pad(jnp.concatenate([w_group, w_router], axis=-1),
                   ((0, 0), (0, 0), (0, LANES - N_GROUPS - N_EXPERTS))).astype(BF16)
    w1b, w3b, w2b = w1.astype(BF16), w3.astype(BF16), w2.astype(BF16)
    wgk = jnp.zeros((depth, 256, 1024), F32)
    wgk = wgk.at[:, :C_RANK, :512].set(w_gk_up_f).at[:, C_RANK:2 * C_RANK, 512:].set(w_gk_up_b).astype(BF16)
    cos, sin = _rope_tables()
    head_id = jnp.arange(512) // HEAD_DIM
    consts = dict(
        bd=(head_id[:, None] == head_id[None, :]).astype(BF16),
        gqa=jnp.tile(q_norm_a, (1, 8)).reshape(depth, 1, 512),
        gka=jnp.tile(k_norm_a, (1, 2)).reshape(depth, 1, LANES),
        gqb=jnp.tile(q_norm_b, (1, 8)).reshape(depth, 1, 512),
        gkb=jnp.tile(k_norm_b, (1, 8)).reshape(depth, 1, 512),
        cos=cos, sin=sin, wgk=wgk,
        bgk=jnp.concatenate([b_gk_f, b_gk_b], axis=-1).reshape(depth, 1, 1024))
    n1 = norm1_g.reshape(depth, 1, d)
    n2 = norm2_g.reshape(depth, 1, d)
    gn = gla_norm_g.reshape(depth, 1, C_DV)
    cond = jnp.zeros((16, d), F32).at[0].set(c_ctx).at[1:1 + nb_s].set(c)
    ctx_row = lambda i: 0
    lat_row = lambda i: 1 + i // (n_s // tm)
    cak = cache_a_k.reshape(nb_s, depth, 1, past, LANES).astype(BF16)
    cav = cache_a_v.reshape(nb_s, depth, 1, past, LANES).astype(BF16)
    cbk = cache_b_k.reshape(nb_s, depth, past, 4, LANES).transpose(0, 1, 3, 2, 4).astype(BF16)
    cbv = cache_b_v.reshape(nb_s, depth, past, 4, LANES).transpose(0, 1, 3, 2, 4).astype(BF16)

    xp = x_prompt.reshape(nb_c * l_c, d)
    xs = x_sample.reshape(nb_s * n_s, d)
    new = [[] for _ in range(6)]
    for l in range(depth):
        mods = _adaln(cond, w_mod, b_mod, l).reshape(16, 1, 6 * d)
        tp = _bias_pair_tiles(rpb_b[l])

        aq, ak, av, bq, bk, bv, cq, ck, cv, cg, la = _inproj(
            xp, mods, ctx_row, n1, w_in_b, consts, l, rope=False, kv_dtype=F32)
        oa = _attention("a_ctx", aq, ak.reshape(1, -1, LANES), av.reshape(1, -1, LANES), l_c, sink=sink_a, layer=l)
        ob = _attention("b_ctx", bq, bk, bv, l_c)
        oc, sf, sb = _gla(cq, ck, cv, cg, la, gn, l_c, l, emit_state=True)
        x1, h2, route = _outproj(oa, ob, oc, xp, mods, ctx_row, n2, w_out_b, wr_b, l)
        xp = _combine(x1, _moe(h2, route, w1b, w3b, w2b, l), route, mods, ctx_row)
        new[0].append(ak.reshape(nb_c, l_c, 2, HEAD_DIM))
        new[1].append(av.reshape(nb_c, l_c, 2, HEAD_DIM))
        new[2].append(bk.transpose(1, 0, 2).reshape(nb_c, l_c, 8, HEAD_DIM))
        new[3].append(bv.transpose(1, 0, 2).reshape(nb_c, l_c, 8, HEAD_DIM))
        new[4].append(sf)
        new[5].append(sb)

        aq, ak, av, bq, bk, bv, cq, ck, cv, cg, la = _inproj(
            xs, mods, lat_row, n1, w_in_b, consts, l, rope=True, kv_dtype=BF16)
        oa = _attention("a_lat", aq, ak.reshape(1, -1, LANES), av.reshape(1, -1, LANES), n_s,
                        prefix=(cak[:, l], cav[:, l]), sink=sink_a, layer=l)
        ob = _attention("b_lat", bq, bk, bv, n_s, prefix=(cbk[:, l], cbv[:, l]), tp=tp)
        oc, = _gla(cq, ck, cv, cg, la, gn, n_s, l, s0=(state_c_fwd, state_c_bwd), emit_state=False)
        x1, h2, route = _outproj(oa, ob, oc, xs, mods, lat_row, n2, w_out_b, wr_b, l)
        xs = _combine(x1, _moe(h2, route, w1b, w3b, w2b, l), route, mods, lat_row)

    outs = [jnp.stack(v, axis=1) for v in new]
    return (xp.reshape(nb_c, l_c, d), xs.reshape(nb_s, n_s, d), *outs)
```

```python
import functools

import jax
import jax.numpy as jnp
from jax import lax
from jax.experimental import pallas as pl
from jax.experimental.pallas import tpu as pltpu

F32 = jnp.float32
BF16 = jnp.bfloat16
I32 = jnp.int32

D_MODEL = 2048
HEAD_DIM = 64
EPS = 1e-6
NEG_INF = -1e30
ROPE_THETA = 10000.0
GRID_W = 64
GRID_ROWS = 32
A_WINDOW = 128
NA_ROWS = 8
NA_COLS = 16
C_HEADS = 4
C_DK = 128
C_DV = 256
C_RANK = 16
C_CHUNK = 64
GLA_BLOCK = 256
N_GROUPS = 4
E_PER_GROUP = 4
N_EXPERTS = 16
D_EXPERT = 512
LANES = 128
N_IN = 5408
N_IN_PAD = 5632
COL_AQ, COL_AKV, COL_BQ, COL_BK, COL_BV = 0, 512, 768, 1280, 1792
COL_CQ, COL_CK, COL_CV, COL_CG, COL_CR = 2304, 2816, 3328, 4352, 5376
VMEM_LIMIT = 56 * 1024 * 1024
TM_PROJ = 256
TM_MOE = 256

_TRANS_B = (((1,), (1,)), ((), ()))
_TRANS_A = (((0,), (0,)), ((), ()))


def _sds(shape, dtype=F32):
    return jax.ShapeDtypeStruct(shape, dtype)


def _params(*sem):
    return pltpu.CompilerParams(dimension_semantics=sem, vmem_limit_bytes=VMEM_LIMIT)


def _resident(shape, index_map):
    return pl.BlockSpec(shape, index_map, pipeline_mode=pl.Buffered(1))


def _adaln_body(c_ref, w_ref, b_ref, o_ref):
    c = c_ref[...]
    s = c * (1.0 / (1.0 + jnp.exp(-c)))
    o_ref[...] = jnp.dot(s.astype(BF16), w_ref[...].astype(BF16), preferred_element_type=F32) + b_ref[...]


def _adaln(cond, w_mod, b_mod, layer):
    d, n = w_mod.shape[1], w_mod.shape[2]
    tn = 1024
    return pl.pallas_call(
        _adaln_body,
        grid=(n // tn,),
        in_specs=[pl.BlockSpec((16, d), lambda j: (0, 0)),
                  pl.BlockSpec((None, d, tn), lambda j: (layer, 0, j)),
                  pl.BlockSpec((None, 1, tn), lambda j: (layer, 0, j))],
        out_specs=pl.BlockSpec((16, tn), lambda j: (0, j)),
        out_shape=_sds((16, n)),
        compiler_params=_params("arbitrary"),
        name="adaln",
    )(cond, w_mod, b_mod.reshape(b_mod.shape[0], 1, n))


def _head_norm(z, gain, bd):
    w = z.shape[1]
    ss = jnp.dot((z * z).astype(BF16), bd[:w, :w], preferred_element_type=F32)
    return z * lax.rsqrt(ss * (1.0 / HEAD_DIM) + EPS) * gain


def _rope(z, cos, sin_signed, lane):
    lower = (lane % 32) < 16
    partner = jnp.where(lower, pltpu.roll(z, LANES - 16, 1), pltpu.roll(z, 16, 1))
    return z * cos + partner * sin_signed


def _inproj_body(x_ref, sh_ref, sc_ref, n1_ref, w_ref, bd_ref, gqa_ref, gka_ref, gqb_ref, gkb_ref,
                 cos_ref, sin_ref, wgk_ref, bgk_ref,
                 aq_ref, ak_ref, av_ref, bq_ref, bk_ref, bv_ref, cq_ref, ck_ref, cv_ref, cg_ref, la_ref,
                 *, rope):
    x = x_ref[...]
    h = x * lax.rsqrt(jnp.mean(x * x, axis=-1, keepdims=True) + EPS) * n1_ref[...]
    h = h * (1.0 + sc_ref[...]) + sh_ref[...]
    hb = h.astype(BF16)
    bd = bd_ref[...]
    lane = lax.broadcasted_iota(I32, (x.shape[0], LANES), 1)
    upper = lane >= HEAD_DIM

    def proj(c0, n):
        return jnp.dot(hb, w_ref[:, c0:c0 + n], preferred_element_type=F32)

    def rot(s):
        return _rope(s, cos_ref[...], sin_ref[...], lane) if rope else s

    z = _head_norm(proj(COL_AQ, 512), gqa_ref[...], bd)
    for p in range(4):
        s = rot(z[:, LANES * p:LANES * (p + 1)]) * (HEAD_DIM ** -0.5)
        r = pltpu.roll(s, HEAD_DIM, 1)
        if p // 2 == 0:
            e0, e1 = jnp.where(upper, 0.0, s), jnp.where(upper, 0.0, r)
        else:
            e0, e1 = jnp.where(upper, r, 0.0), jnp.where(upper, s, 0.0)
        aq_ref[2 * p] = e0.astype(BF16)
        aq_ref[2 * p + 1] = e1.astype(BF16)
    z = proj(COL_AKV, 256)
    ak_ref[...] = rot(_head_norm(z[:, :LANES], gka_ref[...], bd)).astype(ak_ref.dtype)
    av_ref[...] = z[:, LANES:].astype(av_ref.dtype)
    z = _head_norm(proj(COL_BQ, 512), gqb_ref[...], bd) * (HEAD_DIM ** -0.5)
    for p in range(4):
        s = z[:, LANES * p:LANES * (p + 1)]
        bq_ref[2 * p] = jnp.where(upper, 0.0, s).astype(BF16)
        bq_ref[2 * p + 1] = jnp.where(upper, s, 0.0).astype(BF16)
    z = _head_norm(proj(COL_BK, 512), gkb_ref[...], bd)
    for p in range(4):
        bk_ref[p] = z[:, LANES * p:LANES * (p + 1)].astype(bk_ref.dtype)
    z = proj(COL_BV, 512)
    for p in range(4):
        bv_ref[p] = z[:, LANES * p:LANES * (p + 1)].astype(bv_ref.dtype)
    cq_ref[...] = (proj(COL_CQ, 512) * (C_DK ** -0.5)).astype(BF16)
    ck_ref[...] = proj(COL_CK, 512).astype(BF16)
    cv_ref[...] = proj(COL_CV, 1024).astype(BF16)
    cg_ref[...] = proj(COL_CG, 1024).astype(BF16)
    pre = jnp.dot(proj(COL_CR, 256).astype(BF16), wgk_ref[...], preferred_element_type=F32) + bgk_ref[...]
    la_ref[...] = (jnp.minimum(pre, 0.0) - jnp.log(1.0 + jnp.exp(-jnp.abs(pre)))) * (1.0 / 16.0)


def _inproj(x, mods, mod_row_of_tile, n1, w_in_b, consts, layer, *, rope, kv_dtype):
    t, d = x.shape
    tm = TM_PROJ
    seq_tiles = (GRID_W * GRID_ROWS) // tm
    tok = lambda i: (i, 0)
    full = lambda i: (0, 0)
    lay = lambda i: (layer, 0, 0)
    in_specs = [
        pl.BlockSpec((tm, d), tok),
        pl.BlockSpec((None, 1, d), lambda i: (mod_row_of_tile(i), 0, 0)),
        pl.BlockSpec((None, 1, d), lambda i: (mod_row_of_tile(i), 0, 1)),
        pl.BlockSpec((None, 1, d), lay),
        _resident((None, d, N_IN_PAD), lay),
        _resident((512, 512), full),
        pl.BlockSpec((None, 1, 512), lay), pl.BlockSpec((None, 1, LANES), lay),
        pl.BlockSpec((None, 1, 512), lay), pl.BlockSpec((None, 1, 512), lay),
        pl.BlockSpec((tm, LANES), lambda i: (i % seq_tiles, 0)),
        pl.BlockSpec((tm, LANES), lambda i: (i % seq_tiles, 0)),
        _resident((None, 256, 1024), lay),
        pl.BlockSpec((None, 1, 1024), lay),
    ]
    slab8 = pl.BlockSpec((8, tm, LANES), lambda i: (0, i, 0))
    slab4 = pl.BlockSpec((4, tm, LANES), lambda i: (0, i, 0))
    out_specs = [slab8, pl.BlockSpec((tm, LANES), tok), pl.BlockSpec((tm, LANES), tok),
                 slab8, slab4, slab4,
                 pl.BlockSpec((tm, 512), tok), pl.BlockSpec((tm, 512), tok),
                 pl.BlockSpec((tm, 1024), tok), pl.BlockSpec((tm, 1024), tok), pl.BlockSpec((tm, 1024), tok)]
    out_shape = [_sds((8, t, LANES), BF16), _sds((t, LANES), kv_dtype), _sds((t, LANES), kv_dtype),
                 _sds((8, t, LANES), BF16), _sds((4, t, LANES), kv_dtype), _sds((4, t, LANES), kv_dtype),
                 _sds((t, 512), BF16), _sds((t, 512), BF16), _sds((t, 1024), BF16), _sds((t, 1024), BF16),
                 _sds((t, 1024), F32)]
    return pl.pallas_call(
        functools.partial(_inproj_body, rope=rope),
        grid=(t // tm,), in_specs=in_specs, out_specs=out_specs, out_shape=out_shape,
        compiler_params=_params("arbitrary"), name="inproj",
    )(x, mods, mods, n1, w_in_b, consts["bd"], consts["gqa"], consts["gka"], consts["gqb"], consts["gkb"],
      consts["cos"], consts["sin"], consts["wgk"], consts["bgk"])


def _attn_body(*refs, kind, tq, kwin, n_seq, layer):
    is_a = kind in ("a_ctx", "a_lat")
    prefix = kind in ("a_lat", "b_lat")
    it = iter(refs)
    q_ref, k_ref, v_ref = next(it), next(it), next(it)
    kc_ref = vc_ref = sink_ref = tp_ref = None
    if prefix:
        kc_ref, vc_ref = next(it), next(it)
    if is_a:
        sink_ref = next(it)
    if kind == "b_lat":
        tp_ref = next(it)
    o_ref = next(it)
    i = pl.program_id(1)

    if kind == "a_lat":
        ws = jnp.clip(i * tq - A_WINDOW, 0, n_seq - kwin)
        ws = pl.multiple_of(ws, LANES)
        qpos = i * tq + lax.broadcasted_iota(I32, (tq, kwin), 0)
        kpos = ws + lax.broadcasted_iota(I32, (tq, kwin), 1)
        allowed = jnp.abs(qpos - kpos) <= A_WINDOW
    elif kind == "b_lat":
        r0 = i * (tq // GRID_W)
        k0 = jnp.clip(r0 - NA_ROWS // 2, 0, GRID_ROWS - kwin // GRID_W)
        ws = pl.multiple_of(k0 * GRID_W, GRID_W)
        qrow = r0 + lax.broadcasted_iota(I32, (tq, kwin), 0) // GRID_W
        krow = k0 + lax.broadcasted_iota(I32, (tq, kwin), 1) // GRID_W
        start = jnp.clip(qrow - NA_ROWS // 2, 0, GRID_ROWS - NA_ROWS)
        allowed = (krow >= start) & (krow < start + NA_ROWS)
        row_mask = jnp.where(allowed, 0.0, NEG_INF)
    else:
        ws = 0
    lane = lax.broadcasted_iota(I32, (tq, LANES), 1)
    upper = lane >= HEAD_DIM

    def head(h, ks, p):
        q = q_ref[h]
        kl = k_ref[ks, pl.ds(ws, kwin), :].astype(BF16)
        vl = v_ref[ks, pl.ds(ws, kwin), :].astype(BF16)
        s = lax.dot_general(q, kl, _TRANS_B, preferred_element_type=F32)
        if kind == "a_lat":
            s = jnp.where(allowed, s, NEG_INF)
        elif kind == "b_lat":
            rows = []
            for qr in range(tq // GRID_W):
                u0 = k0 - r0 - qr + 15
                rows.append(jnp.concatenate([tp_ref[h, u0 + 2 * m] for m in range(kwin // LANES)], axis=1))
            s = s + jnp.concatenate(rows, axis=0) + row_mask
        m = jnp.max(s, axis=-1, keepdims=True)
        if prefix:
            sc = lax.dot_general(q, kc_ref[ks], _TRANS_B, preferred_element_type=F32)
            m = jnp.maximum(m, jnp.max(sc, axis=-1, keepdims=True))
        if is_a:
            sink = sink_ref[layer, h]
            m = jnp.maximum(m, sink)
        e = jnp.exp(s - m)
        den = jnp.sum(e, axis=-1, keepdims=True)
        o = jnp.dot(e.astype(BF16), vl, preferred_element_type=F32)
        if prefix:
            ec = jnp.exp(sc - m)
            den = den + jnp.sum(ec, axis=-1, keepdims=True)
            o = o + jnp.dot(ec.astype(BF16), vc_ref[ks], preferred_element_type=F32)
        if is_a:
            den = den + jnp.exp(sink - m)
        return o * (1.0 / den)

    def pair(p):
        ks = 0 if is_a else p
        o0 = head(2 * p, ks, p)
        o1 = head(2 * p + 1, ks, p)
        if is_a:
            if p // 2 == 0:
                o1 = pltpu.roll(o1, HEAD_DIM, 1)
            else:
                o0 = pltpu.roll(o0, HEAD_DIM, 1)
        o_ref[p] = jnp.where(upper, o1, o0).astype(o_ref.dtype)

    if kind == "b_lat":
        def body(p, c):
            pair(p)
            return c
        lax.fori_loop(0, 4, body, 0)
    else:
        for p in range(4):
            pair(p)


def _attention(kind, q, k, v, n_seq, *, prefix=None, sink=None, tp=None, layer=0):
    t = q.shape[1]
    nb = t // n_seq
    s_k = k.shape[0]
    tq, kwin = {"a_ctx": (n_seq, n_seq), "b_ctx": (n_seq, n_seq),
                "a_lat": (A_WINDOW, 3 * A_WINDOW), "b_lat": (512, 1024)}[kind]
    nq = n_seq // tq
    in_specs = [pl.BlockSpec((8, tq, LANES), lambda b, i: (0, b * nq + i, 0)),
                pl.BlockSpec((s_k, n_seq, LANES), lambda b, i: (0, b, 0)),
                pl.BlockSpec((s_k, n_seq, LANES), lambda b, i: (0, b, 0))]
    args = [q, k, v]
    if prefix is not None:
        kc, vc = prefix
        spec = pl.BlockSpec((None,) + kc.shape[1:], lambda b, i: (b, 0, 0, 0))
        in_specs += [spec, spec]
        args += [kc, vc]
    if sink is not None:
        in_specs.append(pl.BlockSpec(memory_space=pltpu.SMEM))
        args.append(sink)
    if tp is not None:
        in_specs.append(_resident(tp.shape, lambda b, i: (0, 0, 0, 0)))
        args.append(tp)
    return pl.pallas_call(
        functools.partial(_attn_body, kind=kind, tq=tq, kwin=kwin, n_seq=n_seq, layer=layer),
        grid=(nb, nq), in_specs=in_specs,
        out_specs=pl.BlockSpec((4, tq, LANES), lambda b, i: (0, b * nq + i, 0)),
        out_shape=_sds((4, t, LANES), BF16),
        compiler_params=_params("arbitrary", "arbitrary"), name="attn_" + kind,
    )(*args)


def _gla_body(*refs, n_seq, init, emit_state):
    it = iter(refs)
    q_ref, k_ref, v_ref, g_ref, laf_ref, lab_ref, gn_ref = (next(it) for _ in range(7))
    s0f_ref = s0b_ref = sf_ref = sb_ref = None
    if init:
        s0f_ref, s0b_ref = next(it), next(it)
    o_ref = next(it)
    if emit_state:
        sf_ref, sb_ref = next(it), next(it)
    acc_ref, st_ref, qdf_ref, qdb_ref, dsf_ref, dsb_ref, decf_ref, decb_ref = (next(it) for _ in range(8))
    nc = n_seq // C_CHUNK
    per = GLA_BLOCK // C_CHUNK
    row = lax.broadcasted_iota(I32, (GLA_BLOCK, C_DK), 0) % C_CHUNK
    ri = lax.broadcasted_iota(I32, (GLA_BLOCK, GLA_BLOCK), 0)
    ci = lax.broadcasted_iota(I32, (GLA_BLOCK, GLA_BLOCK), 1)
    same = (ri // C_CHUNK) == (ci // C_CHUNK)
    zero_chunk = jnp.zeros((C_CHUNK, C_DK), BF16)

    def decay_sums(la, fwd):
        x = la
        for s in (1, 2, 4, 8, 16, 32):
            if fwd:
                x = x + jnp.where(row >= s, pltpu.roll(x, s, 0), 0.0)
            else:
                x = x + jnp.where(row < C_CHUNK - s, pltpu.roll(x, GLA_BLOCK - s, 0), 0.0)
        return x

    def block(u, carry):
        sl = pl.ds(pl.multiple_of(u * GLA_BLOCK, GLA_BLOCK), GLA_BLOCK)
        q = q_ref[sl, :].astype(F32)
        k = k_ref[sl, :].astype(F32)
        v = v_ref[sl, :]
        att = None
        rhs = []
        for fwd, la_ref, qd_ref, dec_ref in ((True, laf_ref, qdf_ref, decf_ref), (False, lab_ref, qdb_ref, decb_ref)):
            b = decay_sums(la_ref[sl, :], fwd)
            b3 = b.reshape(per, C_CHUNK, C_DK)
            bl3 = b3[:, C_CHUNK - 1:C_CHUNK, :] if fwd else b3[:, 0:1, :]
            qd = (q * jnp.exp(b)).astype(BF16)
            ki = (k * jnp.exp(-b)).astype(BF16)
            ke3 = (k.reshape(per, C_CHUNK, C_DK) * jnp.exp(bl3 - b3)).astype(BF16)
            qd_ref[sl, :] = qd
            dec = jnp.exp(bl3)
            for j in range(per):
                dec_ref[u * per + j, 0:1, :] = dec[j]
            a = lax.dot_general(qd, ki, _TRANS_B, preferred_element_type=F32)
            a = jnp.where(same & ((ci <= ri) if fwd else (ci >= ri)), a, 0.0)
            att = a if att is None else att + a
            rhs += [jnp.concatenate([ke3[i] if i == j else zero_chunk for i in range(per)], axis=0)
                    for j in range(per)]
        acc_ref[sl, :] = jnp.dot(att.astype(BF16), v, preferred_element_type=F32)
        ds = lax.dot_general(v, jnp.concatenate(rhs, axis=1), _TRANS_A, preferred_element_type=F32)
        for j in range(per):
            dsf_ref[u * per + j] = ds[:, C_DK * j:C_DK * (j + 1)]
            dsb_ref[u * per + j] = ds[:, C_DK * (per + j):C_DK * (per + j + 1)]
        return carry

    lax.fori_loop(0, n_seq // GLA_BLOCK, block, 0)

    def sweep(qd_ref, ds_ref, dec_ref, s0_ref, s_out_ref, fwd):
        st_ref[...] = s0_ref[...].T if init else jnp.zeros((C_DV, C_DK), F32)

        def chunk(j, carry):
            c = j if fwd else nc - 1 - j
            sl = pl.ds(pl.multiple_of(c * C_CHUNK, C_CHUNK), C_CHUNK)
            st = st_ref[...]
            tot = acc_ref[sl, :] + lax.dot_general(qd_ref[sl, :], st.astype(BF16), _TRANS_B,
                                                   preferred_element_type=F32)
            st_ref[...] = st * dec_ref[c, 0:1, :] + ds_ref[c]
            if fwd:
                acc_ref[sl, :] = tot
            else:
                y = tot * lax.rsqrt(jnp.mean(tot * tot, axis=-1, keepdims=True) + EPS) * gn_ref[...]
                g = g_ref[sl, :].astype(F32)
                o_ref[sl, :] = (y * (g * (1.0 / (1.0 + jnp.exp(-g))))).astype(o_ref.dtype)
            return carry

        lax.fori_loop(0, nc, chunk, 0, unroll=2)
        if emit_state:
            s_out_ref[...] = st_ref[...].T

    sweep(qdf_ref, dsf_ref, decf_ref, s0f_ref, sf_ref, True)
    sweep(qdb_ref, dsb_ref, decb_ref, s0b_ref, sb_ref, False)


def _gla(cq, ck, cv, cg, la, gn, n_seq, layer, *, s0=None, emit_state):
    t = cq.shape[0]
    nb = t // n_seq
    nc = n_seq // C_CHUNK
    tok = lambda b, h: (b, h)
    in_specs = [pl.BlockSpec((n_seq, C_DK), tok), pl.BlockSpec((n_seq, C_DK), tok),
                pl.BlockSpec((n_seq, C_DV), tok), pl.BlockSpec((n_seq, C_DV), tok),
                pl.BlockSpec((n_seq, C_DK), tok), pl.BlockSpec((n_seq, C_DK), lambda b, h: (b, C_HEADS + h)),
                pl.BlockSpec((None, 1, C_DV), lambda b, h: (layer, 0, 0))]
    args = [cq, ck, cv, cg, la, la, gn]
    if s0 is not None:
        spec = pl.BlockSpec((None, None, None, C_DK, C_DV), lambda b, h: (b, layer, h, 0, 0))
        in_specs += [spec, spec]
        args += list(s0)
    out_specs = [pl.BlockSpec((n_seq, C_DV), tok)]
    out_shape = [_sds((t, C_HEADS * C_DV), BF16)]
    if emit_state:
        spec = pl.BlockSpec((None, None, C_DK, C_DV), lambda b, h: (b, h, 0, 0))
        out_specs += [spec, spec]
        out_shape += [_sds((nb, C_HEADS, C_DK, C_DV)), _sds((nb, C_HEADS, C_DK, C_DV))]
    return pl.pallas_call(
        functools.partial(_gla_body, n_seq=n_seq, init=s0 is not None, emit_state=emit_state),
        grid=(nb, C_HEADS), in_specs=in_specs, out_specs=out_specs, out_shape=out_shape,
        scratch_shapes=[pltpu.VMEM((n_seq, C_DV), F32), pltpu.VMEM((C_DV, C_DK), F32),
                        pltpu.VMEM((n_seq, C_DK), BF16), pltpu.VMEM((n_seq, C_DK), BF16),
                        pltpu.VMEM((nc, C_DV, C_DK), F32), pltpu.VMEM((nc, C_DV, C_DK), F32),
                        pltpu.VMEM((nc, 8, C_DK), F32), pltpu.VMEM((nc, 8, C_DK), F32)],
        compiler_params=_params("arbitrary", "arbitrary"), name="gla",
    )(*args)


def _outproj_body(oa_ref, ob_ref, oc_ref, x_ref, g1_ref, sh_ref, sc_ref, n2_ref, w_ref, wr_ref,
                  x1_ref, h2_ref, r_ref):
    mix = jnp.concatenate([oa_ref[p] for p in range(4)] + [ob_ref[p] for p in range(4)] + [oc_ref[...]], axis=1)
    y = jnp.dot(mix, w_ref[...], preferred_element_type=F32)
    x1 = x_ref[...] + g1_ref[...] * y
    x1_ref[...] = x1
    h = x1 * lax.rsqrt(jnp.mean(x1 * x1, axis=-1, keepdims=True) + EPS) * n2_ref[...]
    h = h * (1.0 + sc_ref[...]) + sh_ref[...]
    h2_ref[...] = h
    lg = jnp.dot(h.astype(BF16), wr_ref[...], preferred_element_type=F32)
    lane = lax.broadcasted_iota(I32, lg.shape, 1)
    big = jnp.int32(LANES)
    is_g = lane < N_GROUPS
    gmax = jnp.max(jnp.where(is_g, lg, -jnp.inf), axis=-1, keepdims=True)
    gidx = jnp.min(jnp.where(is_g & (lg == gmax), lane, big), axis=-1, keepdims=True)
    gw = 1.0 / jnp.sum(jnp.where(is_g, jnp.exp(lg - gmax), 0.0), axis=-1, keepdims=True)
    lo = N_GROUPS + E_PER_GROUP * gidx
    sel = (lane >= lo) & (lane < lo + E_PER_GROUP)
    v1 = jnp.max(jnp.where(sel, lg, -jnp.inf), axis=-1, keepdims=True)
    i1 = jnp.min(jnp.where(sel & (lg == v1), lane, big), axis=-1, keepdims=True)
    sel2 = sel & (lane != i1)
    v2 = jnp.max(jnp.where(sel2, lg, -jnp.inf), axis=-1, keepdims=True)
    i2 = jnp.min(jnp.where(sel2 & (lg == v2), lane, big), axis=-1, keepdims=True)
    e21 = jnp.exp(v2 - v1)
    w1 = gw / (1.0 + e21)
    w2 = gw * e21 / (1.0 + e21)
    out = jnp.where(lane == 0, (i1 - N_GROUPS).astype(F32),
                    jnp.where(lane == 1, (i2 - N_GROUPS).astype(F32),
                              jnp.where(lane == 2, w1, jnp.where(lane == 3, w2, 0.0))))
    r_ref[...] = out


def _outproj(oa, ob, oc, x, mods, mod_row_of_tile, n2, w_out_b, wr_b, layer):
    t, d = x.shape
    tm = TM_PROJ
    tok = lambda i: (i, 0)
    lay = lambda i: (layer, 0, 0)
    mod = lambda k: pl.BlockSpec((None, 1, d), lambda i: (mod_row_of_tile(i), 0, k))
    slab4 = pl.BlockSpec((4, tm, LANES), lambda i: (0, i, 0))
    return pl.pallas_call(
        _outproj_body,
        grid=(t // tm,),
        in_specs=[slab4, slab4, pl.BlockSpec((tm, 1024), tok), pl.BlockSpec((tm, d), tok),
                  mod(2), mod(3), mod(4), pl.BlockSpec((None, 1, d), lay),
                  _resident((None, d, d), lay), _resident((None, d, LANES), lay)],
        out_specs=[pl.BlockSpec((tm, d), tok), pl.BlockSpec((tm, d), tok), pl.BlockSpec((tm, LANES), tok)],
        out_shape=[_sds((t, d)), _sds((t, d)), _sds((t, LANES))],
        compiler_params=_params("arbitrary"), name="outproj",
    )(oa, ob, oc, x, mods, mods, mods, n2, w_out_b, wr_b)


def _moe_body(te_ref, nt_ref, src_ref, srcn_ref, dst_ref, h_ref, w1_ref, w3_ref, w2_ref, y_ref,
              xbuf, ybuf, gsem, ssem, *, n_tiles):
    tm = xbuf.shape[1]
    r = pl.program_id(0)
    nt = nt_ref[0]
    slot = r % 2

    def gather_rows(idx_ref, s):
        def issue(j, c):
            pltpu.make_async_copy(h_ref.at[pl.ds(idx_ref[0, 0, j], 1)], xbuf.at[s, pl.ds(j, 1)], gsem.at[s]).start()
            return c
        lax.fori_loop(0, tm, issue, 0, unroll=8)

    def all_rows_in(s):
        return pltpu.make_async_copy(h_ref.at[pl.ds(0, tm)], xbuf.at[s], gsem.at[s])

    def all_rows_out(s):
        return pltpu.make_async_copy(ybuf.at[s], y_ref.at[pl.ds(0, tm)], ssem.at[s])

    @pl.when(r == 0)
    def _():
        ybuf[...] = jnp.zeros(ybuf.shape, F32)
        n_pairs = y_ref.shape[0] - 2 * tm
        for s in range(2):
            spare = pltpu.make_async_copy(ybuf.at[s], y_ref.at[pl.ds(n_pairs + s * tm, tm)], ssem.at[s])
            spare.start()
            spare.wait()
        gather_rows(src_ref, 0)

    @pl.when(r + 1 < nt)
    def _():
        gather_rows(srcn_ref, 1 - slot)

    @pl.when(r < nt)
    def _():
        all_rows_in(slot).wait()

        @pl.when(r >= 2)
        def _():
            all_rows_out(slot).wait()

        x = xbuf[slot].astype(BF16)
        h1 = jnp.dot(x, w1_ref[...], preferred_element_type=F32)
        h3 = jnp.dot(x, w3_ref[...], preferred_element_type=F32)
        hid = (h1 * (1.0 / (1.0 + jnp.exp(-h1))) * h3).astype(BF16)
        ybuf[slot] = jnp.dot(hid, w2_ref[...], preferred_element_type=F32)

        def issue(j, c):
            pltpu.make_async_copy(ybuf.at[slot, pl.ds(j, 1)], y_ref.at[pl.ds(dst_ref[0, 0, j], 1)], ssem.at[slot]).start()
            return c
        lax.fori_loop(0, tm, issue, 0, unroll=8)

    @pl.when(r == n_tiles - 1)
    def _():
        @pl.when(nt >= 1)
        def _():
            all_rows_out((nt - 1) % 2).wait()

        @pl.when(nt >= 2)
        def _():
            all_rows_out(nt % 2).wait()


def _moe(h2, route, w1b, w3b, w2b, layer):
    t, d = h2.shape
    tm = TM_MOE
    npair = 2 * t
    n_rows = npair + N_EXPERTS * tm
    n_tiles = n_rows // tm
    ef = route[:, :2].astype(I32).T.reshape(-1)
    order = jnp.argsort(ef, stable=True).astype(I32)
    counts = jnp.sum((ef[:, None] == jnp.arange(N_EXPERTS, dtype=I32)[None, :]).astype(I32), axis=0)
    gstart = jnp.cumsum(counts) - counts
    pcount = ((counts + tm - 1) // tm) * tm
    pend = jnp.cumsum(pcount)
    pstart = pend - pcount
    row = jnp.arange(n_rows, dtype=I32)
    er = jnp.minimum(jnp.sum((row[:, None] >= pend[None, :]).astype(I32), axis=1), N_EXPERTS - 1)
    j = row - pstart[er]
    valid = (j < counts[er]) & (row < pend[-1])
    pair = order[jnp.clip(gstart[er] + j, 0, npair - 1)]
    src = jnp.where(valid, pair % t, 0).reshape(n_tiles, 1, tm)
    dump = npair + ((row // tm) % 2) * tm + row % tm
    dst = jnp.where(valid, pair, dump).reshape(n_tiles, 1, tm)
    nt = (pend[-1] // tm).reshape(1)
    tile_e = er[::tm]
    tile_e = jnp.where(jnp.arange(n_tiles) < nt[0], tile_e, tile_e[jnp.maximum(nt[0] - 1, 0)])

    smem = lambda f: pl.BlockSpec((1, 1, tm), f, memory_space=pltpu.SMEM)
    grid_spec = pltpu.PrefetchScalarGridSpec(
        num_scalar_prefetch=2, grid=(n_tiles,),
        in_specs=[smem(lambda r, te, n: (r, 0, 0)),
                  smem(lambda r, te, n: (jnp.minimum(r + 1, n_tiles - 1), 0, 0)),
                  smem(lambda r, te, n: (r, 0, 0)),
                  pl.BlockSpec(memory_space=pl.ANY),
                  pl.BlockSpec((None, None, d, D_EXPERT), lambda r, te, n: (layer, te[r], 0, 0)),
                  pl.BlockSpec((None, None, d, D_EXPERT), lambda r, te, n: (layer, te[r], 0, 0)),
                  pl.BlockSpec((None, None, D_EXPERT, d), lambda r, te, n: (layer, te[r], 0, 0))],
        out_specs=pl.BlockSpec(memory_space=pl.ANY),
        scratch_shapes=[pltpu.VMEM((2, tm, d), F32), pltpu.VMEM((2, tm, d), F32),
                        pltpu.SemaphoreType.DMA((2,)), pltpu.SemaphoreType.DMA((2,))])
    return pl.pallas_call(
        functools.partial(_moe_body, n_tiles=n_tiles),
        grid_spec=grid_spec, out_shape=_sds((npair + 2 * tm, d)),
        compiler_params=_params("arbitrary"), name="moe",
    )(tile_e, nt, src, src, dst, h2, w1b, w3b, w2b)


def _combine_body(x1_ref, y0_ref, y1_ref, r_ref, g2_ref, o_ref):
    r = r_ref[...]
    lane = lax.broadcasted_iota(I32, r.shape, 1)
    w1 = jnp.sum(jnp.where(lane == 2, r, 0.0), axis=-1, keepdims=True)
    w2 = jnp.sum(jnp.where(lane == 3, r, 0.0), axis=-1, keepdims=True)
    o_ref[...] = x1_ref[...] + g2_ref[...] * (w1 * y0_ref[...] + w2 * y1_ref[...])


def _combine(x1, yg, route, mods, mod_row_of_tile):
    t, d = x1.shape
    tm = TM_PROJ
    nt = t // tm
    tok = lambda i: (i, 0)
    return pl.pallas_call(
        _combine_body,
        grid=(nt,),
        in_specs=[pl.BlockSpec((tm, d), tok), pl.BlockSpec((tm, d), tok),
                  pl.BlockSpec((tm, d), lambda i: (nt + i, 0)), pl.BlockSpec((tm, LANES), tok),
                  pl.BlockSpec((None, 1, d), lambda i: (mod_row_of_tile(i), 0, 5))],
        out_specs=pl.BlockSpec((tm, d), tok), out_shape=_sds((t, d)),
        compiler_params=_params("arbitrary"), name="combine",
    )(x1, yg, yg, route, mods)


def _rope_tables():
    t = jnp.arange(GRID_W * GRID_ROWS)
    pairs = HEAD_DIM // 4
    inv = 1.0 / (ROPE_THETA ** (jnp.arange(pairs, dtype=F32) * 2.0 / (HEAD_DIM // 2)))
    ang_r = (t // GRID_W).astype(F32)[:, None] * inv
    ang_c = (t % GRID_W).astype(F32)[:, None] * inv
    cos = jnp.concatenate([jnp.cos(ang_r)] * 2 + [jnp.cos(ang_c)] * 2, axis=1)
    sin = jnp.concatenate([-jnp.sin(ang_r), jnp.sin(ang_r), -jnp.sin(ang_c), jnp.sin(ang_c)], axis=1)
    return jnp.tile(cos, (1, 2)), jnp.tile(sin, (1, 2))


def _bias_pair_tiles(rpb_l):
    qc = jnp.arange(GRID_W)[:, None]
    kc = jnp.arange(GRID_W)[None, :]
    start_c = jnp.clip(qc - NA_COLS // 2, 0, GRID_W - NA_COLS)
    col_ok = (kc >= start_c) & (kc < start_c + NA_COLS)
    dc = jnp.clip(kc - qc + (NA_COLS - 1), 0, 2 * NA_COLS - 2)
    d = jnp.arange(-8, 23)
    tile = rpb_l[:, jnp.clip(d, 0, 2 * NA_ROWS - 2)][:, :, dc]
    ok = col_ok[None, None] & ((d >= 0) & (d <= 2 * NA_ROWS - 2))[None, :, None, None]
    tile = jnp.where(ok, tile, NEG_INF).astype(F32)
    return jnp.concatenate([tile[:, :-1], tile[:, 1:]], axis=-1)


def kernel(x_prompt, x_sample, c, cache_a_k, cache_a_v, cache_b_k, cache_b_v, state_c_fwd, state_c_bwd, c_ctx, norm1_g, norm2_g, w_mod, b_mod, w_in, q_norm_a, k_norm_a, sink_a, q_norm_b, k_norm_b, rpb_b, w_gk_up_f, b_gk_f, w_gk_up_b, b_gk_b, gla_norm_g, w_out, w_group, w_router, w1, w3, w2):
    depth = w_in.shape[0]
    nb_c, l_c, d = x_prompt.shape
    nb_s, n_s, _ = x_sample.shape
    past = cache_a_k.shape[2]
    tm = TM_PROJ

    w_in_b = jnp.pad(w_in, ((0, 0), (0, 0), (0, N_IN_PAD - N_IN))).astype(BF16)
    w_out_b = w_out.astype(BF16)
    wr_b = jnp.pad(jnp.concatenate([w_group, w_router], axis=-1),
                   ((0, 0), (0, 0), (0, LANES - N_GROUPS - N_EXPERTS))).astype(BF16)
    w1b, w3b, w2b = w1.astype(BF16), w3.astype(BF16), w2.astype(BF16)
    wgk = jnp.zeros((depth, 256, 1024), F32)
    wgk = wgk.at[:, :C_RANK, :512].set(w_gk_up_f).at[:, C_RANK:2 * C_RANK, 512:].set(w_gk_up_b).astype(BF16)
    cos, sin = _rope_tables()
    head_id = jnp.arange(512) // HEAD_DIM
    consts = dict(
        bd=(head_id[:, None] == head_id[None, :]).astype(BF16),
        gqa=jnp.tile(q_norm_a, (1, 8)).reshape(depth, 1, 512),
        gka=jnp.tile(k_norm_a, (1, 2)).reshape(depth, 1, LANES),
        gqb=jnp.tile(q_norm_b, (1, 8)).reshape(depth, 1, 512),
        gkb=jnp.tile(k_norm_b, (1, 8)).reshape(depth, 1, 512),
        cos=cos, sin=sin, wgk=wgk,
        bgk=jnp.concatenate([b_gk_f, b_gk_b], axis=-1).reshape(depth, 1, 1024))
    n1 = norm1_g.reshape(depth, 1, d)
    n2 = norm2_g.reshape(depth, 1, d)
    gn = gla_norm_g.reshape(depth, 1, C_DV)
    cond = jnp.zeros((16, d), F32).at[0].set(c_ctx).at[1:1 + nb_s].set(c)
    ctx_row = lambda i: 0
    lat_row = lambda i: 1 + i // (n_s // tm)
    cak = cache_a_k.reshape(nb_s, depth, 1, past, LANES).astype(BF16)
    cav = cache_a_v.reshape(nb_s, depth, 1, past, LANES).astype(BF16)
    cbk = cache_b_k.reshape(nb_s, depth, past, 4, LANES).transpose(0, 1, 3, 2, 4).astype(BF16)
    cbv = cache_b_v.reshape(nb_s, depth, past, 4, LANES).transpose(0, 1, 3, 2, 4).astype(BF16)

    xp = x_prompt.reshape(nb_c * l_c, d)
    xs = x_sample.reshape(nb_s * n_s, d)
    new = [[] for _ in range(6)]
    for l in range(depth):
        mods = _adaln(cond, w_mod, b_mod, l).reshape(16, 1, 6 * d)
        tp = _bias_pair_tiles(rpb_b[l])

        aq, ak, av, bq, bk, bv, cq, ck, cv, cg, la = _inproj(
            xp, mods, ctx_row, n1, w_in_b, consts, l, rope=False, kv_dtype=F32)
        oa = _attention("a_ctx", aq, ak.reshape(1, -1, LANES), av.reshape(1, -1, LANES), l_c, sink=sink_a, layer=l)
        ob = _attention("b_ctx", bq, bk, bv, l_c)
        oc, sf, sb = _gla(cq, ck, cv, cg, la, gn, l_c, l, emit_state=True)
        x1, h2, route = _outproj(oa, ob, oc, xp, mods, ctx_row, n2, w_out_b, wr_b, l)
        xp = _combine(x1, _moe(h2, route, w1b, w3b, w2b, l), route, mods, ctx_row)
        new[0].append(ak.reshape(nb_c, l_c, 2, HEAD_DIM))
        new[1].append(av.reshape(nb_c, l_c, 2, HEAD_DIM))
        new[2].append(bk.transpose(1, 0, 2).reshape(nb_c, l_c, 8, HEAD_DIM))
        new[3].append(bv.transpose(1, 0, 2).reshape(nb_c, l_c, 8, HEAD_DIM))
        new[4].append(sf)
        new[5].append(sb)

        aq, ak, av, bq, bk, bv, cq, ck, cv, cg, la = _inproj(
            xs, mods, lat_row, n1, w_in_b, consts, l, rope=True, kv_dtype=BF16)
        oa = _attention("a_lat", aq, ak.reshape(1, -1, LANES), av.reshape(1, -1, LANES), n_s,
                        prefix=(cak[:, l], cav[:, l]), sink=sink_a, layer=l)
        ob = _attention("b_lat", bq, bk, bv, n_s, prefix=(cbk[:, l], cbv[:, l]), tp=tp)
        oc, = _gla(cq, ck, cv, cg, la, gn, n_s, l, s0=(state_c_fwd, state_c_bwd), emit_state=False)
        x1, h2, route = _outproj(oa, ob, oc, xs, mods, lat_row, n2, w_out_b, wr_b, l)
        xs = _combine(x1, _moe(h2, route, w1b, w3b, w2b, l), route, mods, lat_row)

    outs = [jnp.stack(v, axis=1) for v in new]
    return (xp.reshape(nb_c, l_c, d), xs.reshape(nb_s, n_s, d), *outs)
```

```python
import functools

import jax
import jax.numpy as jnp
from jax import lax
from jax.experimental import pallas as pl
from jax.experimental.pallas import tpu as pltpu

F32 = jnp.float32
BF16 = jnp.bfloat16
I32 = jnp.int32

D_MODEL = 2048
HEAD_DIM = 64
EPS = 1e-6
NEG_INF = -1e30
ROPE_THETA = 10000.0
GRID_W = 64
GRID_ROWS = 32
A_WINDOW = 128
NA_ROWS = 8
NA_COLS = 16
C_HEADS = 4
C_DK = 128
C_DV = 256
C_RANK = 16
C_CHUNK = 64
GLA_BLOCK = 256
N_GROUPS = 4
E_PER_GROUP = 4
N_EXPERTS = 16
D_EXPERT = 512
LANES = 128
N_IN = 5408
N_IN_PAD = 5632
COL_AQ, COL_AKV, COL_BQ, COL_BK, COL_BV = 0, 512, 768, 1280, 1792
COL_CQ, COL_CK, COL_CV, COL_CG, COL_CR = 2304, 2816, 3328, 4352, 5376
VMEM_LIMIT = 56 * 1024 * 1024
TM_PROJ = 256
TM_MOE = 256

_TRANS_B = (((1,), (1,)), ((), ()))
_TRANS_A = (((0,), (0,)), ((), ()))


def _sds(shape, dtype=F32):
    return jax.ShapeDtypeStruct(shape, dtype)


def _params(*sem):
    return pltpu.CompilerParams(dimension_semantics=sem, vmem_limit_bytes=VMEM_LIMIT)


def _resident(shape, index_map):
    return pl.BlockSpec(shape, index_map, pipeline_mode=pl.Buffered(1))


def _adaln_body(c_ref, w_ref, b_ref, o_ref):
    c = c_ref[...]
    s = c * (1.0 / (1.0 + jnp.exp(-c)))
    o_ref[...] = jnp.dot(s.astype(BF16), w_ref[...].astype(BF16), preferred_element_type=F32) + b_ref[...]


def _adaln(cond, w_mod, b_mod, layer):
    d, n = w_mod.shape[1], w_mod.shape[2]
    tn = 1024
    return pl.pallas_call(
        _adaln_body,
        grid=(n // tn,),
        in_specs=[pl.BlockSpec((16, d), lambda j: (0, 0)),
                  pl.BlockSpec((None, d, tn), lambda j: (layer, 0, j)),
                  pl.BlockSpec((None, 1, tn), lambda j: (layer, 0, j))],
        out_specs=pl.BlockSpec((16, tn), lambda j: (0, j)),
        out_shape=_sds((16, n)),
        compiler_params=_params("arbitrary"),
        name="adaln",
    )(cond, w_mod, b_mod.reshape(b_mod.shape[0], 1, n))


def _head_norm(z, gain, bd):
    w = z.shape[1]
    ss = jnp.dot((z * z).astype(BF16), bd[:w, :w], preferred_element_type=F32)
    return z * lax.rsqrt(ss * (1.0 / HEAD_DIM) + EPS) * gain


def _rope(z, cos, sin_signed, lane):
    lower = (lane % 32) < 16
    partner = jnp.where(lower, pltpu.roll(z, LANES - 16, 1), pltpu.roll(z, 16, 1))
    return z * cos + partner * sin_signed


def _inproj_body(x_ref, sh_ref, sc_ref, n1_ref, w_ref, bd_ref, gqa_ref, gka_ref, gqb_ref, gkb_ref,
                 cos_ref, sin_ref, wgk_ref, bgk_ref,
                 aq_ref, ak_ref, av_ref, bq_ref, bk_ref, bv_ref, cq_ref, ck_ref, cv_ref, cg_ref, la_ref,
                 *, rope):
    x = x_ref[...]
    h = x * lax.rsqrt(jnp.mean(x * x, axis=-1, keepdims=True) + EPS) * n1_ref[...]
    h = h * (1.0 + sc_ref[...]) + sh_ref[...]
    hb = h.astype(BF16)
    bd = bd_ref[...]
    lane = lax.broadcasted_iota(I32, (x.shape[0], LANES), 1)
    upper = lane >= HEAD_DIM

    def proj(c0, n):
        return jnp.dot(hb, w_ref[:, c0:c0 + n], preferred_element_type=F32)

    def rot(s):
        return _rope(s, cos_ref[...], sin_ref[...], lane) if rope else s

    z = _head_norm(proj(COL_AQ, 512), gqa_ref[...], bd)
    for p in range(4):
        s = rot(z[:, LANES * p:LANES * (p + 1)]) * (HEAD_DIM ** -0.5)
        r = pltpu.roll(s, HEAD_DIM, 1)
        if p // 2 == 0:
            e0, e1 = jnp.where(upper, 0.0, s), jnp.where(upper, 0.0, r)
        else:
            e0, e1 = jnp.where(upper, r, 0.0), jnp.where(upper, s, 0.0)
        aq_ref[2 * p] = e0.astype(BF16)
        aq_ref[2 * p + 1] = e1.astype(BF16)
    z = proj(COL_AKV, 256)
    ak_ref[...] = rot(_head_norm(z[:, :LANES], gka_ref[...], bd)).astype(ak_ref.dtype)
    av_ref[...] = z[:, LANES:].astype(av_ref.dtype)
    z = _head_norm(proj(COL_BQ, 512), gqb_ref[...], bd) * (HEAD_DIM ** -0.5)
    for p in range(4):
        s = z[:, LANES * p:LANES * (p + 1)]
        bq_ref[2 * p] = jnp.where(upper, 0.0, s).astype(BF16)
        bq_ref[2 * p + 1] = jnp.where(upper, s, 0.0).astype(BF16)
    z = _head_norm(proj(COL_BK, 512), gkb_ref[...], bd)
    for p in range(4):
        bk_ref[p] = z[:, LANES * p:LANES * (p + 1)].astype(bk_ref.dtype)
    z = proj(COL_BV, 512)
    for p in range(4):
        bv_ref[p] = z[:, LANES * p:LANES * (p + 1)].astype(bv_ref.dtype)
    cq_ref[...] = (proj(COL_CQ, 512) * (C_DK ** -0.5)).astype(BF16)
    ck_ref[...] = proj(COL_CK, 512).astype(BF16)
    cv_ref[...] = proj(COL_CV, 1024).astype(BF16)
    cg_ref[...] = proj(COL_CG, 1024).astype(BF16)
    pre = jnp.dot(proj(COL_CR, 256).astype(BF16), wgk_ref[...], preferred_element_type=F32) + bgk_ref[...]
    la_ref[...] = (jnp.minimum(pre, 0.0) - jnp.log(1.0 + jnp.exp(-jnp.abs(pre)))) * (1.0 / 16.0)


def _inproj(x, mods, mod_row_of_tile, n1, w_in_b, consts, layer, *, rope, kv_dtype):
    t, d = x.shape
    tm = TM_PROJ
    seq_tiles = (GRID_W * GRID_ROWS) // tm
    tok = lambda i: (i, 0)
    full = lambda i: (0, 0)
    lay = lambda i: (layer, 0, 0)
    in_specs = [
        pl.BlockSpec((tm, d), tok),
        pl.BlockSpec((None, 1, d), lambda i: (mod_row_of_tile(i), 0, 0)),
        pl.BlockSpec((None, 1, d), lambda i: (mod_row_of_tile(i), 0, 1)),
        pl.BlockSpec((None, 1, d), lay),
        _resident((None, d, N_IN_PAD), lay),
        _resident((512, 512), full),
        pl.BlockSpec((None, 1, 512), lay), pl.BlockSpec((None, 1, LANES), lay),
        pl.BlockSpec((None, 1, 512), lay), pl.BlockSpec((None, 1, 512), lay),
        pl.BlockSpec((tm, LANES), lambda i: (i % seq_tiles, 0)),
        pl.BlockSpec((tm, LANES), lambda i: (i % seq_tiles, 0)),
        _resident((None, 256, 1024), lay),
        pl.BlockSpec((None, 1, 1024), lay),
    ]
    slab8 = pl.BlockSpec((8, tm, LANES), lambda i: (0, i, 0))
    slab4 = pl.BlockSpec((4, tm, LANES), lambda i: (0, i, 0))
    out_specs = [slab8, pl.BlockSpec((tm, LANES), tok), pl.BlockSpec((tm, LANES), tok),
                 slab8, slab4, slab4,
                 pl.BlockSpec((tm, 512), tok), pl.BlockSpec((tm, 512), tok),
                 pl.BlockSpec((tm, 1024), tok), pl.BlockSpec((tm, 1024), tok), pl.BlockSpec((tm, 1024), tok)]
    out_shape = [_sds((8, t, LANES), BF16), _sds((t, LANES), kv_dtype), _sds((t, LANES), kv_dtype),
                 _sds((8, t, LANES), BF16), _sds((4, t, LANES), kv_dtype), _sds((4, t, LANES), kv_dtype),
                 _sds((t, 512), BF16), _sds((t, 512), BF16), _sds((t, 1024), BF16), _sds((t, 1024), BF16),
                 _sds((t, 1024), F32)]
    return pl.pallas_call(
        functools.partial(_inproj_body, rope=rope),
        grid=(t // tm,), in_specs=in_specs, out_specs=out_specs, out_shape=out_shape,
        compiler_params=_params("arbitrary"), name="inproj",
    )(x, mods, mods, n1, w_in_b, consts["bd"], consts["gqa"], consts["gka"], consts["gqb"], consts["gkb"],
      consts["cos"], consts["sin"], consts["wgk"], consts["bgk"])


def _attn_body(*refs, kind, tq, kwin, n_seq, layer):
    is_a = kind in ("a_ctx", "a_lat")
    prefix = kind in ("a_lat", "b_lat")
    it = iter(refs)
    q_ref, k_ref, v_ref = next(it), next(it), next(it)
    kc_ref = vc_ref = sink_ref = tp_ref = None
    if prefix:
        kc_ref, vc_ref = next(it), next(it)
    if is_a:
        sink_ref = next(it)
    if kind == "b_lat":
        tp_ref = next(it)
    o_ref = next(it)
    i = pl.program_id(1)

    group = 4 if is_a else 1
    if kind == "a_lat":
        ws = jnp.clip(i * tq - A_WINDOW, 0, n_seq - kwin)
        ws = pl.multiple_of(ws, LANES)
        qpos = i * tq + lax.broadcasted_iota(I32, (group * tq, kwin), 0) % tq
        kpos = ws + lax.broadcasted_iota(I32, (group * tq, kwin), 1)
        allowed = jnp.abs(qpos - kpos) <= A_WINDOW
    elif kind == "b_lat":
        r0 = i * (tq // GRID_W)
        k0 = jnp.clip(r0 - NA_ROWS // 2, 0, GRID_ROWS - kwin // GRID_W)
        ws = pl.multiple_of(k0 * GRID_W, GRID_W)
        qrow = r0 + lax.broadcasted_iota(I32, (tq, kwin), 0) // GRID_W
        krow = k0 + lax.broadcasted_iota(I32, (tq, kwin), 1) // GRID_W
        start = jnp.clip(qrow - NA_ROWS // 2, 0, GRID_ROWS - NA_ROWS)
        allowed = (krow >= start) & (krow < start + NA_ROWS)
        row_mask = jnp.where(allowed, 0.0, NEG_INF)
    else:
        ws = 0
    lane = lax.broadcasted_iota(I32, (tq, LANES), 1)
    upper = lane >= HEAD_DIM

    def attend(q, ks, h, sink):
        kl = k_ref[ks, pl.ds(ws, kwin), :].astype(BF16)
        vl = v_ref[ks, pl.ds(ws, kwin), :].astype(BF16)
        s = lax.dot_general(q, kl, _TRANS_B, preferred_element_type=F32)
        if kind == "a_lat":
            s = jnp.where(allowed, s, NEG_INF)
        elif kind == "b_lat":
            rows = []
            for qr in range(tq // GRID_W):
                u0 = k0 - r0 - qr + 15
                rows.append(jnp.concatenate([tp_ref[h, u0 + 2 * m] for m in range(kwin // LANES)], axis=1))
            s = s + jnp.concatenate(rows, axis=0) + row_mask
        m = jnp.max(s, axis=-1, keepdims=True)
        if prefix:
            sc = lax.dot_general(q, kc_ref[ks], _TRANS_B, preferred_element_type=F32)
            m = jnp.maximum(m, jnp.max(sc, axis=-1, keepdims=True))
        if is_a:
            m = jnp.maximum(m, sink)
        e = jnp.exp(s - m)
        den = jnp.sum(e, axis=-1, keepdims=True)
        o = jnp.dot(e.astype(BF16), vl, preferred_element_type=F32)
        if prefix:
            ec = jnp.exp(sc - m)
            den = den + jnp.sum(ec, axis=-1, keepdims=True)
            o = o + jnp.dot(ec.astype(BF16), vc_ref[ks], preferred_element_type=F32)
        if is_a:
            den = den + jnp.exp(sink - m)
        return o * (1.0 / den)

    if is_a:
        head_of_row = lax.broadcasted_iota(I32, (group * tq, 1), 0) // tq
        for g in range(2):
            q = jnp.concatenate([q_ref[group * g + j] for j in range(group)], axis=0)
            sink = jnp.zeros((group * tq, 1), F32)
            for j in range(group):
                sink = jnp.where(head_of_row == j, sink_ref[layer, group * g + j], sink)
            o = attend(q, 0, None, sink)
            for pp in range(2):
                o0 = o[(2 * pp) * tq:(2 * pp + 1) * tq]
                o1 = o[(2 * pp + 1) * tq:(2 * pp + 2) * tq]
                if g == 0:
                    o1 = pltpu.roll(o1, HEAD_DIM, 1)
                else:
                    o0 = pltpu.roll(o0, HEAD_DIM, 1)
                o_ref[2 * g + pp] = jnp.where(upper, o1, o0).astype(o_ref.dtype)
    else:
        def pair(p):
            o0 = attend(q_ref[2 * p], p, 2 * p, None)
            o1 = attend(q_ref[2 * p + 1], p, 2 * p + 1, None)
            o_ref[p] = jnp.where(upper, o1, o0).astype(o_ref.dtype)

        if kind == "b_lat":
            def body(p, c):
                pair(p)
                return c
            lax.fori_loop(0, 4, body, 0)
        else:
            for p in range(4):
                pair(p)


def _attention(kind, q, k, v, n_seq, *, prefix=None, sink=None, tp=None, layer=0):
    t = q.shape[1]
    nb = t // n_seq
    s_k = k.shape[0]
    tq, kwin = {"a_ctx": (n_seq, n_seq), "b_ctx": (n_seq, n_seq),
                "a_lat": (A_WINDOW, 3 * A_WINDOW), "b_lat": (512, 1024)}[kind]
    nq = n_seq // tq
    in_specs = [pl.BlockSpec((8, tq, LANES), lambda b, i: (0, b * nq + i, 0)),
                pl.BlockSpec((s_k, n_seq, LANES), lambda b, i: (0, b, 0)),
                pl.BlockSpec((s_k, n_seq, LANES), lambda b, i: (0, b, 0))]
    args = [q, k, v]
    if prefix is not None:
        kc, vc = prefix
        spec = pl.BlockSpec((None,) + kc.shape[1:], lambda b, i: (b, 0, 0, 0))
        in_specs += [spec, spec]
        args += [kc, vc]
    if sink is not None:
        in_specs.append(pl.BlockSpec(memory_space=pltpu.SMEM))
        args.append(sink)
    if tp is not None:
        in_specs.append(_resident(tp.shape, lambda b, i: (0, 0, 0, 0)))
        args.append(tp)
    return pl.pallas_call(
        functools.partial(_attn_body, kind=kind, tq=tq, kwin=kwin, n_seq=n_seq, layer=layer),
        grid=(nb, nq), in_specs=in_specs,
        out_specs=pl.BlockSpec((4, tq, LANES), lambda b, i: (0, b * nq + i, 0)),
        out_shape=_sds((4, t, LANES), BF16),
        compiler_params=_params("arbitrary", "arbitrary"), name="attn_" + kind,
    )(*args)


def _gla_body(*refs, n_seq, init, emit_state):
    it = iter(refs)
    q_ref, k_ref, v_ref, g_ref, laf_ref, lab_ref, gn_ref = (next(it) for _ in range(7))
    s0f_ref = s0b_ref = sf_ref = sb_ref = None
    if init:
        s0f_ref, s0b_ref = next(it), next(it)
    o_ref = next(it)
    if emit_state:
        sf_ref, sb_ref = next(it), next(it)
    acc_ref, st_ref, qdf_ref, qdb_ref, dsf_ref, dsb_ref, decf_ref, decb_ref = (next(it) for _ in range(8))
    nc = n_seq // C_CHUNK
    per = GLA_BLOCK // C_CHUNK
    row = lax.broadcasted_iota(I32, (GLA_BLOCK, C_DK), 0) % C_CHUNK
    ri = lax.broadcasted_iota(I32, (GLA_BLOCK, GLA_BLOCK), 0)
    ci = lax.broadcasted_iota(I32, (GLA_BLOCK, GLA_BLOCK), 1)
    same = (ri // C_CHUNK) == (ci // C_CHUNK)
    zero_chunk = jnp.zeros((C_CHUNK, C_DK), BF16)

    def decay_sums(la, fwd):
        x = la
        for s in (1, 2, 4, 8, 16, 32):
            if fwd:
                x = x + jnp.where(row >= s, pltpu.roll(x, s, 0), 0.0)
            else:
                x = x + jnp.where(row < C_CHUNK - s, pltpu.roll(x, GLA_BLOCK - s, 0), 0.0)
        return x

    def block(u, carry):
        sl = pl.ds(pl.multiple_of(u * GLA_BLOCK, GLA_BLOCK), GLA_BLOCK)
        q = q_ref[sl, :].astype(F32)
        k = k_ref[sl, :].astype(F32)
        v = v_ref[sl, :]
        att = None
        rhs = []
        for fwd, la_ref, qd_ref, dec_ref in ((True, laf_ref, qdf_ref, decf_ref), (False, lab_ref, qdb_ref, decb_ref)):
            b = decay_sums(la_ref[sl, :], fwd)
            b3 = b.reshape(per, C_CHUNK, C_DK)
            bl3 = b3[:, C_CHUNK - 1:C_CHUNK, :] if fwd else b3[:, 0:1, :]
            qd = (q * jnp.exp(b)).astype(BF16)
            ki = (k * jnp.exp(-b)).astype(BF16)
            ke3 = (k.reshape(per, C_CHUNK, C_DK) * jnp.exp(bl3 - b3)).astype(BF16)
            qd_ref[sl, :] = qd
            dec = jnp.exp(bl3)
            for j in range(per):
                dec_ref[u * per + j, 0:1, :] = dec[j]
            a = lax.dot_general(qd, ki, _TRANS_B, preferred_element_type=F32)
            a = jnp.where(same & ((ci <= ri) if fwd else (ci >= ri)), a, 0.0)
            att = a if att is None else att + a
            rhs += [jnp.concatenate([ke3[i] if i == j else zero_chunk for i in range(per)], axis=0)
                    for j in range(per)]
        acc_ref[sl, :] = jnp.dot(att.astype(BF16), v, preferred_element_type=F32)
        ds = lax.dot_general(v, jnp.concatenate(rhs, axis=1), _TRANS_A, preferred_element_type=F32)
        for j in range(per):
            dsf_ref[u * per + j] = ds[:, C_DK * j:C_DK * (j + 1)]
            dsb_ref[u * per + j] = ds[:, C_DK * (per + j):C_DK * (per + j + 1)]
        return carry

    lax.fori_loop(0, n_seq // GLA_BLOCK, block, 0)

    def sweep(qd_ref, ds_ref, dec_ref, s0_ref, s_out_ref, fwd):
        st_ref[...] = s0_ref[...].T if init else jnp.zeros((C_DV, C_DK), F32)

        def chunk(j, carry):
            c = j if fwd else nc - 1 - j
            sl = pl.ds(pl.multiple_of(c * C_CHUNK, C_CHUNK), C_CHUNK)
            st = st_ref[...]
            tot = acc_ref[sl, :] + lax.dot_general(qd_ref[sl, :], st.astype(BF16), _TRANS_B,
                                                   preferred_element_type=F32)
            st_ref[...] = st * dec_ref[c, 0:1, :] + ds_ref[c]
            if fwd:
                acc_ref[sl, :] = tot
            else:
                y = tot * lax.rsqrt(jnp.mean(tot * tot, axis=-1, keepdims=True) + EPS) * gn_ref[...]
                g = g_ref[sl, :].astype(F32)
                o_ref[sl, :] = (y * (g * (1.0 / (1.0 + jnp.exp(-g))))).astype(o_ref.dtype)
            return carry

        lax.fori_loop(0, nc, chunk, 0, unroll=2)
        if emit_state:
            s_out_ref[...] = st_ref[...].T

    sweep(qdf_ref, dsf_ref, decf_ref, s0f_ref, sf_ref, True)
    sweep(qdb_ref, dsb_ref, decb_ref, s0b_ref, sb_ref, False)


def _gla(cq, ck, cv, cg, la, gn, n_seq, layer, *, s0=None, emit_state):
    t = cq.shape[0]
    nb = t // n_seq
    nc = n_seq // C_CHUNK
    tok = lambda b, h: (b, h)
    in_specs = [pl.BlockSpec((n_seq, C_DK), tok), pl.BlockSpec((n_seq, C_DK), tok),
                pl.BlockSpec((n_seq, C_DV), tok), pl.BlockSpec((n_seq, C_DV), tok),
                pl.BlockSpec((n_seq, C_DK), tok), pl.BlockSpec((n_seq, C_DK), lambda b, h: (b, C_HEADS + h)),
                pl.BlockSpec((None, 1, C_DV), lambda b, h: (layer, 0, 0))]
    args = [cq, ck, cv, cg, la, la, gn]
    if s0 is not None:
        spec = pl.BlockSpec((None, None, None, C_DK, C_DV), lambda b, h: (b, layer, h, 0, 0))
        in_specs += [spec, spec]
        args += list(s0)
    out_specs = [pl.BlockSpec((n_seq, C_DV), tok)]
    out_shape = [_sds((t, C_HEADS * C_DV), BF16)]
    if emit_state:
        spec = pl.BlockSpec((None, None, C_DK, C_DV), lambda b, h: (b, h, 0, 0))
        out_specs += [spec, spec]
        out_shape += [_sds((nb, C_HEADS, C_DK, C_DV)), _sds((nb, C_HEADS, C_DK, C_DV))]
    return pl.pallas_call(
        functools.partial(_gla_body, n_seq=n_seq, init=s0 is not None, emit_state=emit_state),
        grid=(nb, C_HEADS), in_specs=in_specs, out_specs=out_specs, out_shape=out_shape,
        scratch_shapes=[pltpu.VMEM((n_seq, C_DV), F32), pltpu.VMEM((C_DV, C_DK), F32),
                        pltpu.VMEM((n_seq, C_DK), BF16), pltpu.VMEM((n_seq, C_DK), BF16),
                        pltpu.VMEM((nc, C_DV, C_DK), F32), pltpu.VMEM((nc, C_DV, C_DK), F32),
                        pltpu.VMEM((nc, 8, C_DK), F32), pltpu.VMEM((nc, 8, C_DK), F32)],
        compiler_params=_params("arbitrary", "arbitrary"), name="gla",
    )(*args)


def _outproj_body(oa_ref, ob_ref, oc_ref, x_ref, g1_ref, sh_ref, sc_ref, n2_ref, w_ref, wr_ref,
                  x1_ref, h2_ref, r_ref):
    mix = jnp.concatenate([oa_ref[p] for p in range(4)] + [ob_ref[p] for p in range(4)] + [oc_ref[...]], axis=1)
    y = jnp.dot(mix, w_ref[...], preferred_element_type=F32)
    x1 = x_ref[...] + g1_ref[...] * y
    x1_ref[...] = x1
    h = x1 * lax.rsqrt(jnp.mean(x1 * x1, axis=-1, keepdims=True) + EPS) * n2_ref[...]
    h = h * (1.0 + sc_ref[...]) + sh_ref[...]
    h2_ref[...] = h
    lg = jnp.dot(h.astype(BF16), wr_ref[...], preferred_element_type=F32)
    lane = lax.broadcasted_iota(I32, lg.shape, 1)
    big = jnp.int32(LANES)
    is_g = lane < N_GROUPS
    gmax = jnp.max(jnp.where(is_g, lg, -jnp.inf), axis=-1, keepdims=True)
    gidx = jnp.min(jnp.where(is_g & (lg == gmax), lane, big), axis=-1, keepdims=True)
    gw = 1.0 / jnp.sum(jnp.where(is_g, jnp.exp(lg - gmax), 0.0), axis=-1, keepdims=True)
    lo = N_GROUPS + E_PER_GROUP * gidx
    sel = (lane >= lo) & (lane < lo + E_PER_GROUP)
    v1 = jnp.max(jnp.where(sel, lg, -jnp.inf), axis=-1, keepdims=True)
    i1 = jnp.min(jnp.where(sel & (lg == v1), lane, big), axis=-1, keepdims=True)
    sel2 = sel & (lane != i1)
    v2 = jnp.max(jnp.where(sel2, lg, -jnp.inf), axis=-1, keepdims=True)
    i2 = jnp.min(jnp.where(sel2 & (lg == v2), lane, big), axis=-1, keepdims=True)
    e21 = jnp.exp(v2 - v1)
    w1 = gw / (1.0 + e21)
    w2 = gw * e21 / (1.0 + e21)
    out = jnp.where(lane == 0, (i1 - N_GROUPS).astype(F32),
                    jnp.where(lane == 1, (i2 - N_GROUPS).astype(F32),
                              jnp.where(lane == 2, w1, jnp.where(lane == 3, w2, 0.0))))
    r_ref[...] = out


def _outproj(oa, ob, oc, x, mods, mod_row_of_tile, n2, w_out_b, wr_b, layer):
    t, d = x.shape
    tm = TM_PROJ
    tok = lambda i: (i, 0)
    lay = lambda i: (layer, 0, 0)
    mod = lambda k: pl.BlockSpec((None, 1, d), lambda i: (mod_row_of_tile(i), 0, k))
    slab4 = pl.BlockSpec((4, tm, LANES), lambda i: (0, i, 0))
    return pl.pallas_call(
        _outproj_body,
        grid=(t // tm,),
        in_specs=[slab4, slab4, pl.BlockSpec((tm, 1024), tok), pl.BlockSpec((tm, d), tok),
                  mod(2), mod(3), mod(4), pl.BlockSpec((None, 1, d), lay),
                  _resident((None, d, d), lay), _resident((None, d, LANES), lay)],
        out_specs=[pl.BlockSpec((tm, d), tok), pl.BlockSpec((tm, d), tok), pl.BlockSpec((tm, LANES), tok)],
        out_shape=[_sds((t, d)), _sds((t, d)), _sds((t, LANES))],
        compiler_params=_params("arbitrary"), name="outproj",
    )(oa, ob, oc, x, mods, mods, mods, n2, w_out_b, wr_b)


def _moe_body(tile_ref, exp_ref, lo_ref, hi_ref, src0_ref, srcn_ref, dstp_ref, dstc_ref,
              h_ref, w1_ref, w3_ref, w2_ref, y_ref, xbuf, ybuf, gsem, ssem, *, n_steps, n_pairs):
    tm = xbuf.shape[1]
    s = pl.program_id(0)
    slot = s % 2
    other = 1 - slot

    def gather(idx_ref, b):
        for j in range(tm):
            pltpu.make_async_copy(h_ref.at[pl.ds(idx_ref[0, 0, j], 1)], xbuf.at[b, pl.ds(j, 1)], gsem.at[b]).start()

    def gathered(b):
        return pltpu.make_async_copy(h_ref.at[pl.ds(0, tm)], xbuf.at[b], gsem.at[b])

    def scatter(idx_ref, b, lo, hi):
        for j in range(tm):
            row = jnp.where((lo <= j) & (j < hi), idx_ref[0, 0, j], n_pairs + b * tm + j)
            pltpu.make_async_copy(ybuf.at[b, pl.ds(j, 1)], y_ref.at[pl.ds(row, 1)], ssem.at[b]).start()

    def scattered(b):
        return pltpu.make_async_copy(ybuf.at[b], y_ref.at[pl.ds(0, tm)], ssem.at[b])

    @pl.when(s == 0)
    def _():
        ybuf[...] = jnp.zeros(ybuf.shape, F32)
        scatter(dstc_ref, 0, 0, 0)
        gather(src0_ref, 0)

    gathered(slot).wait()
    gather(srcn_ref, other)
    prev = jnp.maximum(s - 1, 0)
    scatter(dstp_ref, other, jnp.where(s > 0, lo_ref[prev], 0), jnp.where(s > 0, hi_ref[prev], 0))
    x = xbuf[slot].astype(BF16)
    h1 = jnp.dot(x, w1_ref[...], preferred_element_type=F32)
    h3 = jnp.dot(x, w3_ref[...], preferred_element_type=F32)
    hid = (h1 * (1.0 / (1.0 + jnp.exp(-h1))) * h3).astype(BF16)
    y = jnp.dot(hid, w2_ref[...], preferred_element_type=F32)
    scattered(slot).wait()
    ybuf[slot] = y

    @pl.when(s == n_steps - 1)
    def _():
        scatter(dstc_ref, slot, lo_ref[s], hi_ref[s])
        scattered(other).wait()
        scattered(slot).wait()
        gathered(other).wait()


def _small_take(table, idx):
    n = table.shape[0]
    hit = idx[:, None] == jnp.arange(n, dtype=I32)[None, :]
    return jnp.sum(jnp.where(hit, table[None, :], 0), axis=1)


def _moe(h2, route, w1b, w3b, w2b, layer):
    t, d = h2.shape
    tm = TM_MOE
    n_pairs = 2 * t
    n_tiles = n_pairs // tm
    n_steps = n_tiles + N_EXPERTS - 1
    ef = route[:, :2].astype(I32).T.reshape(-1)
    order = jnp.argsort(ef, stable=True).astype(I32)
    experts = jnp.arange(N_EXPERTS, dtype=I32)
    counts = jnp.sum((ef[:, None] == experts[None, :]).astype(I32), axis=0)
    cend = jnp.cumsum(counts)
    cstart = cend - counts
    first_row = jnp.arange(n_tiles, dtype=I32) * tm
    e_first = jnp.sum((cend[None, :] <= first_row[:, None]).astype(I32), axis=1)
    e_last = jnp.sum((cend[None, :] <= (first_row + tm - 1)[:, None]).astype(I32), axis=1)
    n_sub = e_last - e_first + 1
    base = jnp.cumsum(n_sub) - n_sub
    step = jnp.arange(n_steps, dtype=I32)
    tile_of = jnp.clip(jnp.sum((base[None, :] <= step[:, None]).astype(I32), axis=1) - 1, 0, n_tiles - 1)
    exp_of = jnp.clip(_small_take(e_first, tile_of) + step - _small_take(base, tile_of), 0, N_EXPERTS - 1)
    used = step < jnp.sum(n_sub)
    row0 = tile_of * tm
    lo = jnp.where(used, jnp.clip(_small_take(cstart, exp_of) - row0, 0, tm), 0)
    hi = jnp.where(used, jnp.clip(_small_take(cend, exp_of) - row0, 0, tm), 0)
    src = (order % t).reshape(n_tiles, 1, tm)
    dst = order.reshape(n_tiles, 1, tm)

    def smem(f):
        return pl.BlockSpec((1, 1, tm), lambda s, tl, ex, lo_, hi_: (tl[f(s)], 0, 0), memory_space=pltpu.SMEM)

    wspec = lambda shape: pl.BlockSpec((None, None) + shape, lambda s, tl, ex, lo_, hi_: (layer, ex[s], 0, 0))
    grid_spec = pltpu.PrefetchScalarGridSpec(
        num_scalar_prefetch=4, grid=(n_steps,),
        in_specs=[smem(lambda s: s), smem(lambda s: jnp.minimum(s + 1, n_steps - 1)),
                  smem(lambda s: jnp.maximum(s - 1, 0)), smem(lambda s: s),
                  pl.BlockSpec(memory_space=pl.ANY),
                  wspec((d, D_EXPERT)), wspec((d, D_EXPERT)), wspec((D_EXPERT, d))],
        out_specs=pl.BlockSpec(memory_space=pl.ANY),
        scratch_shapes=[pltpu.VMEM((2, tm, d), F32), pltpu.VMEM((2, tm, d), F32),
                        pltpu.SemaphoreType.DMA((2,)), pltpu.SemaphoreType.DMA((2,))])
    return pl.pallas_call(
        functools.partial(_moe_body, n_steps=n_steps, n_pairs=n_pairs),
        grid_spec=grid_spec, out_shape=_sds((n_pairs + 2 * tm, d)),
        compiler_params=_params("arbitrary"), name="moe",
    )(tile_of, exp_of, lo, hi, src, src, dst, dst, h2, w1b, w3b, w2b)


def _combine_body(x1_ref, y0_ref, y1_ref, r_ref, g2_ref, o_ref):
    r = r_ref[...]
    lane = lax.broadcasted_iota(I32, r.shape, 1)
    w1 = jnp.sum(jnp.where(lane == 2, r, 0.0), axis=-1, keepdims=True)
    w2 = jnp.sum(jnp.where(lane == 3, r, 0.0), axis=-1, keepdims=True)
    o_ref[...] = x1_ref[...] + g2_ref[...] * (w1 * y0_ref[...] + w2 * y1_ref[...])


def _combine(x1, yg, route, mods, mod_row_of_tile):
    t, d = x1.shape
    tm = TM_PROJ
    nt = t // tm
    tok = lambda i: (i, 0)
    return pl.pallas_call(
        _combine_body,
        grid=(nt,),
        in_specs=[pl.BlockSpec((tm, d), tok), pl.BlockSpec((tm, d), tok),
                  pl.BlockSpec((tm, d), lambda i: (nt + i, 0)), pl.BlockSpec((tm, LANES), tok),
                  pl.BlockSpec((None, 1, d), lambda i: (mod_row_of_tile(i), 0, 5))],
        out_specs=pl.BlockSpec((tm, d), tok), out_shape=_sds((t, d)),
        compiler_params=_params("arbitrary"), name="combine",
    )(x1, yg, yg, route, mods)


def _rope_tables():
    t = jnp.arange(GRID_W * GRID_ROWS)
    pairs = HEAD_DIM // 4
    inv = 1.0 / (ROPE_THETA ** (jnp.arange(pairs, dtype=F32) * 2.0 / (HEAD_DIM // 2)))
    ang_r = (t // GRID_W).astype(F32)[:, None] * inv
    ang_c = (t % GRID_W).astype(F32)[:, None] * inv
    cos = jnp.concatenate([jnp.cos(ang_r)] * 2 + [jnp.cos(ang_c)] * 2, axis=1)
    sin = jnp.concatenate([-jnp.sin(ang_r), jnp.sin(ang_r), -jnp.sin(ang_c), jnp.sin(ang_c)], axis=1)
    return jnp.tile(cos, (1, 2)), jnp.tile(sin, (1, 2))


def _bias_pair_tiles(rpb_l):
    qc = jnp.arange(GRID_W)[:, None]
    kc = jnp.arange(GRID_W)[None, :]
    start_c = jnp.clip(qc - NA_COLS // 2, 0, GRID_W - NA_COLS)
    col_ok = (kc >= start_c) & (kc < start_c + NA_COLS)
    dc = jnp.clip(kc - qc + (NA_COLS - 1), 0, 2 * NA_COLS - 2)
    d = jnp.arange(-8, 23)
    tile = rpb_l[:, jnp.clip(d, 0, 2 * NA_ROWS - 2)][:, :, dc]
    ok = col_ok[None, None] & ((d >= 0) & (d <= 2 * NA_ROWS - 2))[None, :, None, None]
    tile = jnp.where(ok, tile, NEG_INF).astype(F32)
    return jnp.concatenate([tile[:, :-1], tile[:, 1:]], axis=-1)


def kernel(x_prompt, x_sample, c, cache_a_k, cache_a_v, cache_b_k, cache_b_v, state_c_fwd, state_c_bwd, c_ctx, norm1_g, norm2_g, w_mod, b_mod, w_in, q_norm_a, k_norm_a, sink_a, q_norm_b, k_norm_b, rpb_b, w_gk_up_f, b_gk_f, w_gk_up_b, b_gk_b, gla_norm_g, w_out, w_group, w_router, w1, w3, w2):
    depth = w_in.shape[0]
    nb_c, l_c, d = x_prompt.shape
    nb_s, n_s, _ = x_sample.shape
    past = cache_a_k.shape[2]
    tm = TM_PROJ

    w_in_b = jnp.pad(w_in, ((0, 0), (0, 0), (0, N_IN_PAD - N_IN))).astype(BF16)
    w_out_b = w_out.astype(BF16)
    wr_b = jnp.pad(jnp.concatenate([w_group, w_router], axis=-1),
                   ((0, 0), (0, 0), (0, LANES - N_GROUPS - N_EXPERTS))).astype(BF16)
    w1b, w3b, w2b = w1.astype(BF16), w3.astype(BF16), w2.astype(BF16)
    wgk = jnp.zeros((depth, 256, 1024), F32)
    wgk = wgk.at[:, :C_RANK, :512].set(w_gk_up_f).at[:, C_RANK:2 * C_RANK, 512:].set(w_gk_up_b).astype(BF16)
    cos, sin = _rope_tables()
    head_id = jnp.arange(512) // HEAD_DIM
    consts = dict(
        bd=(head_id[:, None] == head_id[None, :]).astype(BF16),
        gqa=jnp.tile(q_norm_a, (1, 8)).reshape(depth, 1, 512),
        gka=jnp.tile(k_norm_a, (1, 2)).reshape(depth, 1, LANES),
        gqb=jnp.tile(q_norm_b, (1, 8)).reshape(depth, 1, 512),
        gkb=jnp.tile(k_norm_b, (1, 8)).reshape(depth, 1, 512),
        cos=cos, sin=sin, wgk=wgk,
        bgk=jnp.concatenate([b_gk_f, b_gk_b], axis=-1).reshape(depth, 1, 1024))
    n1 = norm1_g.reshape(depth, 1, d)
    n2 = norm2_g.reshape(depth, 1, d)
    gn = gla_norm_g.reshape(depth, 1, C_DV)
    cond = jnp.zeros((16, d), F32).at[0].set(c_ctx).at[1:1 + nb_s].set(c)
    ctx_row = lambda i: 0
    lat_row = lambda i: 1 + i // (n_s // tm)
    cak = cache_a_k.reshape(nb_s, depth, 1, past, LANES).astype(BF16)
    cav = cache_a_v.reshape(nb_s, depth, 1, past, LANES).astype(BF16)
    cbk = cache_b_k.reshape(nb_s, depth, past, 4, LANES).transpose(0, 1, 3, 2, 4).astype(BF16)
    cbv = cache_b_v.reshape(nb_s, depth, past, 4, LANES).transpose(0, 1, 3, 2, 4).astype(BF16)

    xp = x_prompt.reshape(nb_c * l_c, d)
    xs = x_sample.reshape(nb_s * n_s, d)
    new = [[] for _ in range(6)]
    for l in range(depth):
        mods = _adaln(cond, w_mod, b_mod, l).reshape(16, 1, 6 * d)
        tp = _bias_pair_tiles(rpb_b[l])

        aq, ak, av, bq, bk, bv, cq, ck, cv, cg, la = _inproj(
            xp, mods, ctx_row, n1, w_in_b, consts, l, rope=False, kv_dtype=F32)
        oa = _attention("a_ctx", aq, ak.reshape(1, -1, LANES), av.reshape(1, -1, LANES), l_c, sink=sink_a, layer=l)
        ob = _attention("b_ctx", bq, bk, bv, l_c)
        oc, sf, sb = _gla(cq, ck, cv, cg, la, gn, l_c, l, emit_state=True)
        x1, h2, route = _outproj(oa, ob, oc, xp, mods, ctx_row, n2, w_out_b, wr_b, l)
        xp = _combine(x1, _moe(h2, route, w1b, w3b, w2b, l), route, mods, ctx_row)
        new[0].append(ak.reshape(nb_c, l_c, 2, HEAD_DIM))
        new[1].append(av.reshape(nb_c, l_c, 2, HEAD_DIM))
        new[2].append(bk.transpose(1, 0, 2).reshape(nb_c, l_c, 8, HEAD_DIM))
        new[3].append(bv.transpose(1, 0, 2).reshape(nb_c, l_c, 8, HEAD_DIM))
        new[4].append(sf)
        new[5].append(sb)

        aq, ak, av, bq, bk, bv, cq, ck, cv, cg, la = _inproj(
            xs, mods, lat_row, n1, w_in_b, consts, l, rope=True, kv_dtype=BF16)
        oa = _attention("a_lat", aq, ak.reshape(1, -1, LANES), av.reshape(1, -1, LANES), n_s,
                        prefix=(cak[:, l], cav[:, l]), sink=sink_a, layer=l)
        ob = _attention("b_lat", bq, bk, bv, n_s, prefix=(cbk[:, l], cbv[:, l]), tp=tp)
        oc, = _gla(cq, ck, cv, cg, la, gn, n_s, l, s0=(state_c_fwd, state_c_bwd), emit_state=False)
        x1, h2, route = _outproj(oa, ob, oc, xs, mods, lat_row, n2, w_out_b, wr_b, l)
        xs = _combine(x1, _moe(h2, route, w1b, w3b, w2b, l), route, mods, lat_row)

    outs = [jnp.stack(v, axis=1) for v in new]
    return (xp.reshape(nb_c, l_c, d), xs.reshape(nb_s, n_s, d), *outs)
```

```python
import functools

import jax
import jax.numpy as jnp
from jax import lax
from jax.experimental import pallas as pl
from jax.experimental.pallas import tpu as pltpu

F32 = jnp.float32
BF16 = jnp.bfloat16
I32 = jnp.int32

D_MODEL = 2048
HEAD_DIM = 64
EPS = 1e-6
NEG_INF = -1e30
ROPE_THETA = 10000.0
GRID_W = 64
GRID_ROWS = 32
A_WINDOW = 128
NA_ROWS = 8
NA_COLS = 16
C_HEADS = 4
C_DK = 128
C_DV = 256
C_RANK = 16
C_CHUNK = 64
GLA_BLOCK = 256
N_GROUPS = 4
E_PER_GROUP = 4
N_EXPERTS = 16
D_EXPERT = 512
LANES = 128
N_IN = 5408
N_IN_PAD = 5632
COL_AQ, COL_AKV, COL_BQ, COL_BK, COL_BV = 0, 512, 768, 1280, 1792
COL_CQ, COL_CK, COL_CV, COL_CG, COL_CR = 2304, 2816, 3328, 4352, 5376
VMEM_LIMIT = 56 * 1024 * 1024
TM_PROJ = 512
TM_MOE = 256

_TRANS_B = (((1,), (1,)), ((), ()))
_TRANS_A = (((0,), (0,)), ((), ()))


def _sds(shape, dtype=F32):
    return jax.ShapeDtypeStruct(shape, dtype)


def _params(*sem):
    return pltpu.CompilerParams(dimension_semantics=sem, vmem_limit_bytes=VMEM_LIMIT)


def _resident(shape, index_map):
    return pl.BlockSpec(shape, index_map, pipeline_mode=pl.Buffered(1))


def _adaln_body(c_ref, w_ref, b_ref, o_ref):
    c = c_ref[...]
    s = c * (1.0 / (1.0 + jnp.exp(-c)))
    o_ref[...] = jnp.dot(s.astype(BF16), w_ref[...].astype(BF16), preferred_element_type=F32) + b_ref[...]


def _adaln(cond, w_mod, b_mod, layer):
    d, n = w_mod.shape[1], w_mod.shape[2]
    tn = 1024
    return pl.pallas_call(
        _adaln_body,
        grid=(n // tn,),
        in_specs=[pl.BlockSpec((16, d), lambda j: (0, 0)),
                  pl.BlockSpec((None, d, tn), lambda j: (layer, 0, j)),
                  pl.BlockSpec((None, 1, tn), lambda j: (layer, 0, j))],
        out_specs=pl.BlockSpec((16, tn), lambda j: (0, j)),
        out_shape=_sds((16, n)),
        compiler_params=_params("arbitrary"),
        name="adaln",
    )(cond, w_mod, b_mod.reshape(b_mod.shape[0], 1, n))


def _head_norm(z, gain, bd):
    w = z.shape[1]
    ss = jnp.dot((z * z).astype(BF16), bd[:w, :w], preferred_element_type=F32)
    return z * lax.rsqrt(ss * (1.0 / HEAD_DIM) + EPS) * gain


def _rope(z, cos, sin_signed, lane):
    lower = (lane % 32) < 16
    partner = jnp.where(lower, pltpu.roll(z, LANES - 16, 1), pltpu.roll(z, 16, 1))
    return z * cos + partner * sin_signed


def _inproj_body(x_ref, sh_ref, sc_ref, n1_ref, w_ref, bd_ref, gqa_ref, gka_ref, gqb_ref, gkb_ref,
                 cos_ref, sin_ref, wgk_ref, bgk_ref,
                 aq_ref, ak_ref, av_ref, bq_ref, bk_ref, bv_ref, cq_ref, ck_ref, cv_ref, cg_ref, la_ref,
                 *, rope):
    x = x_ref[...]
    h = x * lax.rsqrt(jnp.mean(x * x, axis=-1, keepdims=True) + EPS) * n1_ref[...]
    h = h * (1.0 + sc_ref[...]) + sh_ref[...]
    hb = h.astype(BF16)
    bd = bd_ref[...]
    lane = lax.broadcasted_iota(I32, (x.shape[0], LANES), 1)
    upper = lane >= HEAD_DIM

    def proj(c0, n):
        return jnp.dot(hb, w_ref[:, c0:c0 + n], preferred_element_type=F32)

    def rot(s):
        return _rope(s, cos_ref[...], sin_ref[...], lane) if rope else s

    z = _head_norm(proj(COL_AQ, 512), gqa_ref[...], bd)
    for p in range(4):
        s = rot(z[:, LANES * p:LANES * (p + 1)]) * (HEAD_DIM ** -0.5)
        r = pltpu.roll(s, HEAD_DIM, 1)
        if p // 2 == 0:
            e0, e1 = jnp.where(upper, 0.0, s), jnp.where(upper, 0.0, r)
        else:
            e0, e1 = jnp.where(upper, r, 0.0), jnp.where(upper, s, 0.0)
        aq_ref[2 * p] = e0.astype(BF16)
        aq_ref[2 * p + 1] = e1.astype(BF16)
    z = proj(COL_AKV, 256)
    ak_ref[...] = rot(_head_norm(z[:, :LANES], gka_ref[...], bd)).astype(ak_ref.dtype)
    av_ref[...] = z[:, LANES:].astype(av_ref.dtype)
    z = _head_norm(proj(COL_BQ, 512), gqb_ref[...], bd) * (HEAD_DIM ** -0.5)
    for p in range(4):
        s = z[:, LANES * p:LANES * (p + 1)]
        bq_ref[2 * p] = jnp.where(upper, 0.0, s).astype(BF16)
        bq_ref[2 * p + 1] = jnp.where(upper, s, 0.0).astype(BF16)
    z = _head_norm(proj(COL_BK, 512), gkb_ref[...], bd)
    for p in range(4):
        bk_ref[p] = z[:, LANES * p:LANES * (p + 1)].astype(bk_ref.dtype)
    z = proj(COL_BV, 512)
    for p in range(4):
        bv_ref[p] = z[:, LANES * p:LANES * (p + 1)].astype(bv_ref.dtype)
    cq_ref[...] = (proj(COL_CQ, 512) * (C_DK ** -0.5)).astype(BF16)
    ck_ref[...] = proj(COL_CK, 512).astype(BF16)
    cv_ref[...] = proj(COL_CV, 1024).astype(BF16)
    cg_ref[...] = proj(COL_CG, 1024).astype(BF16)
    pre = jnp.dot(proj(COL_CR, 256).astype(BF16), wgk_ref[...], preferred_element_type=F32) + bgk_ref[...]
    la_ref[...] = (jnp.minimum(pre, 0.0) - jnp.log(1.0 + jnp.exp(-jnp.abs(pre)))) * (1.0 / 16.0)


def _inproj(x, mods, mod_row_of_tile, n1, w_in_b, consts, layer, *, rope, kv_dtype):
    t, d = x.shape
    tm = TM_PROJ
    seq_tiles = (GRID_W * GRID_ROWS) // tm
    tok = lambda i: (i, 0)
    full = lambda i: (0, 0)
    lay = lambda i: (layer, 0, 0)
    in_specs = [
        pl.BlockSpec((tm, d), tok),
        pl.BlockSpec((None, 1, d), lambda i: (mod_row_of_tile(i), 0, 0)),
        pl.BlockSpec((None, 1, d), lambda i: (mod_row_of_tile(i), 0, 1)),
        pl.BlockSpec((None, 1, d), lay),
        _resident((None, d, N_IN_PAD), lay),
        _resident((512, 512), full),
        pl.BlockSpec((None, 1, 512), lay), pl.BlockSpec((None, 1, LANES), lay),
        pl.BlockSpec((None, 1, 512), lay), pl.BlockSpec((None, 1, 512), lay),
        pl.BlockSpec((tm, LANES), lambda i: (i % seq_tiles, 0)),
        pl.BlockSpec((tm, LANES), lambda i: (i % seq_tiles, 0)),
        _resident((None, 256, 1024), lay),
        pl.BlockSpec((None, 1, 1024), lay),
    ]
    slab8 = pl.BlockSpec((8, tm, LANES), lambda i: (0, i, 0))
    slab4 = pl.BlockSpec((4, tm, LANES), lambda i: (0, i, 0))
    out_specs = [slab8, pl.BlockSpec((tm, LANES), tok), pl.BlockSpec((tm, LANES), tok),
                 slab8, slab4, slab4,
                 pl.BlockSpec((tm, 512), tok), pl.BlockSpec((tm, 512), tok),
                 pl.BlockSpec((tm, 1024), tok), pl.BlockSpec((tm, 1024), tok), pl.BlockSpec((tm, 1024), tok)]
    out_shape = [_sds((8, t, LANES), BF16), _sds((t, LANES), kv_dtype), _sds((t, LANES), kv_dtype),
                 _sds((8, t, LANES), BF16), _sds((4, t, LANES), kv_dtype), _sds((4, t, LANES), kv_dtype),
                 _sds((t, 512), BF16), _sds((t, 512), BF16), _sds((t, 1024), BF16), _sds((t, 1024), BF16),
                 _sds((t, 1024), F32)]
    return pl.pallas_call(
        functools.partial(_inproj_body, rope=rope),
        grid=(t // tm,), in_specs=in_specs, out_specs=out_specs, out_shape=out_shape,
        compiler_params=_params("arbitrary"), name="inproj",
    )(x, mods, mods, n1, w_in_b, consts["bd"], consts["gqa"], consts["gka"], consts["gqb"], consts["gkb"],
      consts["cos"], consts["sin"], consts["wgk"], consts["bgk"])


def _attn_body(*refs, kind, tq, kwin, n_seq, layer):
    is_a = kind in ("a_ctx", "a_lat")
    prefix = kind in ("a_lat", "b_lat")
    it = iter(refs)
    q_ref, k_ref, v_ref = next(it), next(it), next(it)
    kc_ref = vc_ref = sink_ref = tp_ref = None
    if prefix:
        kc_ref, vc_ref = next(it), next(it)
    if is_a:
        sink_ref = next(it)
    if kind == "b_lat":
        tp_ref = next(it)
    o_ref = next(it)
    i = pl.program_id(1)

    group = 4 if is_a else 1
    if kind == "a_lat":
        ws = jnp.clip(i * tq - A_WINDOW, 0, n_seq - kwin)
        ws = pl.multiple_of(ws, LANES)
        qpos = i * tq + lax.broadcasted_iota(I32, (group * tq, kwin), 0) % tq
        kpos = ws + lax.broadcasted_iota(I32, (group * tq, kwin), 1)
        allowed = jnp.abs(qpos - kpos) <= A_WINDOW
    elif kind == "b_lat":
        r0 = i * (tq // GRID_W)
        k0 = jnp.clip(r0 - NA_ROWS // 2, 0, GRID_ROWS - kwin // GRID_W)
        ws = pl.multiple_of(k0 * GRID_W, GRID_W)
        qrow = r0 + lax.broadcasted_iota(I32, (tq, kwin), 0) // GRID_W
        krow = k0 + lax.broadcasted_iota(I32, (tq, kwin), 1) // GRID_W
        start = jnp.clip(qrow - NA_ROWS // 2, 0, GRID_ROWS - NA_ROWS)
        allowed = (krow >= start) & (krow < start + NA_ROWS)
        row_mask = jnp.where(allowed, 0.0, NEG_INF)
    else:
        ws = 0
    lane = lax.broadcasted_iota(I32, (tq, LANES), 1)
    upper = lane >= HEAD_DIM

    def attend(q, ks, h, sink):
        kl = k_ref[ks, pl.ds(ws, kwin), :].astype(BF16)
        vl = v_ref[ks, pl.ds(ws, kwin), :].astype(BF16)
        s = lax.dot_general(q, kl, _TRANS_B, preferred_element_type=F32)
        if kind == "a_lat":
            s = jnp.where(allowed, s, NEG_INF)
        elif kind == "b_lat":
            rows = []
            for qr in range(tq // GRID_W):
                u0 = k0 - r0 - qr + 15
                rows.append(jnp.concatenate([tp_ref[h, u0 + 2 * m] for m in range(kwin // LANES)], axis=1))
            s = s + jnp.concatenate(rows, axis=0) + row_mask
        m = jnp.max(s, axis=-1, keepdims=True)
        if prefix:
            sc = lax.dot_general(q, kc_ref[ks], _TRANS_B, preferred_element_type=F32)
            m = jnp.maximum(m, jnp.max(sc, axis=-1, keepdims=True))
        if is_a:
            m = jnp.maximum(m, sink)
        e = jnp.exp(s - m)
        den = jnp.sum(e, axis=-1, keepdims=True)
        o = jnp.dot(e.astype(BF16), vl, preferred_element_type=F32)
        if prefix:
            ec = jnp.exp(sc - m)
            den = den + jnp.sum(ec, axis=-1, keepdims=True)
            o = o + jnp.dot(ec.astype(BF16), vc_ref[ks], preferred_element_type=F32)
        if is_a:
            den = den + jnp.exp(sink - m)
        return o * (1.0 / den)

    if is_a:
        head_of_row = lax.broadcasted_iota(I32, (group * tq, 1), 0) // tq
        for g in range(2):
            q = jnp.concatenate([q_ref[group * g + j] for j in range(group)], axis=0)
            sink = jnp.zeros((group * tq, 1), F32)
            for j in range(group):
                sink = jnp.where(head_of_row == j, sink_ref[layer, group * g + j], sink)
            o = attend(q, 0, None, sink)
            for pp in range(2):
                o0 = o[(2 * pp) * tq:(2 * pp + 1) * tq]
                o1 = o[(2 * pp + 1) * tq:(2 * pp + 2) * tq]
                if g == 0:
                    o1 = pltpu.roll(o1, HEAD_DIM, 1)
                else:
                    o0 = pltpu.roll(o0, HEAD_DIM, 1)
                o_ref[2 * g + pp] = jnp.where(upper, o1, o0).astype(o_ref.dtype)
    else:
        def pair(p):
            o0 = attend(q_ref[2 * p], p, 2 * p, None)
            o1 = attend(q_ref[2 * p + 1], p, 2 * p + 1, None)
            o_ref[p] = jnp.where(upper, o1, o0).astype(o_ref.dtype)

        if kind == "b_lat":
            def body(p, c):
                pair(p)
                return c
            lax.fori_loop(0, 4, body, 0)
        else:
            for p in range(4):
                pair(p)


def _attention(kind, q, k, v, n_seq, *, prefix=None, sink=None, tp=None, layer=0):
    t = q.shape[1]
    nb = t // n_seq
    s_k = k.shape[0]
    tq, kwin = {"a_ctx": (n_seq, n_seq), "b_ctx": (n_seq, n_seq),
                "a_lat": (A_WINDOW, 3 * A_WINDOW), "b_lat": (512, 1024)}[kind]
    nq = n_seq // tq
    in_specs = [pl.BlockSpec((8, tq, LANES), lambda b, i: (0, b * nq + i, 0)),
                pl.BlockSpec((s_k, n_seq, LANES), lambda b, i: (0, b, 0)),
                pl.BlockSpec((s_k, n_seq, LANES), lambda b, i: (0, b, 0))]
    args = [q, k, v]
    if prefix is not None:
        kc, vc = prefix
        spec = pl.BlockSpec((None,) + kc.shape[1:], lambda b, i: (b, 0, 0, 0))
        in_specs += [spec, spec]
        args += [kc, vc]
    if sink is not None:
        in_specs.append(pl.BlockSpec(memory_space=pltpu.SMEM))
        args.append(sink)
    if tp is not None:
        in_specs.append(_resident(tp.shape, lambda b, i: (0, 0, 0, 0)))
        args.append(tp)
    return pl.pallas_call(
        functools.partial(_attn_body, kind=kind, tq=tq, kwin=kwin, n_seq=n_seq, layer=layer),
        grid=(nb, nq), in_specs=in_specs,
        out_specs=pl.BlockSpec((4, tq, LANES), lambda b, i: (0, b * nq + i, 0)),
        out_shape=_sds((4, t, LANES), BF16),
        compiler_params=_params("arbitrary", "arbitrary"), name="attn_" + kind,
    )(*args)


def _gla_body(*refs, n_seq, init, emit_state):
    it = iter(refs)
    q_ref, k_ref, v_ref, g_ref, laf_ref, lab_ref, gn_ref = (next(it) for _ in range(7))
    s0f_ref = s0b_ref = sf_ref = sb_ref = None
    if init:
        s0f_ref, s0b_ref = next(it), next(it)
    o_ref = next(it)
    if emit_state:
        sf_ref, sb_ref = next(it), next(it)
    acc_ref, qdf_ref, qdb_ref, dsf_ref, dsb_ref, decf_ref, decb_ref = (next(it) for _ in range(7))
    nc = n_seq // C_CHUNK
    per = GLA_BLOCK // C_CHUNK
    row = lax.broadcasted_iota(I32, (GLA_BLOCK, C_DK), 0) % C_CHUNK
    ri = lax.broadcasted_iota(I32, (GLA_BLOCK, GLA_BLOCK), 0)
    ci = lax.broadcasted_iota(I32, (GLA_BLOCK, GLA_BLOCK), 1)
    same = (ri // C_CHUNK) == (ci // C_CHUNK)
    zero_chunk = jnp.zeros((C_CHUNK, C_DK), BF16)

    def decay_sums(la, fwd):
        x = la
        for s in (1, 2, 4, 8, 16, 32):
            if fwd:
                x = x + jnp.where(row >= s, pltpu.roll(x, s, 0), 0.0)
            else:
                x = x + jnp.where(row < C_CHUNK - s, pltpu.roll(x, GLA_BLOCK - s, 0), 0.0)
        return x

    def block(u, carry):
        sl = pl.ds(pl.multiple_of(u * GLA_BLOCK, GLA_BLOCK), GLA_BLOCK)
        q = q_ref[sl, :].astype(F32)
        k = k_ref[sl, :].astype(F32)
        v = v_ref[sl, :]
        att = None
        rhs = []
        for fwd, la_ref, qd_ref, dec_ref in ((True, laf_ref, qdf_ref, decf_ref), (False, lab_ref, qdb_ref, decb_ref)):
            b = decay_sums(la_ref[sl, :], fwd)
            b3 = b.reshape(per, C_CHUNK, C_DK)
            bl3 = b3[:, C_CHUNK - 1:C_CHUNK, :] if fwd else b3[:, 0:1, :]
            qd = (q * jnp.exp(b)).astype(BF16)
            ki = (k * jnp.exp(-b)).astype(BF16)
            ke3 = (k.reshape(per, C_CHUNK, C_DK) * jnp.exp(bl3 - b3)).astype(BF16)
            qd_ref[sl, :] = qd
            dec = jnp.exp(bl3)
            for j in range(per):
                dec_ref[u * per + j, 0:1, :] = dec[j]
            a = lax.dot_general(qd, ki, _TRANS_B, preferred_element_type=F32)
            a = jnp.where(same & ((ci <= ri) if fwd else (ci >= ri)), a, 0.0)
            att = a if att is None else att + a
            rhs += [jnp.concatenate([ke3[i] if i == j else zero_chunk for i in range(per)], axis=0)
                    for j in range(per)]
        acc_ref[sl, :] = jnp.dot(att.astype(BF16), v, preferred_element_type=F32)
        ds = lax.dot_general(v, jnp.concatenate(rhs, axis=1), _TRANS_A, preferred_element_type=F32)
        for j in range(per):
            dsf_ref[u * per + j] = ds[:, C_DK * j:C_DK * (j + 1)]
            dsb_ref[u * per + j] = ds[:, C_DK * (per + j):C_DK * (per + j + 1)]
        return carry

    lax.fori_loop(0, n_seq // GLA_BLOCK, block, 0)

    def states(ds_ref, dec_ref, s0_ref, s_out_ref, fwd):
        def chunk(j, st):
            c = j if fwd else nc - 1 - j
            inc = ds_ref[c]
            ds_ref[c] = st
            return st * dec_ref[c, 0:1, :] + inc

        st = lax.fori_loop(0, nc, chunk, s0_ref[...].T if init else jnp.zeros((C_DV, C_DK), F32), unroll=4)
        if emit_state:
            s_out_ref[...] = st.T

    states(dsf_ref, decf_ref, s0f_ref, sf_ref, True)
    states(dsb_ref, decb_ref, s0b_ref, sb_ref, False)

    def finish(u, carry):
        sl = pl.ds(pl.multiple_of(u * GLA_BLOCK, GLA_BLOCK), GLA_BLOCK)
        inter = []
        for j in range(per):
            c = u * per + j
            rows = pl.ds(pl.multiple_of(c * C_CHUNK, C_CHUNK), C_CHUNK)
            inter.append(
                lax.dot_general(qdf_ref[rows, :], dsf_ref[c].astype(BF16), _TRANS_B, preferred_element_type=F32)
                + lax.dot_general(qdb_ref[rows, :], dsb_ref[c].astype(BF16), _TRANS_B, preferred_element_type=F32))
        tot = acc_ref[sl, :] + jnp.concatenate(inter, axis=0)
        y = tot * lax.rsqrt(jnp.mean(tot * tot, axis=-1, keepdims=True) + EPS) * gn_ref[...]
        g = g_ref[sl, :].astype(F32)
        o_ref[sl, :] = (y * (g * (1.0 / (1.0 + jnp.exp(-g))))).astype(o_ref.dtype)
        return carry

    lax.fori_loop(0, n_seq // GLA_BLOCK, finish, 0)


def _gla(cq, ck, cv, cg, la, gn, n_seq, layer, *, s0=None, emit_state):
    t = cq.shape[0]
    nb = t // n_seq
    nc = n_seq // C_CHUNK
    tok = lambda b, h: (b, h)
    in_specs = [pl.BlockSpec((n_seq, C_DK), tok), pl.BlockSpec((n_seq, C_DK), tok),
                pl.BlockSpec((n_seq, C_DV), tok), pl.BlockSpec((n_seq, C_DV), tok),
                pl.BlockSpec((n_seq, C_DK), tok), pl.BlockSpec((n_seq, C_DK), lambda b, h: (b, C_HEADS + h)),
                pl.BlockSpec((None, 1, C_DV), lambda b, h: (layer, 0, 0))]
    args = [cq, ck, cv, cg, la, la, gn]
    if s0 is not None:
        spec = pl.BlockSpec((None, None, None, C_DK, C_DV), lambda b, h: (b, layer, h, 0, 0))
        in_specs += [spec, spec]
        args += list(s0)
    out_specs = [pl.BlockSpec((n_seq, C_DV), tok)]
    out_shape = [_sds((t, C_HEADS * C_DV), BF16)]
    if emit_state:
        spec = pl.BlockSpec((None, None, C_DK, C_DV), lambda b, h: (b, h, 0, 0))
        out_specs += [spec, spec]
        out_shape += [_sds((nb, C_HEADS, C_DK, C_DV)), _sds((nb, C_HEADS, C_DK, C_DV))]
    return pl.pallas_call(
        functools.partial(_gla_body, n_seq=n_seq, init=s0 is not None, emit_state=emit_state),
        grid=(nb, C_HEADS), in_specs=in_specs, out_specs=out_specs, out_shape=out_shape,
        scratch_shapes=[pltpu.VMEM((n_seq, C_DV), F32),
                        pltpu.VMEM((n_seq, C_DK), BF16), pltpu.VMEM((n_seq, C_DK), BF16),
                        pltpu.VMEM((nc, C_DV, C_DK), F32), pltpu.VMEM((nc, C_DV, C_DK), F32),
                        pltpu.VMEM((nc, 8, C_DK), F32), pltpu.VMEM((nc, 8, C_DK), F32)],
        compiler_params=_params("arbitrary", "arbitrary"), name="gla",
    )(*args)


def _outproj_body(oa_ref, ob_ref, oc_ref, x_ref, g1_ref, sh_ref, sc_ref, n2_ref, w_ref, wr_ref,
                  x1_ref, h2_ref, r_ref):
    mix = jnp.concatenate([oa_ref[p] for p in range(4)] + [ob_ref[p] for p in range(4)] + [oc_ref[...]], axis=1)
    y = jnp.dot(mix, w_ref[...], preferred_element_type=F32)
    x1 = x_ref[...] + g1_ref[...] * y
    x1_ref[...] = x1
    h = x1 * lax.rsqrt(jnp.mean(x1 * x1, axis=-1, keepdims=True) + EPS) * n2_ref[...]
    h = h * (1.0 + sc_ref[...]) + sh_ref[...]
    h2_ref[...] = h
    lg = jnp.dot(h.astype(BF16), wr_ref[...], preferred_element_type=F32)
    lane = lax.broadcasted_iota(I32, lg.shape, 1)
    big = jnp.int32(LANES)
    is_g = lane < N_GROUPS
    gmax = jnp.max(jnp.where(is_g, lg, -jnp.inf), axis=-1, keepdims=True)
    gidx = jnp.min(jnp.where(is_g & (lg == gmax), lane, big), axis=-1, keepdims=True)
    gw = 1.0 / jnp.sum(jnp.where(is_g, jnp.exp(lg - gmax), 0.0), axis=-1, keepdims=True)
    lo = N_GROUPS + E_PER_GROUP * gidx
    sel = (lane >= lo) & (lane < lo + E_PER_GROUP)
    v1 = jnp.max(jnp.where(sel, lg, -jnp.inf), axis=-1, keepdims=True)
    i1 = jnp.min(jnp.where(sel & (lg == v1), lane, big), axis=-1, keepdims=True)
    sel2 = sel & (lane != i1)
    v2 = jnp.max(jnp.where(sel2, lg, -jnp.inf), axis=-1, keepdims=True)
    i2 = jnp.min(jnp.where(sel2 & (lg == v2), lane, big), axis=-1, keepdims=True)
    e21 = jnp.exp(v2 - v1)
    w1 = gw / (1.0 + e21)
    w2 = gw * e21 / (1.0 + e21)
    out = jnp.where(lane == 0, (i1 - N_GROUPS).astype(F32),
                    jnp.where(lane == 1, (i2 - N_GROUPS).astype(F32),
                              jnp.where(lane == 2, w1, jnp.where(lane == 3, w2, 0.0))))
    r_ref[...] = out


def _outproj(oa, ob, oc, x, mods, mod_row_of_tile, n2, w_out_b, wr_b, layer):
    t, d = x.shape
    tm = TM_PROJ
    tok = lambda i: (i, 0)
    lay = lambda i: (layer, 0, 0)
    mod = lambda k: pl.BlockSpec((None, 1, d), lambda i: (mod_row_of_tile(i), 0, k))
    slab4 = pl.BlockSpec((4, tm, LANES), lambda i: (0, i, 0))
    return pl.pallas_call(
        _outproj_body,
        grid=(t // tm,),
        in_specs=[slab4, slab4, pl.BlockSpec((tm, 1024), tok), pl.BlockSpec((tm, d), tok),
                  mod(2), mod(3), mod(4), pl.BlockSpec((None, 1, d), lay),
                  _resident((None, d, d), lay), _resident((None, d, LANES), lay)],
        out_specs=[pl.BlockSpec((tm, d), tok), pl.BlockSpec((tm, d), tok), pl.BlockSpec((tm, LANES), tok)],
        out_shape=[_sds((t, d)), _sds((t, d)), _sds((t, LANES))],
        compiler_params=_params("arbitrary"), name="outproj",
    )(oa, ob, oc, x, mods, mods, mods, n2, w_out_b, wr_b)


def _moe_body(tile_ref, exp_ref, lo_ref, hi_ref, src0_ref, srcn_ref, dstp_ref, dstc_ref,
              h_ref, w1_ref, w3_ref, w2_ref, y_ref, xbuf, ybuf, gsem, ssem, *, n_steps, n_pairs):
    tm = xbuf.shape[1]
    s = pl.program_id(0)
    slot = s % 2
    other = 1 - slot

    def gather(idx_ref, b):
        for j in range(tm):
            pltpu.make_async_copy(h_ref.at[pl.ds(idx_ref[0, 0, j], 1)], xbuf.at[b, pl.ds(j, 1)], gsem.at[b]).start()

    def gathered(b):
        return pltpu.make_async_copy(h_ref.at[pl.ds(0, tm)], xbuf.at[b], gsem.at[b])

    def scatter(idx_ref, b, lo, hi):
        for j in range(tm):
            row = jnp.where((lo <= j) & (j < hi), idx_ref[0, 0, j], n_pairs + b * tm + j)
            pltpu.make_async_copy(ybuf.at[b, pl.ds(j, 1)], y_ref.at[pl.ds(row, 1)], ssem.at[b]).start()

    def scattered(b):
        return pltpu.make_async_copy(ybuf.at[b], y_ref.at[pl.ds(0, tm)], ssem.at[b])

    @pl.when(s == 0)
    def _():
        ybuf[...] = jnp.zeros(ybuf.shape, F32)
        scatter(dstc_ref, 0, 0, 0)
        gather(src0_ref, 0)

    gathered(slot).wait()
    gather(srcn_ref, other)
    prev = jnp.maximum(s - 1, 0)
    scatter(dstp_ref, other, jnp.where(s > 0, lo_ref[prev], 0), jnp.where(s > 0, hi_ref[prev], 0))
    x = xbuf[slot].astype(BF16)
    h1 = jnp.dot(x, w1_ref[...], preferred_element_type=F32)
    h3 = jnp.dot(x, w3_ref[...], preferred_element_type=F32)
    hid = (h1 * (1.0 / (1.0 + jnp.exp(-h1))) * h3).astype(BF16)
    y = jnp.dot(hid, w2_ref[...], preferred_element_type=F32)
    scattered(slot).wait()
    ybuf[slot] = y

    @pl.when(s == n_steps - 1)
    def _():
        scatter(dstc_ref, slot, lo_ref[s], hi_ref[s])
        scattered(other).wait()
        scattered(slot).wait()
        gathered(other).wait()


def _small_take(table, idx):
    n = table.shape[0]
    hit = idx[:, None] == jnp.arange(n, dtype=I32)[None, :]
    return jnp.sum(jnp.where(hit, table[None, :], 0), axis=1)


def _moe(h2, route, w1b, w3b, w2b, layer):
    t, d = h2.shape
    tm = TM_MOE
    n_pairs = 2 * t
    n_tiles = n_pairs // tm
    n_steps = n_tiles + N_EXPERTS - 1
    ef = route[:, :2].astype(I32).T.reshape(-1)
    order = jnp.argsort(ef, stable=True).astype(I32)
    experts = jnp.arange(N_EXPERTS, dtype=I32)
    counts = jnp.sum((ef[:, None] == experts[None, :]).astype(I32), axis=0)
    cend = jnp.cumsum(counts)
    cstart = cend - counts
    first_row = jnp.arange(n_tiles, dtype=I32) * tm
    e_first = jnp.sum((cend[None, :] <= first_row[:, None]).astype(I32), axis=1)
    e_last = jnp.sum((cend[None, :] <= (first_row + tm - 1)[:, None]).astype(I32), axis=1)
    n_sub = e_last - e_first + 1
    base = jnp.cumsum(n_sub) - n_sub
    step = jnp.arange(n_steps, dtype=I32)
    tile_of = jnp.clip(jnp.sum((base[None, :] <= step[:, None]).astype(I32), axis=1) - 1, 0, n_tiles - 1)
    exp_of = jnp.clip(_small_take(e_first, tile_of) + step - _small_take(base, tile_of), 0, N_EXPERTS - 1)
    used = step < jnp.sum(n_sub)
    row0 = tile_of * tm
    lo = jnp.where(used, jnp.clip(_small_take(cstart, exp_of) - row0, 0, tm), 0)
    hi = jnp.where(used, jnp.clip(_small_take(cend, exp_of) - row0, 0, tm), 0)
    src = (order % t).reshape(n_tiles, 1, tm)
    dst = order.reshape(n_tiles, 1, tm)

    def smem(f):
        return pl.BlockSpec((1, 1, tm), lambda s, tl, ex, lo_, hi_: (tl[f(s)], 0, 0), memory_space=pltpu.SMEM)

    wspec = lambda shape: pl.BlockSpec((None, None) + shape, lambda s, tl, ex, lo_, hi_: (layer, ex[s], 0, 0))
    grid_spec = pltpu.PrefetchScalarGridSpec(
        num_scalar_prefetch=4, grid=(n_steps,),
        in_specs=[smem(lambda s: s), smem(lambda s: jnp.minimum(s + 1, n_steps - 1)),
                  smem(lambda s: jnp.maximum(s - 1, 0)), smem(lambda s: s),
                  pl.BlockSpec(memory_space=pl.ANY),
                  wspec((d, D_EXPERT)), wspec((d, D_EXPERT)), wspec((D_EXPERT, d))],
        out_specs=pl.BlockSpec(memory_space=pl.ANY),
        scratch_shapes=[pltpu.VMEM((2, tm, d), F32), pltpu.VMEM((2, tm, d), F32),
                        pltpu.SemaphoreType.DMA((2,)), pltpu.SemaphoreType.DMA((2,))])
    return pl.pallas_call(
        functools.partial(_moe_body, n_steps=n_steps, n_pairs=n_pairs),
        grid_spec=grid_spec, out_shape=_sds((n_pairs + 2 * tm, d)),
        compiler_params=_params("arbitrary"), name="moe",
    )(tile_of, exp_of, lo, hi, src, src, dst, dst, h2, w1b, w3b, w2b)


def _combine_body(x1_ref, y0_ref, y1_ref, r_ref, g2_ref, o_ref):
    r = r_ref[...]
    lane = lax.broadcasted_iota(I32, r.shape, 1)
    w1 = jnp.sum(jnp.where(lane == 2, r, 0.0), axis=-1, keepdims=True)
    w2 = jnp.sum(jnp.where(lane == 3, r, 0.0), axis=-1, keepdims=True)
    o_ref[...] = x1_ref[...] + g2_ref[...] * (w1 * y0_ref[...] + w2 * y1_ref[...])


def _combine(x1, yg, route, mods, mod_row_of_tile):
    t, d = x1.shape
    tm = TM_PROJ
    nt = t // tm
    tok = lambda i: (i, 0)
    return pl.pallas_call(
        _combine_body,
        grid=(nt,),
        in_specs=[pl.BlockSpec((tm, d), tok), pl.BlockSpec((tm, d), tok),
                  pl.BlockSpec((tm, d), lambda i: (nt + i, 0)), pl.BlockSpec((tm, LANES), tok),
                  pl.BlockSpec((None, 1, d), lambda i: (mod_row_of_tile(i), 0, 5))],
        out_specs=pl.BlockSpec((tm, d), tok), out_shape=_sds((t, d)),
        compiler_params=_params("arbitrary"), name="combine",
    )(x1, yg, yg, route, mods)


def _rope_tables():
    t = jnp.arange(GRID_W * GRID_ROWS)
    pairs = HEAD_DIM // 4
    inv = 1.0 / (ROPE_THETA ** (jnp.arange(pairs, dtype=F32) * 2.0 / (HEAD_DIM // 2)))
    ang_r = (t // GRID_W).astype(F32)[:, None] * inv
    ang_c = (t % GRID_W).astype(F32)[:, None] * inv
    cos = jnp.concatenate([jnp.cos(ang_r)] * 2 + [jnp.cos(ang_c)] * 2, axis=1)
    sin = jnp.concatenate([-jnp.sin(ang_r), jnp.sin(ang_r), -jnp.sin(ang_c), jnp.sin(ang_c)], axis=1)
    return jnp.tile(cos, (1, 2)), jnp.tile(sin, (1, 2))


def _bias_pair_tiles(rpb_l):
    qc = jnp.arange(GRID_W)[:, None]
    kc = jnp.arange(GRID_W)[None, :]
    start_c = jnp.clip(qc - NA_COLS // 2, 0, GRID_W - NA_COLS)
    col_ok = (kc >= start_c) & (kc < start_c + NA_COLS)
    dc = jnp.clip(kc - qc + (NA_COLS - 1), 0, 2 * NA_COLS - 2)
    d = jnp.arange(-8, 23)
    tile = rpb_l[:, jnp.clip(d, 0, 2 * NA_ROWS - 2)][:, :, dc]
    ok = col_ok[None, None] & ((d >= 0) & (d <= 2 * NA_ROWS - 2))[None, :, None, None]
    tile = jnp.where(ok, tile, NEG_INF).astype(F32)
    return jnp.concatenate([tile[:, :-1], tile[:, 1:]], axis=-1)


def kernel(x_prompt, x_sample, c, cache_a_k, cache_a_v, cache_b_k, cache_b_v, state_c_fwd, state_c_bwd, c_ctx, norm1_g, norm2_g, w_mod, b_mod, w_in, q_norm_a, k_norm_a, sink_a, q_norm_b, k_norm_b, rpb_b, w_gk_up_f, b_gk_f, w_gk_up_b, b_gk_b, gla_norm_g, w_out, w_group, w_router, w1, w3, w2):
    depth = w_in.shape[0]
    nb_c, l_c, d = x_prompt.shape
    nb_s, n_s, _ = x_sample.shape
    past = cache_a_k.shape[2]
    tm = TM_PROJ

    w_in_b = jnp.pad(w_in, ((0, 0), (0, 0), (0, N_IN_PAD - N_IN))).astype(BF16)
    w_out_b = w_out.astype(BF16)
    wr_b = jnp.pad(jnp.concatenate([w_group, w_router], axis=-1),
                   ((0, 0), (0, 0), (0, LANES - N_GROUPS - N_EXPERTS))).astype(BF16)
    w1b, w3b, w2b = w1.astype(BF16), w3.astype(BF16), w2.astype(BF16)
    wgk = jnp.zeros((depth, 256, 1024), F32)
    wgk = wgk.at[:, :C_RANK, :512].set(w_gk_up_f).at[:, C_RANK:2 * C_RANK, 512:].set(w_gk_up_b).astype(BF16)
    cos, sin = _rope_tables()
    head_id = jnp.arange(512) // HEAD_DIM
    consts = dict(
        bd=(head_id[:, None] == head_id[None, :]).astype(BF16),
        gqa=jnp.tile(q_norm_a, (1, 8)).reshape(depth, 1, 512),
        gka=jnp.tile(k_norm_a, (1, 2)).reshape(depth, 1, LANES),
        gqb=jnp.tile(q_norm_b, (1, 8)).reshape(depth, 1, 512),
        gkb=jnp.tile(k_norm_b, (1, 8)).reshape(depth, 1, 512),
        cos=cos, sin=sin, wgk=wgk,
        bgk=jnp.concatenate([b_gk_f, b_gk_b], axis=-1).reshape(depth, 1, 1024))
    n1 = norm1_g.reshape(depth, 1, d)
    n2 = norm2_g.reshape(depth, 1, d)
    gn = gla_norm_g.reshape(depth, 1, C_DV)
    cond = jnp.zeros((16, d), F32).at[0].set(c_ctx).at[1:1 + nb_s].set(c)
    ctx_row = lambda i: 0
    lat_row = lambda i: 1 + i // (n_s // tm)
    cak = cache_a_k.reshape(nb_s, depth, 1, past, LANES).astype(BF16)
    cav = cache_a_v.reshape(nb_s, depth, 1, past, LANES).astype(BF16)
    cbk = cache_b_k.reshape(nb_s, depth, past, 4, LANES).transpose(0, 1, 3, 2, 4).astype(BF16)
    cbv = cache_b_v.reshape(nb_s, depth, past, 4, LANES).transpose(0, 1, 3, 2, 4).astype(BF16)

    xp = x_prompt.reshape(nb_c * l_c, d)
    xs = x_sample.reshape(nb_s * n_s, d)
    new = [[] for _ in range(6)]
    for l in range(depth):
        mods = _adaln(cond, w_mod, b_mod, l).reshape(16, 1, 6 * d)
        tp = _bias_pair_tiles(rpb_b[l])

        aq, ak, av, bq, bk, bv, cq, ck, cv, cg, la = _inproj(
            xp, mods, ctx_row, n1, w_in_b, consts, l, rope=False, kv_dtype=F32)
        oa = _attention("a_ctx", aq, ak.reshape(1, -1, LANES), av.reshape(1, -1, LANES), l_c, sink=sink_a, layer=l)
        ob = _attention("b_ctx", bq, bk, bv, l_c)
        oc, sf, sb = _gla(cq, ck, cv, cg, la, gn, l_c, l, emit_state=True)
        x1, h2, route = _outproj(oa, ob, oc, xp, mods, ctx_row, n2, w_out_b, wr_b, l)
        xp = _combine(x1, _moe(h2, route, w1b, w3b, w2b, l), route, mods, ctx_row)
        new[0].append(ak.reshape(nb_c, l_c, 2, HEAD_DIM))
        new[1].append(av.reshape(nb_c, l_c, 2, HEAD_DIM))
        new[2].append(bk.transpose(1, 0, 2).reshape(nb_c, l_c, 8, HEAD_DIM))
        new[3].append(bv.transpose(1, 0, 2).reshape(nb_c, l_c, 8, HEAD_DIM))
        new[4].append(sf)
        new[5].append(sb)

        aq, ak, av, bq, bk, bv, cq, ck, cv, cg, la = _inproj(
            xs, mods, lat_row, n1, w_in_b, consts, l, rope=True, kv_dtype=BF16)
        oa = _attention("a_lat", aq, ak.reshape(1, -1, LANES), av.reshape(1, -1, LANES), n_s,
                        prefix=(cak[:, l], cav[:, l]), sink=sink_a, layer=l)
        ob = _attention("b_lat", bq, bk, bv, n_s, prefix=(cbk[:, l], cbv[:, l]), tp=tp)
        oc, = _gla(cq, ck, cv, cg, la, gn, n_s, l, s0=(state_c_fwd, state_c_bwd), emit_state=False)
        x1, h2, route = _outproj(oa, ob, oc, xs, mods, lat_row, n2, w_out_b, wr_b, l)
        xs = _combine(x1, _moe(h2, route, w1b, w3b, w2b, l), route, mods, lat_row)

    outs = [jnp.stack(v, axis=1) for v in new]
    return (xp.reshape(nb_c, l_c, d), xs.reshape(nb_s, n_s, d), *outs)
```

```python
import functools

import jax
import jax.numpy as jnp
from jax import lax
from jax.experimental import pallas as pl
from jax.experimental.pallas import tpu as pltpu

F32 = jnp.float32
BF16 = jnp.bfloat16
I32 = jnp.int32

D_MODEL = 2048
HEAD_DIM = 64
EPS = 1e-6
NEG_INF = -1e30
ROPE_THETA = 10000.0
GRID_W = 64
GRID_ROWS = 32
A_WINDOW = 128
NA_ROWS = 8
NA_COLS = 16
C_HEADS = 4
C_DK = 128
C_DV = 256
C_RANK = 16
C_CHUNK = 64
GLA_BLOCK = 256
N_GROUPS = 4
E_PER_GROUP = 4
N_EXPERTS = 16
D_EXPERT = 512
LANES = 128
N_IN = 5408
N_IN_PAD = 5632
COL_AQ, COL_AKV, COL_BQ, COL_BK, COL_BV = 0, 512, 768, 1280, 1792
COL_CQ, COL_CK, COL_CV, COL_CG, COL_CR = 2304, 2816, 3328, 4352, 5376
VMEM_LIMIT = 56 * 1024 * 1024
TM_PROJ = 512
TM_MOE = 256

_TRANS_B = (((1,), (1,)), ((), ()))
_TRANS_A = (((0,), (0,)), ((), ()))


def _sds(shape, dtype=F32):
    return jax.ShapeDtypeStruct(shape, dtype)


def _params(*sem):
    return pltpu.CompilerParams(dimension_semantics=sem, vmem_limit_bytes=VMEM_LIMIT)


def _resident(shape, index_map):
    return pl.BlockSpec(shape, index_map, pipeline_mode=pl.Buffered(1))


def _adaln_body(c_ref, w_ref, b_ref, o_ref):
    c = c_ref[...]
    s = c * (1.0 / (1.0 + jnp.exp(-c)))
    o_ref[...] = jnp.dot(s.astype(BF16), w_ref[...].astype(BF16), preferred_element_type=F32) + b_ref[...]


def _adaln(cond, w_mod, b_mod, layer):
    d, n = w_mod.shape[1], w_mod.shape[2]
    tn = 1024
    return pl.pallas_call(
        _adaln_body,
        grid=(n // tn,),
        in_specs=[pl.BlockSpec((16, d), lambda j: (0, 0)),
                  pl.BlockSpec((None, d, tn), lambda j: (layer, 0, j)),
                  pl.BlockSpec((None, 1, tn), lambda j: (layer, 0, j))],
        out_specs=pl.BlockSpec((16, tn), lambda j: (0, j)),
        out_shape=_sds((16, n)),
        compiler_params=_params("arbitrary"),
        name="adaln",
    )(cond, w_mod, b_mod.reshape(b_mod.shape[0], 1, n))


def _head_norm(z, gain, bd):
    w = z.shape[1]
    ss = jnp.dot((z * z).astype(BF16), bd[:w, :w], preferred_element_type=F32)
    return z * lax.rsqrt(ss * (1.0 / HEAD_DIM) + EPS) * gain


def _rope(z, cos, sin_signed, lane):
    lower = (lane % 32) < 16
    partner = jnp.where(lower, pltpu.roll(z, LANES - 16, 1), pltpu.roll(z, 16, 1))
    return z * cos + partner * sin_signed


def _inproj_body(x_ref, sh_ref, sc_ref, n1_ref, w_ref, bd_ref, gqa_ref, gka_ref, gqb_ref, gkb_ref,
                 cos_ref, sin_ref, wgk_ref, bgk_ref,
                 *rest, rope, n_prev, seq_len):
    prev_refs = rest[:4] if n_prev else ()
    rest = rest[4 if n_prev else 0:]
    aq_ref, ak_ref, av_ref, bq_ref, bk_ref, bv_ref, cq_ref, ck_ref, cv_ref, cg_ref, la_ref = rest[:11]
    stack_refs = rest[11:]

    def emit(idx, val):
        if not seq_len:
            return
        ref = stack_refs[idx]
        if n_prev:
            ref[:, :n_prev] = prev_refs[idx][...]
        ref[:, n_prev] = val.reshape(val.shape[0] // seq_len, seq_len, val.shape[1])

    x = x_ref[...]
    h = x * lax.rsqrt(jnp.mean(x * x, axis=-1, keepdims=True) + EPS) * n1_ref[...]
    h = h * (1.0 + sc_ref[...]) + sh_ref[...]
    hb = h.astype(BF16)
    bd = bd_ref[...]
    lane = lax.broadcasted_iota(I32, (x.shape[0], LANES), 1)
    upper = lane >= HEAD_DIM

    def proj(c0, n):
        return jnp.dot(hb, w_ref[:, c0:c0 + n], preferred_element_type=F32)

    def rot(s):
        return _rope(s, cos_ref[...], sin_ref[...], lane) if rope else s

    z = _head_norm(proj(COL_AQ, 512), gqa_ref[...], bd)
    for p in range(4):
        s = rot(z[:, LANES * p:LANES * (p + 1)]) * (HEAD_DIM ** -0.5)
        r = pltpu.roll(s, HEAD_DIM, 1)
        if p // 2 == 0:
            e0, e1 = jnp.where(upper, 0.0, s), jnp.where(upper, 0.0, r)
        else:
            e0, e1 = jnp.where(upper, r, 0.0), jnp.where(upper, s, 0.0)
        aq_ref[2 * p] = e0.astype(BF16)
        aq_ref[2 * p + 1] = e1.astype(BF16)
    z = proj(COL_AKV, 256)
    k = rot(_head_norm(z[:, :LANES], gka_ref[...], bd))
    ak_ref[...] = k.astype(BF16)
    av_ref[...] = z[:, LANES:].astype(BF16)
    emit(0, k)
    emit(1, z[:, LANES:])
    z = _head_norm(proj(COL_BQ, 512), gqb_ref[...], bd) * (HEAD_DIM ** -0.5)
    for p in range(4):
        s = z[:, LANES * p:LANES * (p + 1)]
        bq_ref[2 * p] = jnp.where(upper, 0.0, s).astype(BF16)
        bq_ref[2 * p + 1] = jnp.where(upper, s, 0.0).astype(BF16)
    z = _head_norm(proj(COL_BK, 512), gkb_ref[...], bd)
    for p in range(4):
        bk_ref[p] = z[:, LANES * p:LANES * (p + 1)].astype(BF16)
    emit(2, z)
    z = proj(COL_BV, 512)
    for p in range(4):
        bv_ref[p] = z[:, LANES * p:LANES * (p + 1)].astype(BF16)
    emit(3, z)
    cq_ref[...] = (proj(COL_CQ, 512) * (C_DK ** -0.5)).astype(BF16)
    ck_ref[...] = proj(COL_CK, 512).astype(BF16)
    cv_ref[...] = proj(COL_CV, 1024).astype(BF16)
    cg_ref[...] = proj(COL_CG, 1024).astype(BF16)
    pre = jnp.dot(proj(COL_CR, 256).astype(BF16), wgk_ref[...], preferred_element_type=F32) + bgk_ref[...]
    la_ref[...] = (jnp.minimum(pre, 0.0) - jnp.log(1.0 + jnp.exp(-jnp.abs(pre)))) * (1.0 / 16.0)


def _inproj(x, mods, mod_row_of_tile, n1, w_in_b, consts, layer, *, rope, tm, seq_len=0, prev=None):
    t, d = x.shape
    n_prev = 0 if prev is None else prev[0].shape[1]
    seq_tiles = (GRID_W * GRID_ROWS) // tm
    tok = lambda i: (i, 0)
    full = lambda i: (0, 0)
    lay = lambda i: (layer, 0, 0)
    in_specs = [
        pl.BlockSpec((tm, d), tok),
        pl.BlockSpec((None, 1, d), lambda i: (mod_row_of_tile(i), 0, 0)),
        pl.BlockSpec((None, 1, d), lambda i: (mod_row_of_tile(i), 0, 1)),
        pl.BlockSpec((None, 1, d), lay),
        _resident((None, d, N_IN_PAD), lay),
        _resident((512, 512), full),
        pl.BlockSpec((None, 1, 512), lay), pl.BlockSpec((None, 1, LANES), lay),
        pl.BlockSpec((None, 1, 512), lay), pl.BlockSpec((None, 1, 512), lay),
        pl.BlockSpec((tm, LANES), lambda i: (i % seq_tiles, 0)),
        pl.BlockSpec((tm, LANES), lambda i: (i % seq_tiles, 0)),
        _resident((None, 256, 1024), lay),
        pl.BlockSpec((None, 1, 1024), lay),
    ]
    slab8 = pl.BlockSpec((8, tm, LANES), lambda i: (0, i, 0))
    slab4 = pl.BlockSpec((4, tm, LANES), lambda i: (0, i, 0))
    out_specs = [slab8, pl.BlockSpec((tm, LANES), tok), pl.BlockSpec((tm, LANES), tok),
                 slab8, slab4, slab4,
                 pl.BlockSpec((tm, 512), tok), pl.BlockSpec((tm, 512), tok),
                 pl.BlockSpec((tm, 1024), tok), pl.BlockSpec((tm, 1024), tok), pl.BlockSpec((tm, 1024), tok)]
    out_shape = [_sds((8, t, LANES), BF16), _sds((t, LANES), BF16), _sds((t, LANES), BF16),
                 _sds((8, t, LANES), BF16), _sds((4, t, LANES), BF16), _sds((4, t, LANES), BF16),
                 _sds((t, 512), BF16), _sds((t, 512), BF16), _sds((t, 1024), BF16), _sds((t, 1024), BF16),
                 _sds((t, 1024), F32)]
    args = [x, mods, mods, n1, w_in_b, consts["bd"], consts["gqa"], consts["gka"], consts["gqb"], consts["gkb"],
            consts["cos"], consts["sin"], consts["wgk"], consts["bgk"]]
    if seq_len:
        nseq = tm // seq_len
        for j, width in enumerate((LANES, LANES, 512, 512)):
            if n_prev:
                in_specs.append(pl.BlockSpec((nseq, n_prev, seq_len, width), lambda i: (i, 0, 0, 0)))
                args.append(prev[j])
            out_specs.append(pl.BlockSpec((nseq, n_prev + 1, seq_len, width), lambda i: (i, 0, 0, 0)))
            out_shape.append(_sds((t // seq_len, n_prev + 1, seq_len, width)))
    return pl.pallas_call(
        functools.partial(_inproj_body, rope=rope, n_prev=n_prev, seq_len=seq_len),
        grid=(t // tm,), in_specs=in_specs, out_specs=out_specs, out_shape=out_shape,
        compiler_params=_params("arbitrary"), name="inproj",
    )(*args)


def _attn_body(*refs, kind, tq, kwin, n_seq, layer):
    is_a = kind in ("a_ctx", "a_lat")
    prefix = kind in ("a_lat", "b_lat")
    it = iter(refs)
    q_ref, k_ref, v_ref = next(it), next(it), next(it)
    kc_ref = vc_ref = sink_ref = tp_ref = None
    if prefix:
        kc_ref, vc_ref = next(it), next(it)
    if is_a:
        sink_ref = next(it)
    if kind == "b_lat":
        tp_ref = next(it)
    o_ref = next(it)
    i = pl.program_id(1)

    group = 4 if is_a else 1
    if kind == "a_lat":
        ws = jnp.clip(i * tq - A_WINDOW, 0, n_seq - kwin)
        ws = pl.multiple_of(ws, LANES)
        qpos = i * tq + lax.broadcasted_iota(I32, (group * tq, kwin), 0) % tq
        kpos = ws + lax.broadcasted_iota(I32, (group * tq, kwin), 1)
        allowed = jnp.abs(qpos - kpos) <= A_WINDOW
    elif kind == "b_lat":
        r0 = i * (tq // GRID_W)
        k0 = jnp.clip(r0 - NA_ROWS // 2, 0, GRID_ROWS - kwin // GRID_W)
        ws = pl.multiple_of(k0 * GRID_W, GRID_W)
        qrow = r0 + lax.broadcasted_iota(I32, (tq, kwin), 0) // GRID_W
        krow = k0 + lax.broadcasted_iota(I32, (tq, kwin), 1) // GRID_W
        start = jnp.clip(qrow - NA_ROWS // 2, 0, GRID_ROWS - NA_ROWS)
        allowed = (krow >= start) & (krow < start + NA_ROWS)
        row_mask = jnp.where(allowed, 0.0, NEG_INF)
    else:
        ws = 0
    lane = lax.broadcasted_iota(I32, (tq, LANES), 1)
    upper = lane >= HEAD_DIM

    def attend(q, ks, h, sink):
        kl = k_ref[ks, pl.ds(ws, kwin), :].astype(BF16)
        vl = v_ref[ks, pl.ds(ws, kwin), :].astype(BF16)
        s = lax.dot_general(q, kl, _TRANS_B, preferred_element_type=F32)
        if kind == "a_lat":
            s = jnp.where(allowed, s, NEG_INF)
        elif kind == "b_lat":
            rows = []
            for qr in range(tq // GRID_W):
                u0 = k0 - r0 - qr + 15
                rows.append(jnp.concatenate([tp_ref[h, u0 + 2 * m] for m in range(kwin // LANES)], axis=1))
            s = s + jnp.concatenate(rows, axis=0) + row_mask
        m = jnp.max(s, axis=-1, keepdims=True)
        if prefix:
            sc = lax.dot_general(q, kc_ref[ks], _TRANS_B, preferred_element_type=F32)
            m = jnp.maximum(m, jnp.max(sc, axis=-1, keepdims=True))
        if is_a:
            m = jnp.maximum(m, sink)
        e = jnp.exp(s - m)
        den = jnp.sum(e, axis=-1, keepdims=True)
        o = jnp.dot(e.astype(BF16), vl, preferred_element_type=F32)
        if prefix:
            ec = jnp.exp(sc - m)
            den = den + jnp.sum(ec, axis=-1, keepdims=True)
            o = o + jnp.dot(ec.astype(BF16), vc_ref[ks], preferred_element_type=F32)
        if is_a:
            den = den + jnp.exp(sink - m)
        return o * (1.0 / den)

    if is_a:
        head_of_row = lax.broadcasted_iota(I32, (group * tq, 1), 0) // tq
        for g in range(2):
            q = jnp.concatenate([q_ref[group * g + j] for j in range(group)], axis=0)
            sink = jnp.zeros((group * tq, 1), F32)
            for j in range(group):
                sink = jnp.where(head_of_row == j, sink_ref[layer, group * g + j], sink)
            o = attend(q, 0, None, sink)
            for pp in range(2):
                o0 = o[(2 * pp) * tq:(2 * pp + 1) * tq]
                o1 = o[(2 * pp + 1) * tq:(2 * pp + 2) * tq]
                if g == 0:
                    o1 = pltpu.roll(o1, HEAD_DIM, 1)
                else:
                    o0 = pltpu.roll(o0, HEAD_DIM, 1)
                o_ref[2 * g + pp] = jnp.where(upper, o1, o0).astype(o_ref.dtype)
    else:
        def pair(p):
            o0 = attend(q_ref[2 * p], p, 2 * p, None)
            o1 = attend(q_ref[2 * p + 1], p, 2 * p + 1, None)
            o_ref[p] = jnp.where(upper, o1, o0).astype(o_ref.dtype)

        if kind == "b_lat":
            def body(p, c):
                pair(p)
                return c
            lax.fori_loop(0, 4, body, 0)
        else:
            for p in range(4):
                pair(p)


def _attention(kind, q, k, v, n_seq, *, prefix=None, sink=None, tp=None, layer=0):
    t = q.shape[1]
    nb = t // n_seq
    s_k = k.shape[0]
    tq, kwin = {"a_ctx": (n_seq, n_seq), "b_ctx": (n_seq, n_seq),
                "a_lat": (A_WINDOW, 3 * A_WINDOW), "b_lat": (256, 768)}[kind]
    nq = n_seq // tq
    in_specs = [pl.BlockSpec((8, tq, LANES), lambda b, i: (0, b * nq + i, 0)),
                pl.BlockSpec((s_k, n_seq, LANES), lambda b, i: (0, b, 0)),
                pl.BlockSpec((s_k, n_seq, LANES), lambda b, i: (0, b, 0))]
    args = [q, k, v]
    if prefix is not None:
        kc, vc = prefix
        spec = pl.BlockSpec((None,) + kc.shape[1:], lambda b, i: (b, 0, 0, 0))
        in_specs += [spec, spec]
        args += [kc, vc]
    if sink is not None:
        in_specs.append(pl.BlockSpec(memory_space=pltpu.SMEM))
        args.append(sink)
    if tp is not None:
        in_specs.append(_resident(tp.shape, lambda b, i: (0, 0, 0, 0)))
        args.append(tp)
    return pl.pallas_call(
        functools.partial(_attn_body, kind=kind, tq=tq, kwin=kwin, n_seq=n_seq, layer=layer),
        grid=(nb, nq), in_specs=in_specs,
        out_specs=pl.BlockSpec((4, tq, LANES), lambda b, i: (0, b * nq + i, 0)),
        out_shape=_sds((4, t, LANES), BF16),
        compiler_params=_params("arbitrary", "arbitrary"), name="attn_" + kind,
    )(*args)


def _gla_body(*refs, n_seq, init, emit_state, n_prev):
    it = iter(refs)
    q_ref, k_ref, v_ref, g_ref, laf_ref, lab_ref, gn_ref = (next(it) for _ in range(7))
    s0f_ref = s0b_ref = sf_ref = sb_ref = pf_ref = pb_ref = None
    if init:
        s0f_ref, s0b_ref = next(it), next(it)
    if n_prev:
        pf_ref, pb_ref = next(it), next(it)
    o_ref = next(it)
    if emit_state:
        sf_ref, sb_ref = next(it), next(it)
    acc_ref, qdf_ref, qdb_ref, dsf_ref, dsb_ref, decf_ref, decb_ref = (next(it) for _ in range(7))
    nc = n_seq // C_CHUNK
    per = GLA_BLOCK // C_CHUNK
    row = lax.broadcasted_iota(I32, (GLA_BLOCK, C_DK), 0) % C_CHUNK
    ri = lax.broadcasted_iota(I32, (GLA_BLOCK, GLA_BLOCK), 0)
    ci = lax.broadcasted_iota(I32, (GLA_BLOCK, GLA_BLOCK), 1)
    same = (ri // C_CHUNK) == (ci // C_CHUNK)
    zero_chunk = jnp.zeros((C_CHUNK, C_DK), BF16)

    def decay_sums(la, fwd):
        x = la
        for s in (1, 2, 4, 8, 16, 32):
            if fwd:
                x = x + jnp.where(row >= s, pltpu.roll(x, s, 0), 0.0)
            else:
                x = x + jnp.where(row < C_CHUNK - s, pltpu.roll(x, GLA_BLOCK - s, 0), 0.0)
        return x

    def block(u, carry):
        sl = pl.ds(pl.multiple_of(u * GLA_BLOCK, GLA_BLOCK), GLA_BLOCK)
        q = q_ref[sl, :].astype(F32)
        k = k_ref[sl, :].astype(F32)
        v = v_ref[sl, :]
        att = None
        rhs = []
        for fwd, la_ref, qd_ref, dec_ref in ((True, laf_ref, qdf_ref, decf_ref), (False, lab_ref, qdb_ref, decb_ref)):
            b = decay_sums(la_ref[sl, :], fwd)
            b3 = b.reshape(per, C_CHUNK, C_DK)
            bl3 = b3[:, C_CHUNK - 1:C_CHUNK, :] if fwd else b3[:, 0:1, :]
            qd = (q * jnp.exp(b)).astype(BF16)
            ki = (k * jnp.exp(-b)).astype(BF16)
            ke3 = (k.reshape(per, C_CHUNK, C_DK) * jnp.exp(bl3 - b3)).astype(BF16)
            qd_ref[sl, :] = qd
            dec = jnp.exp(bl3)
            for j in range(per):
                dec_ref[u * per + j, 0:1, :] = dec[j]
            a = lax.dot_general(qd, ki, _TRANS_B, preferred_element_type=F32)
            a = jnp.where(same & ((ci <= ri) if fwd else (ci >= ri)), a, 0.0)
            att = a if att is None else att + a
            rhs += [jnp.concatenate([ke3[i] if i == j else zero_chunk for i in range(per)], axis=0)
                    for j in range(per)]
        acc_ref[sl, :] = jnp.dot(att.astype(BF16), v, preferred_element_type=F32)
        ds = lax.dot_general(v, jnp.concatenate(rhs, axis=1), _TRANS_A, preferred_element_type=F32)
        for j in range(per):
            dsf_ref[u * per + j] = ds[:, C_DK * j:C_DK * (j + 1)]
            dsb_ref[u * per + j] = ds[:, C_DK * (per + j):C_DK * (per + j + 1)]
        return carry

    lax.fori_loop(0, n_seq // GLA_BLOCK, block, 0)

    def states(ds_ref, dec_ref, s0_ref, s_out_ref, prev_ref, fwd):
        def chunk(j, st):
            c = j if fwd else nc - 1 - j
            inc = ds_ref[c]
            ds_ref[c] = st
            return st * dec_ref[c, 0:1, :] + inc

        st = lax.fori_loop(0, nc, chunk, s0_ref[...].T if init else jnp.zeros((C_DV, C_DK), F32), unroll=4)
        if emit_state:
            if n_prev:
                s_out_ref[:n_prev] = prev_ref[...]
            s_out_ref[n_prev] = st.T

    states(dsf_ref, decf_ref, s0f_ref, sf_ref, pf_ref, True)
    states(dsb_ref, decb_ref, s0b_ref, sb_ref, pb_ref, False)

    def finish(u, carry):
        sl = pl.ds(pl.multiple_of(u * GLA_BLOCK, GLA_BLOCK), GLA_BLOCK)
        inter = []
        for j in range(per):
            c = u * per + j
            rows = pl.ds(pl.multiple_of(c * C_CHUNK, C_CHUNK), C_CHUNK)
            inter.append(
                lax.dot_general(qdf_ref[rows, :], dsf_ref[c].astype(BF16), _TRANS_B, preferred_element_type=F32)
                + lax.dot_general(qdb_ref[rows, :], dsb_ref[c].astype(BF16), _TRANS_B, preferred_element_type=F32))
        tot = acc_ref[sl, :] + jnp.concatenate(inter, axis=0)
        y = tot * lax.rsqrt(jnp.mean(tot * tot, axis=-1, keepdims=True) + EPS) * gn_ref[...]
        g = g_ref[sl, :].astype(F32)
        o_ref[sl, :] = (y * (g * (1.0 / (1.0 + jnp.exp(-g))))).astype(o_ref.dtype)
        return carry

    lax.fori_loop(0, n_seq // GLA_BLOCK, finish, 0)


def _gla(cq, ck, cv, cg, la, gn, n_seq, layer, *, s0=None, emit_state, prev=None):
    t = cq.shape[0]
    nb = t // n_seq
    nc = n_seq // C_CHUNK
    tok = lambda b, h: (b, h)
    in_specs = [pl.BlockSpec((n_seq, C_DK), tok), pl.BlockSpec((n_seq, C_DK), tok),
                pl.BlockSpec((n_seq, C_DV), tok), pl.BlockSpec((n_seq, C_DV), tok),
                pl.BlockSpec((n_seq, C_DK), tok), pl.BlockSpec((n_seq, C_DK), lambda b, h: (b, C_HEADS + h)),
                pl.BlockSpec((None, 1, C_DV), lambda b, h: (layer, 0, 0))]
    args = [cq, ck, cv, cg, la, la, gn]
    if s0 is not None:
        spec = pl.BlockSpec((None, None, None, C_DK, C_DV), lambda b, h: (b, layer, h, 0, 0))
        in_specs += [spec, spec]
        args += list(s0)
    n_prev = 0 if prev is None else prev[0].shape[1]
    if n_prev:
        spec = pl.BlockSpec((None, n_prev, None, C_DK, C_DV), lambda b, h: (b, 0, h, 0, 0))
        in_specs += [spec, spec]
        args += list(prev)
    out_specs = [pl.BlockSpec((n_seq, C_DV), tok)]
    out_shape = [_sds((t, C_HEADS * C_DV), BF16)]
    if emit_state:
        spec = pl.BlockSpec((None, n_prev + 1, None, C_DK, C_DV), lambda b, h: (b, 0, h, 0, 0))
        out_specs += [spec, spec]
        out_shape += [_sds((nb, n_prev + 1, C_HEADS, C_DK, C_DV))] * 2
    return pl.pallas_call(
        functools.partial(_gla_body, n_seq=n_seq, init=s0 is not None, emit_state=emit_state, n_prev=n_prev),
        grid=(nb, C_HEADS), in_specs=in_specs, out_specs=out_specs, out_shape=out_shape,
        scratch_shapes=[pltpu.VMEM((n_seq, C_DV), F32),
                        pltpu.VMEM((n_seq, C_DK), BF16), pltpu.VMEM((n_seq, C_DK), BF16),
                        pltpu.VMEM((nc, C_DV, C_DK), F32), pltpu.VMEM((nc, C_DV, C_DK), F32),
                        pltpu.VMEM((nc, 8, C_DK), F32), pltpu.VMEM((nc, 8, C_DK), F32)],
        compiler_params=_params("arbitrary", "arbitrary"), name="gla",
    )(*args)


def _outproj_body(oa_ref, ob_ref, oc_ref, x_ref, g1_ref, sh_ref, sc_ref, n2_ref, w_ref, wr_ref,
                  x1_ref, h2_ref, r_ref):
    mix = jnp.concatenate([oa_ref[p] for p in range(4)] + [ob_ref[p] for p in range(4)] + [oc_ref[...]], axis=1)
    y = jnp.dot(mix, w_ref[...], preferred_element_type=F32)
    x1 = x_ref[...] + g1_ref[...] * y
    x1_ref[...] = x1
    h = x1 * lax.rsqrt(jnp.mean(x1 * x1, axis=-1, keepdims=True) + EPS) * n2_ref[...]
    h = h * (1.0 + sc_ref[...]) + sh_ref[...]
    h2_ref[...] = h
    lg = jnp.dot(h.astype(BF16), wr_ref[...], preferred_element_type=F32)
    lane = lax.broadcasted_iota(I32, lg.shape, 1)
    big = jnp.int32(LANES)
    is_g = lane < N_GROUPS
    gmax = jnp.max(jnp.where(is_g, lg, -jnp.inf), axis=-1, keepdims=True)
    gidx = jnp.min(jnp.where(is_g & (lg == gmax), lane, big), axis=-1, keepdims=True)
    gw = 1.0 / jnp.sum(jnp.where(is_g, jnp.exp(lg - gmax), 0.0), axis=-1, keepdims=True)
    lo = N_GROUPS + E_PER_GROUP * gidx
    sel = (lane >= lo) & (lane < lo + E_PER_GROUP)
    v1 = jnp.max(jnp.where(sel, lg, -jnp.inf), axis=-1, keepdims=True)
    i1 = jnp.min(jnp.where(sel & (lg == v1), lane, big), axis=-1, keepdims=True)
    sel2 = sel & (lane != i1)
    v2 = jnp.max(jnp.where(sel2, lg, -jnp.inf), axis=-1, keepdims=True)
    i2 = jnp.min(jnp.where(sel2 & (lg == v2), lane, big), axis=-1, keepdims=True)
    e21 = jnp.exp(v2 - v1)
    w1 = gw / (1.0 + e21)
    w2 = gw * e21 / (1.0 + e21)
    out = jnp.where(lane == 0, (i1 - N_GROUPS).astype(F32),
                    jnp.where(lane == 1, (i2 - N_GROUPS).astype(F32),
                              jnp.where(lane == 2, w1, jnp.where(lane == 3, w2, 0.0))))
    r_ref[...] = out


def _outproj(oa, ob, oc, x, mods, mod_row_of_tile, n2, w_out_b, wr_b, layer):
    t, d = x.shape
    tm = TM_PROJ
    tok = lambda i: (i, 0)
    lay = lambda i: (layer, 0, 0)
    mod = lambda k: pl.BlockSpec((None, 1, d), lambda i: (mod_row_of_tile(i), 0, k))
    slab4 = pl.BlockSpec((4, tm, LANES), lambda i: (0, i, 0))
    return pl.pallas_call(
        _outproj_body,
        grid=(t // tm,),
        in_specs=[slab4, slab4, pl.BlockSpec((tm, 1024), tok), pl.BlockSpec((tm, d), tok),
                  mod(2), mod(3), mod(4), pl.BlockSpec((None, 1, d), lay),
                  _resident((None, d, d), lay), _resident((None, d, LANES), lay)],
        out_specs=[pl.BlockSpec((tm, d), tok), pl.BlockSpec((tm, d), tok), pl.BlockSpec((tm, LANES), tok)],
        out_shape=[_sds((t, d)), _sds((t, d)), _sds((t, LANES))],
        compiler_params=_params("arbitrary"), name="outproj",
    )(oa, ob, oc, x, mods, mods, mods, n2, w_out_b, wr_b)


def _moe_body(tile_ref, exp_ref, lo_ref, hi_ref, src0_ref, srcn_ref, dstp_ref, dstc_ref,
              h_ref, w1_ref, w3_ref, w2_ref, y_ref, xbuf, ybuf, gsem, ssem, *, n_steps, n_pairs):
    tm = xbuf.shape[1]
    s = pl.program_id(0)
    slot = s % 2
    other = 1 - slot

    def gather(idx_ref, b):
        for j in range(tm):
            pltpu.make_async_copy(h_ref.at[pl.ds(idx_ref[0, 0, j], 1)], xbuf.at[b, pl.ds(j, 1)], gsem.at[b]).start()

    def gathered(b):
        return pltpu.make_async_copy(h_ref.at[pl.ds(0, tm)], xbuf.at[b], gsem.at[b])

    def scatter(idx_ref, b, lo, hi):
        for j in range(tm):
            row = jnp.where((lo <= j) & (j < hi), idx_ref[0, 0, j], n_pairs + b * tm + j)
            pltpu.make_async_copy(ybuf.at[b, pl.ds(j, 1)], y_ref.at[pl.ds(row, 1)], ssem.at[b]).start()

    def scattered(b):
        return pltpu.make_async_copy(ybuf.at[b], y_ref.at[pl.ds(0, tm)], ssem.at[b])

    @pl.when(s == 0)
    def _():
        ybuf[...] = jnp.zeros(ybuf.shape, F32)
        scatter(dstc_ref, 0, 0, 0)
        gather(src0_ref, 0)

    gathered(slot).wait()
    gather(srcn_ref, other)
    prev = jnp.maximum(s - 1, 0)
    scatter(dstp_ref, other, jnp.where(s > 0, lo_ref[prev], 0), jnp.where(s > 0, hi_ref[prev], 0))
    x = xbuf[slot].astype(BF16)
    h1 = jnp.dot(x, w1_ref[...], preferred_element_type=F32)
    h3 = jnp.dot(x, w3_ref[...], preferred_element_type=F32)
    hid = (h1 * (1.0 / (1.0 + jnp.exp(-h1))) * h3).astype(BF16)
    y = jnp.dot(hid, w2_ref[...], preferred_element_type=F32)
    scattered(slot).wait()
    ybuf[slot] = y

    @pl.when(s == n_steps - 1)
    def _():
        scatter(dstc_ref, slot, lo_ref[s], hi_ref[s])
        scattered(other).wait()
        scattered(slot).wait()
        gathered(other).wait()


def _small_take(table, idx):
    n = table.shape[0]
    hit = idx[:, None] == jnp.arange(n, dtype=I32)[None, :]
    return jnp.sum(jnp.where(hit, table[None, :], 0), axis=1)


def _moe(h2, route, w1b, w3b, w2b, layer):
    t, d = h2.shape
    tm = TM_MOE
    n_pairs = 2 * t
    n_tiles = n_pairs // tm
    n_steps = n_tiles + N_EXPERTS - 1
    ef = route[:, :2].astype(I32).T.reshape(-1)
    order = jnp.argsort(ef, stable=True).astype(I32)
    experts = jnp.arange(N_EXPERTS, dtype=I32)
    counts = jnp.sum((ef[:, None] == experts[None, :]).astype(I32), axis=0)
    cend = jnp.cumsum(counts)
    cstart = cend - counts
    first_row = jnp.arange(n_tiles, dtype=I32) * tm
    e_first = jnp.sum((cend[None, :] <= first_row[:, None]).astype(I32), axis=1)
    e_last = jnp.sum((cend[None, :] <= (first_row + tm - 1)[:, None]).astype(I32), axis=1)
    n_sub = e_last - e_first + 1
    base = jnp.cumsum(n_sub) - n_sub
    step = jnp.arange(n_steps, dtype=I32)
    tile_of = jnp.clip(jnp.sum((base[None, :] <= step[:, None]).astype(I32), axis=1) - 1, 0, n_tiles - 1)
    exp_of = jnp.clip(_small_take(e_first, tile_of) + step - _small_take(base, tile_of), 0, N_EXPERTS - 1)
    used = step < jnp.sum(n_sub)
    row0 = tile_of * tm
    lo = jnp.where(used, jnp.clip(_small_take(cstart, exp_of) - row0, 0, tm), 0)
    hi = jnp.where(used, jnp.clip(_small_take(cend, exp_of) - row0, 0, tm), 0)
    src = (order % t).reshape(n_tiles, 1, tm)
    dst = order.reshape(n_tiles, 1, tm)

    def smem(f):
        return pl.BlockSpec((1, 1, tm), lambda s, tl, ex, lo_, hi_: (tl[f(s)], 0, 0), memory_space=pltpu.SMEM)

    wspec = lambda shape: pl.BlockSpec((None, None) + shape, lambda s, tl, ex, lo_, hi_: (layer, ex[s], 0, 0))
    grid_spec = pltpu.PrefetchScalarGridSpec(
        num_scalar_prefetch=4, grid=(n_steps,),
        in_specs=[smem(lambda s: s), smem(lambda s: jnp.minimum(s + 1, n_steps - 1)),
                  smem(lambda s: jnp.maximum(s - 1, 0)), smem(lambda s: s),
                  pl.BlockSpec(memory_space=pl.ANY),
                  wspec((d, D_EXPERT)), wspec((d, D_EXPERT)), wspec((D_EXPERT, d))],
        out_specs=pl.BlockSpec(memory_space=pl.ANY),
        scratch_shapes=[pltpu.VMEM((2, tm, d), F32), pltpu.VMEM((2, tm, d), F32),
                        pltpu.SemaphoreType.DMA((2,)), pltpu.SemaphoreType.DMA((2,))])
    return pl.pallas_call(
        functools.partial(_moe_body, n_steps=n_steps, n_pairs=n_pairs),
        grid_spec=grid_spec, out_shape=_sds((n_pairs + 2 * tm, d)),
        compiler_params=_params("arbitrary"), name="moe",
    )(tile_of, exp_of, lo, hi, src, src, dst, dst, h2, w1b, w3b, w2b)


def _combine_body(x1_ref, y0_ref, y1_ref, r_ref, g2_ref, o_ref):
    r = r_ref[...]
    lane = lax.broadcasted_iota(I32, r.shape, 1)
    w1 = jnp.sum(jnp.where(lane == 2, r, 0.0), axis=-1, keepdims=True)
    w2 = jnp.sum(jnp.where(lane == 3, r, 0.0), axis=-1, keepdims=True)
    o_ref[...] = x1_ref[...] + g2_ref[...] * (w1 * y0_ref[...] + w2 * y1_ref[...])


def _combine(x1, yg, route, mods, mod_row_of_tile):
    t, d = x1.shape
    tm = TM_PROJ
    nt = t // tm
    tok = lambda i: (i, 0)
    return pl.pallas_call(
        _combine_body,
        grid=(nt,),
        in_specs=[pl.BlockSpec((tm, d), tok), pl.BlockSpec((tm, d), tok),
                  pl.BlockSpec((tm, d), lambda i: (nt + i, 0)), pl.BlockSpec((tm, LANES), tok),
                  pl.BlockSpec((None, 1, d), lambda i: (mod_row_of_tile(i), 0, 5))],
        out_specs=pl.BlockSpec((tm, d), tok), out_shape=_sds((t, d)),
        compiler_params=_params("arbitrary"), name="combine",
    )(x1, yg, yg, route, mods)


def _rope_tables():
    t = jnp.arange(GRID_W * GRID_ROWS)
    pairs = HEAD_DIM // 4
    inv = 1.0 / (ROPE_THETA ** (jnp.arange(pairs, dtype=F32) * 2.0 / (HEAD_DIM // 2)))
    ang_r = (t // GRID_W).astype(F32)[:, None] * inv
    ang_c = (t % GRID_W).astype(F32)[:, None] * inv
    cos = jnp.concatenate([jnp.cos(ang_r)] * 2 + [jnp.cos(ang_c)] * 2, axis=1)
    sin = jnp.concatenate([-jnp.sin(ang_r), jnp.sin(ang_r), -jnp.sin(ang_c), jnp.sin(ang_c)], axis=1)
    return jnp.tile(cos, (1, 2)), jnp.tile(sin, (1, 2))


def _bias_pair_tiles(rpb_l):
    qc = jnp.arange(GRID_W)[:, None]
    kc = jnp.arange(GRID_W)[None, :]
    start_c = jnp.clip(qc - NA_COLS // 2, 0, GRID_W - NA_COLS)
    col_ok = (kc >= start_c) & (kc < start_c + NA_COLS)
    dc = jnp.clip(kc - qc + (NA_COLS - 1), 0, 2 * NA_COLS - 2)
    d = jnp.arange(-8, 23)
    rows = jnp.stack([rpb_l[:, min(max(dd, 0), 2 * NA_ROWS - 2)] for dd in range(-8, 23)], axis=1)
    onehot = (dc[None] == jnp.arange(2 * NA_COLS - 1)[:, None, None]).astype(F32)
    tile = jnp.einsum("hdx,xqk->hdqk", rows, onehot, precision=lax.Precision.HIGHEST)
    ok = col_ok[None, None] & ((d >= 0) & (d <= 2 * NA_ROWS - 2))[None, :, None, None]
    tile = jnp.where(ok, tile, NEG_INF).astype(F32)
    return jnp.concatenate([tile[:, :-1], tile[:, 1:]], axis=-1)


def kernel(x_prompt, x_sample, c, cache_a_k, cache_a_v, cache_b_k, cache_b_v, state_c_fwd, state_c_bwd, c_ctx, norm1_g, norm2_g, w_mod, b_mod, w_in, q_norm_a, k_norm_a, sink_a, q_norm_b, k_norm_b, rpb_b, w_gk_up_f, b_gk_f, w_gk_up_b, b_gk_b, gla_norm_g, w_out, w_group, w_router, w1, w3, w2):
    depth = w_in.shape[0]
    nb_c, l_c, d = x_prompt.shape
    nb_s, n_s, _ = x_sample.shape
    past = cache_a_k.shape[2]
    tm = TM_PROJ

    w_in_b = jnp.pad(w_in, ((0, 0), (0, 0), (0, N_IN_PAD - N_IN))).astype(BF16)
    w_out_b = w_out.astype(BF16)
    wr_b = jnp.pad(jnp.concatenate([w_group, w_router], axis=-1),
                   ((0, 0), (0, 0), (0, LANES - N_GROUPS - N_EXPERTS))).astype(BF16)
    w1b, w3b, w2b = w1.astype(BF16), w3.astype(BF16), w2.astype(BF16)
    wgk = jnp.zeros((depth, 256, 1024), F32)
    wgk = wgk.at[:, :C_RANK, :512].set(w_gk_up_f).at[:, C_RANK:2 * C_RANK, 512:].set(w_gk_up_b).astype(BF16)
    cos, sin = _rope_tables()
    head_id = jnp.arange(512) // HEAD_DIM
    consts = dict(
        bd=(head_id[:, None] == head_id[None, :]).astype(BF16),
        gqa=jnp.tile(q_norm_a, (1, 8)).reshape(depth, 1, 512),
        gka=jnp.tile(k_norm_a, (1, 2)).reshape(depth, 1, LANES),
        gqb=jnp.tile(q_norm_b, (1, 8)).reshape(depth, 1, 512),
        gkb=jnp.tile(k_norm_b, (1, 8)).reshape(depth, 1, 512),
        cos=cos, sin=sin, wgk=wgk,
        bgk=jnp.concatenate([b_gk_f, b_gk_b], axis=-1).reshape(depth, 1, 1024))
    n1 = norm1_g.reshape(depth, 1, d)
    n2 = norm2_g.reshape(depth, 1, d)
    gn = gla_norm_g.reshape(depth, 1, C_DV)
    cond = jnp.zeros((16, d), F32).at[0].set(c_ctx).at[1:1 + nb_s].set(c)
    ctx_row = lambda i: 0
    lat_row = lambda i: 1 + i // (n_s // tm)
    cak = cache_a_k.reshape(nb_s, depth, 1, past, LANES).astype(BF16)
    cav = cache_a_v.reshape(nb_s, depth, 1, past, LANES).astype(BF16)
    cbk = cache_b_k.reshape(nb_s, depth, past, 4, LANES).transpose(0, 1, 3, 2, 4).astype(BF16)
    cbv = cache_b_v.reshape(nb_s, depth, past, 4, LANES).transpose(0, 1, 3, 2, 4).astype(BF16)

    xp = x_prompt.reshape(nb_c * l_c, d)
    xs = x_sample.reshape(nb_s * n_s, d)
    kv_ctx = st_ctx = None
    for l in range(depth):
        mods = _adaln(cond, w_mod, b_mod, l).reshape(16, 1, 6 * d)
        tp = _bias_pair_tiles(rpb_b[l])

        aq, ak, av, bq, bk, bv, cq, ck, cv, cg, la, *kv_ctx = _inproj(
            xp, mods, ctx_row, n1, w_in_b, consts, l, rope=False, tm=TM_PROJ // 2, seq_len=l_c, prev=kv_ctx)
        oa = _attention("a_ctx", aq, ak.reshape(1, -1, LANES), av.reshape(1, -1, LANES), l_c, sink=sink_a, layer=l)
        ob = _attention("b_ctx", bq, bk, bv, l_c)
        oc, *st_ctx = _gla(cq, ck, cv, cg, la, gn, l_c, l, emit_state=True, prev=st_ctx)
        x1, h2, route = _outproj(oa, ob, oc, xp, mods, ctx_row, n2, w_out_b, wr_b, l)
        xp = _combine(x1, _moe(h2, route, w1b, w3b, w2b, l), route, mods, ctx_row)

        aq, ak, av, bq, bk, bv, cq, ck, cv, cg, la = _inproj(
            xs, mods, lat_row, n1, w_in_b, consts, l, rope=True, tm=TM_PROJ)
        oa = _attention("a_lat", aq, ak.reshape(1, -1, LANES), av.reshape(1, -1, LANES), n_s,
                        prefix=(cak[:, l], cav[:, l]), sink=sink_a, layer=l)
        ob = _attention("b_lat", bq, bk, bv, n_s, prefix=(cbk[:, l], cbv[:, l]), tp=tp)
        oc, = _gla(cq, ck, cv, cg, la, gn, n_s, l, s0=(state_c_fwd, state_c_bwd), emit_state=False)
        x1, h2, route = _outproj(oa, ob, oc, xs, mods, lat_row, n2, w_out_b, wr_b, l)
        xs = _combine(x1, _moe(h2, route, w1b, w3b, w2b, l), route, mods, lat_row)

    new_kv = [a.reshape(nb_c, depth, l_c, -1, HEAD_DIM) for a in kv_ctx]
    return (xp.reshape(nb_c, l_c, d), xs.reshape(nb_s, n_s, d), *new_kv, *st_ctx)
```

```python
import functools

import jax
import jax.numpy as jnp
from jax import lax
from jax.experimental import pallas as pl
from jax.experimental.pallas import tpu as pltpu

F32 = jnp.float32
BF16 = jnp.bfloat16
I32 = jnp.int32

D_MODEL = 2048
HEAD_DIM = 64
EPS = 1e-6
NEG_INF = -1e30
ROPE_THETA = 10000.0
GRID_W = 64
GRID_ROWS = 32
A_WINDOW = 128
NA_ROWS = 8
NA_COLS = 16
C_HEADS = 4
C_DK = 128
C_DV = 256
C_RANK = 16
C_CHUNK = 64
GLA_BLOCK = 256
N_GROUPS = 4
E_PER_GROUP = 4
N_EXPERTS = 16
D_EXPERT = 512
LANES = 128
N_IN = 5408
N_IN_PAD = 5632
COL_AQ, COL_AKV, COL_BQ, COL_BK, COL_BV = 0, 512, 768, 1280, 1792
COL_CQ, COL_CK, COL_CV, COL_CG, COL_CR = 2304, 2816, 3328, 4352, 5376
VMEM_LIMIT = 56 * 1024 * 1024
TM_PROJ = 512
TM_MOE = 256
TM_COMBINE = 256

_TRANS_B = (((1,), (1,)), ((), ()))
_TRANS_A = (((0,), (0,)), ((), ()))


def _sds(shape, dtype=F32):
    return jax.ShapeDtypeStruct(shape, dtype)


def _params(*sem):
    return pltpu.CompilerParams(dimension_semantics=sem, vmem_limit_bytes=VMEM_LIMIT)


def _resident(shape, index_map):
    return pl.BlockSpec(shape, index_map, pipeline_mode=pl.Buffered(1))


def _adaln_body(c_ref, w_ref, b_ref, o_ref):
    c = c_ref[...]
    s = c * (1.0 / (1.0 + jnp.exp(-c)))
    o_ref[...] = jnp.dot(s.astype(BF16), w_ref[...].astype(BF16), preferred_element_type=F32) + b_ref[...]


def _adaln(cond, w_mod, b_mod, layer):
    d, n = w_mod.shape[1], w_mod.shape[2]
    tn = 1024
    return pl.pallas_call(
        _adaln_body,
        grid=(n // tn,),
        in_specs=[pl.BlockSpec((16, d), lambda j: (0, 0)),
                  pl.BlockSpec((None, d, tn), lambda j: (layer, 0, j)),
                  pl.BlockSpec((None, 1, tn), lambda j: (layer, 0, j))],
        out_specs=pl.BlockSpec((16, tn), lambda j: (0, j)),
        out_shape=_sds((16, n)),
        compiler_params=_params("arbitrary"),
        name="adaln",
    )(cond, w_mod, b_mod.reshape(b_mod.shape[0], 1, n))


def _head_norm(z, gain, bd):
    w = z.shape[1]
    ss = jnp.dot((z * z).astype(BF16), bd[:w, :w], preferred_element_type=F32)
    return z * lax.rsqrt(ss * (1.0 / HEAD_DIM) + EPS) * gain


def _rope(z, cos, sin_signed, lane):
    lower = (lane % 32) < 16
    partner = jnp.where(lower, pltpu.roll(z, LANES - 16, 1), pltpu.roll(z, 16, 1))
    return z * cos + partner * sin_signed


def _inproj_body(x_ref, sh_ref, sc_ref, n1_ref, w_ref, bd_ref, gqa_ref, gka_ref, gqb_ref, gkb_ref,
                 cos_ref, sin_ref, wgk_ref, bgk_ref,
                 *rest, rope, n_prev, seq_len):
    prev_refs = rest[:4] if n_prev else ()
    rest = rest[4 if n_prev else 0:]
    aq_ref, ak_ref, av_ref, bq_ref, bk_ref, bv_ref, cq_ref, ck_ref, cv_ref, cg_ref, la_ref = rest[:11]
    stack_refs = rest[11:]

    def emit(idx, val):
        if not seq_len:
            return
        ref = stack_refs[idx]
        if n_prev:
            ref[:, :n_prev] = prev_refs[idx][...]
        ref[:, n_prev] = val.reshape(val.shape[0] // seq_len, seq_len, val.shape[1])

    x = x_ref[...]
    h = x * lax.rsqrt(jnp.mean(x * x, axis=-1, keepdims=True) + EPS) * n1_ref[...]
    h = h * (1.0 + sc_ref[...]) + sh_ref[...]
    hb = h.astype(BF16)
    bd = bd_ref[...]
    lane = lax.broadcasted_iota(I32, (x.shape[0], LANES), 1)
    upper = lane >= HEAD_DIM

    def proj(c0, n):
        return jnp.dot(hb, w_ref[:, c0:c0 + n], preferred_element_type=F32)

    def rot(s):
        return _rope(s, cos_ref[...], sin_ref[...], lane) if rope else s

    z = _head_norm(proj(COL_AQ, 512), gqa_ref[...], bd)
    for p in range(4):
        s = rot(z[:, LANES * p:LANES * (p + 1)]) * (HEAD_DIM ** -0.5)
        r = pltpu.roll(s, HEAD_DIM, 1)
        if p // 2 == 0:
            e0, e1 = jnp.where(upper, 0.0, s), jnp.where(upper, 0.0, r)
        else:
            e0, e1 = jnp.where(upper, r, 0.0), jnp.where(upper, s, 0.0)
        aq_ref[2 * p] = e0.astype(BF16)
        aq_ref[2 * p + 1] = e1.astype(BF16)
    z = proj(COL_AKV, 256)
    k = rot(_head_norm(z[:, :LANES], gka_ref[...], bd))
    ak_ref[...] = k.astype(BF16)
    av_ref[...] = z[:, LANES:].astype(BF16)
    emit(0, k)
    emit(1, z[:, LANES:])
    z = _head_norm(proj(COL_BQ, 512), gqb_ref[...], bd) * (HEAD_DIM ** -0.5)
    for p in range(4):
        s = z[:, LANES * p:LANES * (p + 1)]
        bq_ref[2 * p] = jnp.where(upper, 0.0, s).astype(BF16)
        bq_ref[2 * p + 1] = jnp.where(upper, s, 0.0).astype(BF16)
    z = _head_norm(proj(COL_BK, 512), gkb_ref[...], bd)
    for p in range(4):
        bk_ref[p] = z[:, LANES * p:LANES * (p + 1)].astype(BF16)
    emit(2, z)
    z = proj(COL_BV, 512)
    for p in range(4):
        bv_ref[p] = z[:, LANES * p:LANES * (p + 1)].astype(BF16)
    emit(3, z)
    cq_ref[...] = (proj(COL_CQ, 512) * (C_DK ** -0.5)).astype(BF16)
    ck_ref[...] = proj(COL_CK, 512).astype(BF16)
    cv_ref[...] = proj(COL_CV, 1024).astype(BF16)
    cg_ref[...] = proj(COL_CG, 1024).astype(BF16)
    pre = jnp.dot(proj(COL_CR, 256).astype(BF16), wgk_ref[...], preferred_element_type=F32) + bgk_ref[...]
    la_ref[...] = (jnp.minimum(pre, 0.0) - jnp.log(1.0 + jnp.exp(-jnp.abs(pre)))) * (1.0 / 16.0)


def _inproj(x, mods, mod_row_of_tile, n1, w_in_b, consts, layer, *, rope, tm, seq_len=0, prev=None):
    t, d = x.shape
    n_prev = 0 if prev is None else prev[0].shape[1]
    seq_tiles = (GRID_W * GRID_ROWS) // tm
    tok = lambda i: (i, 0)
    full = lambda i: (0, 0)
    lay = lambda i: (layer, 0, 0)
    in_specs = [
        pl.BlockSpec((tm, d), tok),
        pl.BlockSpec((None, 1, d), lambda i: (mod_row_of_tile(i, tm), 0, 0)),
        pl.BlockSpec((None, 1, d), lambda i: (mod_row_of_tile(i, tm), 0, 1)),
        pl.BlockSpec((None, 1, d), lay),
        _resident((None, d, N_IN_PAD), lay),
        _resident((512, 512), full),
        pl.BlockSpec((None, 1, 512), lay), pl.BlockSpec((None, 1, LANES), lay),
        pl.BlockSpec((None, 1, 512), lay), pl.BlockSpec((None, 1, 512), lay),
        pl.BlockSpec((tm, LANES), lambda i: (i % seq_tiles, 0)),
        pl.BlockSpec((tm, LANES), lambda i: (i % seq_tiles, 0)),
        _resident((None, 256, 1024), lay),
        pl.BlockSpec((None, 1, 1024), lay),
    ]
    slab8 = pl.BlockSpec((8, tm, LANES), lambda i: (0, i, 0))
    slab4 = pl.BlockSpec((4, tm, LANES), lambda i: (0, i, 0))
    out_specs = [slab8, pl.BlockSpec((tm, LANES), tok), pl.BlockSpec((tm, LANES), tok),
                 slab8, slab4, slab4,
                 pl.BlockSpec((tm, 512), tok), pl.BlockSpec((tm, 512), tok),
                 pl.BlockSpec((tm, 1024), tok), pl.BlockSpec((tm, 1024), tok), pl.BlockSpec((tm, 1024), tok)]
    out_shape = [_sds((8, t, LANES), BF16), _sds((t, LANES), BF16), _sds((t, LANES), BF16),
                 _sds((8, t, LANES), BF16), _sds((4, t, LANES), BF16), _sds((4, t, LANES), BF16),
                 _sds((t, 512), BF16), _sds((t, 512), BF16), _sds((t, 1024), BF16), _sds((t, 1024), BF16),
                 _sds((t, 1024), F32)]
    args = [x, mods, mods, n1, w_in_b, consts["bd"], consts["gqa"], consts["gka"], consts["gqb"], consts["gkb"],
            consts["cos"], consts["sin"], consts["wgk"], consts["bgk"]]
    if seq_len:
        nseq = tm // seq_len
        for j, width in enumerate((LANES, LANES, 512, 512)):
            if n_prev:
                in_specs.append(pl.BlockSpec((nseq, n_prev, seq_len, width), lambda i: (i, 0, 0, 0)))
                args.append(prev[j])
            out_specs.append(pl.BlockSpec((nseq, n_prev + 1, seq_len, width), lambda i: (i, 0, 0, 0)))
            out_shape.append(_sds((t // seq_len, n_prev + 1, seq_len, width)))
    return pl.pallas_call(
        functools.partial(_inproj_body, rope=rope, n_prev=n_prev, seq_len=seq_len),
        grid=(t // tm,), in_specs=in_specs, out_specs=out_specs, out_shape=out_shape,
        compiler_params=_params("arbitrary"), name="inproj",
    )(*args)


def _attn_body(*refs, kind, tq, kwin, n_seq, layer):
    is_a = kind in ("a_ctx", "a_lat")
    prefix = kind in ("a_lat", "b_lat")
    it = iter(refs)
    q_ref, k_ref, v_ref = next(it), next(it), next(it)
    kc_ref = vc_ref = sink_ref = tp_ref = None
    if prefix:
        kc_ref, vc_ref = next(it), next(it)
    if is_a:
        sink_ref = next(it)
    if kind == "b_lat":
        tp_ref = next(it)
    o_ref = next(it)
    i = pl.program_id(1)

    group = 4 if is_a else 1
    if kind == "a_lat":
        ws = jnp.clip(i * tq - A_WINDOW, 0, n_seq - kwin)
        ws = pl.multiple_of(ws, LANES)
        qpos = i * tq + lax.broadcasted_iota(I32, (group * tq, kwin), 0) % tq
        kpos = ws + lax.broadcasted_iota(I32, (group * tq, kwin), 1)
        allowed = jnp.abs(qpos - kpos) <= A_WINDOW
    elif kind == "b_lat":
        r0 = i * (tq // GRID_W)
        k0 = jnp.clip(r0 - NA_ROWS // 2, 0, GRID_ROWS - kwin // GRID_W)
        ws = pl.multiple_of(k0 * GRID_W, GRID_W)
        qrow = r0 + lax.broadcasted_iota(I32, (tq, kwin), 0) // GRID_W
        krow = k0 + lax.broadcasted_iota(I32, (tq, kwin), 1) // GRID_W
        start = jnp.clip(qrow - NA_ROWS // 2, 0, GRID_ROWS - NA_ROWS)
        allowed = (krow >= start) & (krow < start + NA_ROWS)
        row_mask = jnp.where(allowed, 0.0, NEG_INF)
    else:
        ws = 0
    lane = lax.broadcasted_iota(I32, (tq, LANES), 1)
    upper = lane >= HEAD_DIM

    def attend(q, ks, h, sink):
        kl = k_ref[ks, pl.ds(ws, kwin), :].astype(BF16)
        vl = v_ref[ks, pl.ds(ws, kwin), :].astype(BF16)
        s = lax.dot_general(q, kl, _TRANS_B, preferred_element_type=F32)
        if kind == "a_lat":
            s = jnp.where(allowed, s, NEG_INF)
        elif kind == "b_lat":
            rows = []
            for qr in range(tq // GRID_W):
                u0 = k0 - r0 - qr + 15
                rows.append(jnp.concatenate([tp_ref[h, u0 + 2 * m] for m in range(kwin // LANES)], axis=1))
            s = s + jnp.concatenate(rows, axis=0) + row_mask
        m = jnp.max(s, axis=-1, keepdims=True)
        if prefix:
            sc = lax.dot_general(q, kc_ref[ks], _TRANS_B, preferred_element_type=F32)
            m = jnp.maximum(m, jnp.max(sc, axis=-1, keepdims=True))
        if is_a:
            m = jnp.maximum(m, sink)
        e = jnp.exp(s - m)
        den = jnp.sum(e, axis=-1, keepdims=True)
        o = jnp.dot(e.astype(BF16), vl, preferred_element_type=F32)
        if prefix:
            ec = jnp.exp(sc - m)
            den = den + jnp.sum(ec, axis=-1, keepdims=True)
            o = o + jnp.dot(ec.astype(BF16), vc_ref[ks], preferred_element_type=F32)
        if is_a:
            den = den + jnp.exp(sink - m)
        return o * (1.0 / den)

    if is_a:
        head_of_row = lax.broadcasted_iota(I32, (group * tq, 1), 0) // tq
        for g in range(2):
            q = jnp.concatenate([q_ref[group * g + j] for j in range(group)], axis=0)
            sink = jnp.zeros((group * tq, 1), F32)
            for j in range(group):
                sink = jnp.where(head_of_row == j, sink_ref[layer, group * g + j], sink)
            o = attend(q, 0, None, sink)
            for pp in range(2):
                o0 = o[(2 * pp) * tq:(2 * pp + 1) * tq]
                o1 = o[(2 * pp + 1) * tq:(2 * pp + 2) * tq]
                if g == 0:
                    o1 = pltpu.roll(o1, HEAD_DIM, 1)
                else:
                    o0 = pltpu.roll(o0, HEAD_DIM, 1)
                o_ref[2 * g + pp] = jnp.where(upper, o1, o0).astype(o_ref.dtype)
    else:
        def pair(p):
            o0 = attend(q_ref[2 * p], p, 2 * p, None)
            o1 = attend(q_ref[2 * p + 1], p, 2 * p + 1, None)
            o_ref[p] = jnp.where(upper, o1, o0).astype(o_ref.dtype)

        if kind == "b_lat":
            def body(p, c):
                pair(p)
                return c
            lax.fori_loop(0, 4, body, 0)
        else:
            for p in range(4):
                pair(p)


def _attention(kind, q, k, v, n_seq, *, prefix=None, sink=None, tp=None, layer=0):
    t = q.shape[1]
    nb = t // n_seq
    s_k = k.shape[0]
    tq, kwin = {"a_ctx": (n_seq, n_seq), "b_ctx": (n_seq, n_seq),
                "a_lat": (A_WINDOW, 3 * A_WINDOW), "b_lat": (256, 768)}[kind]
    nq = n_seq // tq
    in_specs = [pl.BlockSpec((8, tq, LANES), lambda b, i: (0, b * nq + i, 0)),
                pl.BlockSpec((s_k, n_seq, LANES), lambda b, i: (0, b, 0)),
                pl.BlockSpec((s_k, n_seq, LANES), lambda b, i: (0, b, 0))]
    args = [q, k, v]
    if prefix is not None:
        kc, vc = prefix
        spec = pl.BlockSpec((None,) + kc.shape[1:], lambda b, i: (b, 0, 0, 0))
        in_specs += [spec, spec]
        args += [kc, vc]
    if sink is not None:
        in_specs.append(pl.BlockSpec(memory_space=pltpu.SMEM))
        args.append(sink)
    if tp is not None:
        in_specs.append(_resident(tp.shape, lambda b, i: (0, 0, 0, 0)))
        args.append(tp)
    return pl.pallas_call(
        functools.partial(_attn_body, kind=kind, tq=tq, kwin=kwin, n_seq=n_seq, layer=layer),
        grid=(nb, nq), in_specs=in_specs,
        out_specs=pl.BlockSpec((4, tq, LANES), lambda b, i: (0, b * nq + i, 0)),
        out_shape=_sds((4, t, LANES), BF16),
        compiler_params=_params("arbitrary", "arbitrary"), name="attn_" + kind,
    )(*args)


def _gla_body(*refs, n_seq, init, emit_state, n_prev):
    it = iter(refs)
    q_ref, k_ref, v_ref, g_ref, laf_ref, lab_ref, gn_ref = (next(it) for _ in range(7))
    s0f_ref = s0b_ref = sf_ref = sb_ref = pf_ref = pb_ref = None
    if init:
        s0f_ref, s0b_ref = next(it), next(it)
    if n_prev:
        pf_ref, pb_ref = next(it), next(it)
    o_ref = next(it)
    if emit_state:
        sf_ref, sb_ref = next(it), next(it)
    acc_ref, qdf_ref, qdb_ref, dsf_ref, dsb_ref, decf_ref, decb_ref = (next(it) for _ in range(7))
    nc = n_seq // C_CHUNK
    per = GLA_BLOCK // C_CHUNK
    row = lax.broadcasted_iota(I32, (GLA_BLOCK, C_DK), 0) % C_CHUNK
    ri = lax.broadcasted_iota(I32, (GLA_BLOCK, GLA_BLOCK), 0)
    ci = lax.broadcasted_iota(I32, (GLA_BLOCK, GLA_BLOCK), 1)
    same = (ri // C_CHUNK) == (ci // C_CHUNK)
    zero_chunk = jnp.zeros((C_CHUNK, C_DK), BF16)

    def decay_sums(la, fwd):
        x = la
        for s in (1, 2, 4, 8, 16, 32):
            if fwd:
                x = x + jnp.where(row >= s, pltpu.roll(x, s, 0), 0.0)
            else:
                x = x + jnp.where(row < C_CHUNK - s, pltpu.roll(x, GLA_BLOCK - s, 0), 0.0)
        return x

    def block(u, carry):
        sl = pl.ds(pl.multiple_of(u * GLA_BLOCK, GLA_BLOCK), GLA_BLOCK)
        q = q_ref[sl, :].astype(F32)
        k = k_ref[sl, :].astype(F32)
        v = v_ref[sl, :]
        att = None
        rhs = []
        for fwd, la_ref, qd_ref, dec_ref in ((True, laf_ref, qdf_ref, decf_ref), (False, lab_ref, qdb_ref, decb_ref)):
            b = decay_sums(la_ref[sl, :], fwd)
            b3 = b.reshape(per, C_CHUNK, C_DK)
            bl3 = b3[:, C_CHUNK - 1:C_CHUNK, :] if fwd else b3[:, 0:1, :]
            qd = (q * jnp.exp(b)).astype(BF16)
            ki = (k * jnp.exp(-b)).astype(BF16)
            ke3 = (k.reshape(per, C_CHUNK, C_DK) * jnp.exp(bl3 - b3)).astype(BF16)
            qd_ref[sl, :] = qd
            dec = jnp.exp(bl3)
            for j in range(per):
                dec_ref[u * per + j, 0:1, :] = dec[j]
            a = lax.dot_general(qd, ki, _TRANS_B, preferred_element_type=F32)
            a = jnp.where(same & ((ci <= ri) if fwd else (ci >= ri)), a, 0.0)
            att = a if att is None else att + a
            rhs += [jnp.concatenate([ke3[i] if i == j else zero_chunk for i in range(per)], axis=0)
                    for j in range(per)]
        acc_ref[sl, :] = jnp.dot(att.astype(BF16), v, preferred_element_type=F32)
        ds = lax.dot_general(v, jnp.concatenate(rhs, axis=1), _TRANS_A, preferred_element_type=F32)
        for j in range(per):
            dsf_ref[u * per + j] = ds[:, C_DK * j:C_DK * (j + 1)]
            dsb_ref[u * per + j] = ds[:, C_DK * (per + j):C_DK * (per + j + 1)]
        return carry

    lax.fori_loop(0, n_seq // GLA_BLOCK, block, 0)

    def states(ds_ref, dec_ref, s0_ref, s_out_ref, prev_ref, fwd):
        def chunk(j, st):
            c = j if fwd else nc - 1 - j
            inc = ds_ref[c]
            ds_ref[c] = st
            return st * dec_ref[c, 0:1, :] + inc

        st = lax.fori_loop(0, nc, chunk, s0_ref[...].T if init else jnp.zeros((C_DV, C_DK), F32), unroll=4)
        if emit_state:
            if n_prev:
                s_out_ref[:n_prev] = prev_ref[...]
            s_out_ref[n_prev] = st.T

    states(dsf_ref, decf_ref, s0f_ref, sf_ref, pf_ref, True)
    states(dsb_ref, decb_ref, s0b_ref, sb_ref, pb_ref, False)

    def finish(u, carry):
        sl = pl.ds(pl.multiple_of(u * GLA_BLOCK, GLA_BLOCK), GLA_BLOCK)
        inter = []
        for j in range(per):
            c = u * per + j
            rows = pl.ds(pl.multiple_of(c * C_CHUNK, C_CHUNK), C_CHUNK)
            inter.append(
                lax.dot_general(qdf_ref[rows, :], dsf_ref[c].astype(BF16), _TRANS_B, preferred_element_type=F32)
                + lax.dot_general(qdb_ref[rows, :], dsb_ref[c].astype(BF16), _TRANS_B, preferred_element_type=F32))
        tot = acc_ref[sl, :] + jnp.concatenate(inter, axis=0)
        y = tot * lax.rsqrt(jnp.mean(tot * tot, axis=-1, keepdims=True) + EPS) * gn_ref[...]
        g = g_ref[sl, :].astype(F32)
        o_ref[sl, :] = (y * (g * (1.0 / (1.0 + jnp.exp(-g))))).astype(o_ref.dtype)
        return carry

    lax.fori_loop(0, n_seq // GLA_BLOCK, finish, 0)


def _gla(cq, ck, cv, cg, la, gn, n_seq, layer, *, s0=None, emit_state, prev=None):
    t = cq.shape[0]
    nb = t // n_seq
    nc = n_seq // C_CHUNK
    tok = lambda b, h: (b, h)
    in_specs = [pl.BlockSpec((n_seq, C_DK), tok), pl.BlockSpec((n_seq, C_DK), tok),
                pl.BlockSpec((n_seq, C_DV), tok), pl.BlockSpec((n_seq, C_DV), tok),
                pl.BlockSpec((n_seq, C_DK), tok), pl.BlockSpec((n_seq, C_DK), lambda b, h: (b, C_HEADS + h)),
                pl.BlockSpec((None, 1, C_DV), lambda b, h: (layer, 0, 0))]
    args = [cq, ck, cv, cg, la, la, gn]
    if s0 is not None:
        spec = pl.BlockSpec((None, None, None, C_DK, C_DV), lambda b, h: (b, layer, h, 0, 0))
        in_specs += [spec, spec]
        args += list(s0)
    n_prev = 0 if prev is None else prev[0].shape[1]
    if n_prev:
        spec = pl.BlockSpec((None, n_prev, None, C_DK, C_DV), lambda b, h: (b, 0, h, 0, 0))
        in_specs += [spec, spec]
        args += list(prev)
    out_specs = [pl.BlockSpec((n_seq, C_DV), tok)]
    out_shape = [_sds((t, C_HEADS * C_DV), BF16)]
    if emit_state:
        spec = pl.BlockSpec((None, n_prev + 1, None, C_DK, C_DV), lambda b, h: (b, 0, h, 0, 0))
        out_specs += [spec, spec]
        out_shape += [_sds((nb, n_prev + 1, C_HEADS, C_DK, C_DV))] * 2
    return pl.pallas_call(
        functools.partial(_gla_body, n_seq=n_seq, init=s0 is not None, emit_state=emit_state, n_prev=n_prev),
        grid=(nb, C_HEADS), in_specs=in_specs, out_specs=out_specs, out_shape=out_shape,
        scratch_shapes=[pltpu.VMEM((n_seq, C_DV), F32),
                        pltpu.VMEM((n_seq, C_DK), BF16), pltpu.VMEM((n_seq, C_DK), BF16),
                        pltpu.VMEM((nc, C_DV, C_DK), F32), pltpu.VMEM((nc, C_DV, C_DK), F32),
                        pltpu.VMEM((nc, 8, C_DK), F32), pltpu.VMEM((nc, 8, C_DK), F32)],
        compiler_params=_params("arbitrary", "arbitrary"), name="gla",
    )(*args)


def _outproj_body(oa_ref, ob_ref, oc_ref, x_ref, g1_ref, sh_ref, sc_ref, n2_ref, w_ref, wr_ref,
                  x1_ref, h2_ref, r_ref):
    mix = jnp.concatenate([oa_ref[p] for p in range(4)] + [ob_ref[p] for p in range(4)] + [oc_ref[...]], axis=1)
    y = jnp.dot(mix, w_ref[...], preferred_element_type=F32)
    x1 = x_ref[...] + g1_ref[...] * y
    x1_ref[...] = x1
    h = x1 * lax.rsqrt(jnp.mean(x1 * x1, axis=-1, keepdims=True) + EPS) * n2_ref[...]
    h = h * (1.0 + sc_ref[...]) + sh_ref[...]
    h2_ref[...] = h
    lg = jnp.dot(h.astype(BF16), wr_ref[...], preferred_element_type=F32)
    lane = lax.broadcasted_iota(I32, lg.shape, 1)
    big = jnp.int32(LANES)
    is_g = lane < N_GROUPS
    gmax = jnp.max(jnp.where(is_g, lg, -jnp.inf), axis=-1, keepdims=True)
    gidx = jnp.min(jnp.where(is_g & (lg == gmax), lane, big), axis=-1, keepdims=True)
    gw = 1.0 / jnp.sum(jnp.where(is_g, jnp.exp(lg - gmax), 0.0), axis=-1, keepdims=True)
    lo = N_GROUPS + E_PER_GROUP * gidx
    sel = (lane >= lo) & (lane < lo + E_PER_GROUP)
    v1 = jnp.max(jnp.where(sel, lg, -jnp.inf), axis=-1, keepdims=True)
    i1 = jnp.min(jnp.where(sel & (lg == v1), lane, big), axis=-1, keepdims=True)
    sel2 = sel & (lane != i1)
    v2 = jnp.max(jnp.where(sel2, lg, -jnp.inf), axis=-1, keepdims=True)
    i2 = jnp.min(jnp.where(sel2 & (lg == v2), lane, big), axis=-1, keepdims=True)
    e21 = jnp.exp(v2 - v1)
    w1 = gw / (1.0 + e21)
    w2 = gw * e21 / (1.0 + e21)
    out = jnp.where(lane == 0, (i1 - N_GROUPS).astype(F32),
                    jnp.where(lane == 1, (i2 - N_GROUPS).astype(F32),
                              jnp.where(lane == 2, w1, jnp.where(lane == 3, w2, 0.0))))
    r_ref[...] = out


def _outproj(oa, ob, oc, x, mods, mod_row_of_tile, n2, w_out_b, wr_b, layer):
    t, d = x.shape
    tm = TM_PROJ
    tok = lambda i: (i, 0)
    lay = lambda i: (layer, 0, 0)
    mod = lambda k: pl.BlockSpec((None, 1, d), lambda i: (mod_row_of_tile(i, tm), 0, k))
    slab4 = pl.BlockSpec((4, tm, LANES), lambda i: (0, i, 0))
    return pl.pallas_call(
        _outproj_body,
        grid=(t // tm,),
        in_specs=[slab4, slab4, pl.BlockSpec((tm, 1024), tok), pl.BlockSpec((tm, d), tok),
                  mod(2), mod(3), mod(4), pl.BlockSpec((None, 1, d), lay),
                  _resident((None, d, d), lay), _resident((None, d, LANES), lay)],
        out_specs=[pl.BlockSpec((tm, d), tok), pl.BlockSpec((tm, d), tok), pl.BlockSpec((tm, LANES), tok)],
        out_shape=[_sds((t, d)), _sds((t, d)), _sds((t, LANES))],
        compiler_params=_params("arbitrary"), name="outproj",
    )(oa, ob, oc, x, mods, mods, mods, n2, w_out_b, wr_b)


def _moe_body(tile_ref, exp_ref, lo_ref, hi_ref, src0_ref, srcn_ref, h_ref, w1_ref, w3_ref, w2_ref,
              y_ref, xbuf, gsem, w1s, w3s, w2s, *, n_steps):
    tm = xbuf.shape[1]
    s = pl.program_id(0)
    slot = s % 2
    prev = jnp.maximum(s - 1, 0)

    def gather(idx_ref, b):
        for j in range(tm):
            pltpu.make_async_copy(h_ref.at[pl.ds(idx_ref[0, 0, j], 1)], xbuf.at[b, pl.ds(j, 1)], gsem.at[b]).start()

    def gathered(b):
        return pltpu.make_async_copy(h_ref.at[pl.ds(0, tm)], xbuf.at[b], gsem.at[b])

    @pl.when(s == 0)
    def _():
        gather(src0_ref, 0)

    @pl.when((s == 0) | (exp_ref[s] != exp_ref[prev]))
    def _():
        w1s[...] = w1_ref[...].astype(BF16)
        w3s[...] = w3_ref[...].astype(BF16)
        w2s[...] = w2_ref[...].astype(BF16)

    gathered(slot).wait()
    gather(srcn_ref, 1 - slot)
    x = xbuf[slot].astype(BF16)
    h1 = jnp.dot(x, w1s[...], preferred_element_type=F32)
    h3 = jnp.dot(x, w3s[...], preferred_element_type=F32)
    hid = (h1 * (1.0 / (1.0 + jnp.exp(-h1))) * h3).astype(BF16)
    y = jnp.dot(hid, w2s[...], preferred_element_type=F32)
    first_visit = (s == 0) | (tile_ref[s] != tile_ref[prev])

    @pl.when(first_visit)
    def _():
        y_ref[...] = y

    @pl.when(jnp.logical_not(first_visit))
    def _():
        row = lax.broadcasted_iota(I32, (tm, 1), 0)
        y_ref[...] = jnp.where((row >= lo_ref[s]) & (row < hi_ref[s]), y, y_ref[...])

    @pl.when(s == n_steps - 1)
    def _():
        gathered(1 - slot).wait()


def _small_take(table, idx):
    n = table.shape[0]
    hit = idx[:, None] == jnp.arange(n, dtype=I32)[None, :]
    return jnp.sum(jnp.where(hit, table[None, :], 0), axis=1)


def _moe(h2, route, w1, w3, w2, layer):
    t, d = h2.shape
    tm = TM_MOE
    n_pairs = 2 * t
    n_tiles = n_pairs // tm
    n_steps = n_tiles + N_EXPERTS - 1
    ef = route[:, :2].astype(I32).T.reshape(-1)
    order = jnp.argsort(ef, stable=True).astype(I32)
    experts = jnp.arange(N_EXPERTS, dtype=I32)
    counts = jnp.sum((ef[:, None] == experts[None, :]).astype(I32), axis=0)
    cend = jnp.cumsum(counts)
    cstart = cend - counts
    first_row = jnp.arange(n_tiles, dtype=I32) * tm
    e_first = jnp.sum((cend[None, :] <= first_row[:, None]).astype(I32), axis=1)
    e_last = jnp.sum((cend[None, :] <= (first_row + tm - 1)[:, None]).astype(I32), axis=1)
    n_sub = e_last - e_first + 1
    base = jnp.cumsum(n_sub) - n_sub
    step = jnp.arange(n_steps, dtype=I32)
    tile_of = jnp.clip(jnp.sum((base[None, :] <= step[:, None]).astype(I32), axis=1) - 1, 0, n_tiles - 1)
    exp_of = jnp.clip(_small_take(e_first, tile_of) + step - _small_take(base, tile_of), 0, N_EXPERTS - 1)
    used = step < jnp.sum(n_sub)
    row0 = tile_of * tm
    lo = jnp.where(used, jnp.clip(_small_take(cstart, exp_of) - row0, 0, tm), 0)
    hi = jnp.where(used, jnp.clip(_small_take(cend, exp_of) - row0, 0, tm), 0)
    src = (order % t).reshape(n_tiles, 1, tm)

    def smem(f):
        return pl.BlockSpec((1, 1, tm), lambda s, tl, ex, lo_, hi_: (tl[f(s)], 0, 0), memory_space=pltpu.SMEM)

    wspec = lambda shape: pl.BlockSpec((None, None) + shape, lambda s, tl, ex, lo_, hi_: (layer, ex[s], 0, 0))
    grid_spec = pltpu.PrefetchScalarGridSpec(
        num_scalar_prefetch=4, grid=(n_steps,),
        in_specs=[smem(lambda s: s), smem(lambda s: jnp.minimum(s + 1, n_steps - 1)),
                  pl.BlockSpec(memory_space=pl.ANY),
                  wspec((d, D_EXPERT)), wspec((d, D_EXPERT)), wspec((D_EXPERT, d))],
        out_specs=pl.BlockSpec((tm, d), lambda s, tl, ex, lo_, hi_: (tl[s], 0)),
        scratch_shapes=[pltpu.VMEM((2, tm, d), F32), pltpu.SemaphoreType.DMA((2,)),
                        pltpu.VMEM((d, D_EXPERT), BF16), pltpu.VMEM((d, D_EXPERT), BF16),
                        pltpu.VMEM((D_EXPERT, d), BF16)])
    ys = pl.pallas_call(
        functools.partial(_moe_body, n_steps=n_steps),
        grid_spec=grid_spec, out_shape=_sds((n_pairs, d)),
        compiler_params=_params("arbitrary"), name="moe",
    )(tile_of, exp_of, lo, hi, src, src, h2, w1, w3, w2)
    return ys, jnp.argsort(order).astype(I32)


def _combine_body(posc_ref, posn_ref, x1_ref, r_ref, g2_ref, ys_ref, o_ref, ybuf, sem, *, n_tiles):
    tm = x1_ref.shape[0]
    i = pl.program_id(0)
    slot = i % 2

    def gather(idx_ref, b):
        for j in range(2 * tm):
            pltpu.make_async_copy(ys_ref.at[pl.ds(idx_ref[0, 0, j], 1)], ybuf.at[b, pl.ds(j, 1)], sem.at[b]).start()

    @pl.when(i == 0)
    def _():
        gather(posc_ref, 0)

    @pl.when(i + 1 < n_tiles)
    def _():
        gather(posn_ref, 1 - slot)

    pltpu.make_async_copy(ys_ref.at[pl.ds(0, 2 * tm)], ybuf.at[slot], sem.at[slot]).wait()
    r = r_ref[...]
    lane = lax.broadcasted_iota(I32, r.shape, 1)
    w1 = jnp.sum(jnp.where(lane == 2, r, 0.0), axis=-1, keepdims=True)
    w2 = jnp.sum(jnp.where(lane == 3, r, 0.0), axis=-1, keepdims=True)
    o_ref[...] = x1_ref[...] + g2_ref[...] * (w1 * ybuf[slot, :tm] + w2 * ybuf[slot, tm:])


def _combine(x1, ys, pos, route, mods, mod_row_of_tile):
    t, d = x1.shape
    tm = TM_COMBINE
    nt = t // tm
    tok = lambda i: (i, 0)
    idx = pos.reshape(2, nt, tm).transpose(1, 0, 2).reshape(nt, 1, 2 * tm)
    smem = lambda f: pl.BlockSpec((1, 1, 2 * tm), lambda i: (f(i), 0, 0), memory_space=pltpu.SMEM)
    return pl.pallas_call(
        functools.partial(_combine_body, n_tiles=nt),
        grid=(nt,),
        in_specs=[smem(lambda i: i), smem(lambda i: jnp.minimum(i + 1, nt - 1)),
                  pl.BlockSpec((tm, d), tok), pl.BlockSpec((tm, LANES), tok),
                  pl.BlockSpec((None, 1, d), lambda i: (mod_row_of_tile(i, tm), 0, 5)),
                  pl.BlockSpec(memory_space=pl.ANY)],
        out_specs=pl.BlockSpec((tm, d), tok), out_shape=_sds((t, d)),
        scratch_shapes=[pltpu.VMEM((2, 2 * tm, d), F32), pltpu.SemaphoreType.DMA((2,))],
        compiler_params=_params("arbitrary"), name="combine",
    )(idx, idx, x1, route, mods, ys)


def _rope_tables():
    t = jnp.arange(GRID_W * GRID_ROWS)
    pairs = HEAD_DIM // 4
    inv = 1.0 / (ROPE_THETA ** (jnp.arange(pairs, dtype=F32) * 2.0 / (HEAD_DIM // 2)))
    ang_r = (t // GRID_W).astype(F32)[:, None] * inv
    ang_c = (t % GRID_W).astype(F32)[:, None] * inv
    cos = jnp.concatenate([jnp.cos(ang_r)] * 2 + [jnp.cos(ang_c)] * 2, axis=1)
    sin = jnp.concatenate([-jnp.sin(ang_r), jnp.sin(ang_r), -jnp.sin(ang_c), jnp.sin(ang_c)], axis=1)
    return jnp.tile(cos, (1, 2)), jnp.tile(sin, (1, 2))


def _bias_pair_tiles(rpb_l):
    qc = jnp.arange(GRID_W)[:, None]
    kc = jnp.arange(GRID_W)[None, :]
    start_c = jnp.clip(qc - NA_COLS // 2, 0, GRID_W - NA_COLS)
    col_ok = (kc >= start_c) & (kc < start_c + NA_COLS)
    dc = jnp.clip(kc - qc + (NA_COLS - 1), 0, 2 * NA_COLS - 2)
    d = jnp.arange(-8, 23)
    rows = jnp.stack([rpb_l[:, min(max(dd, 0), 2 * NA_ROWS - 2)] for dd in range(-8, 23)], axis=1)
    onehot = (dc[None] == jnp.arange(2 * NA_COLS - 1)[:, None, None]).astype(F32)
    tile = jnp.einsum("hdx,xqk->hdqk", rows, onehot, precision=lax.Precision.HIGHEST)
    ok = col_ok[None, None] & ((d >= 0) & (d <= 2 * NA_ROWS - 2))[None, :, None, None]
    tile = jnp.where(ok, tile, NEG_INF).astype(F32)
    return jnp.concatenate([tile[:, :-1], tile[:, 1:]], axis=-1)


def kernel(x_prompt, x_sample, c, cache_a_k, cache_a_v, cache_b_k, cache_b_v, state_c_fwd, state_c_bwd, c_ctx, norm1_g, norm2_g, w_mod, b_mod, w_in, q_norm_a, k_norm_a, sink_a, q_norm_b, k_norm_b, rpb_b, w_gk_up_f, b_gk_f, w_gk_up_b, b_gk_b, gla_norm_g, w_out, w_group, w_router, w1, w3, w2):
    depth = w_in.shape[0]
    nb_c, l_c, d = x_prompt.shape
    nb_s, n_s, _ = x_sample.shape
    past = cache_a_k.shape[2]
    tm = TM_PROJ

    w_in_b = jnp.pad(w_in, ((0, 0), (0, 0), (0, N_IN_PAD - N_IN))).astype(BF16)
    w_out_b = w_out.astype(BF16)
    wr_b = jnp.pad(jnp.concatenate([w_group, w_router], axis=-1),
                   ((0, 0), (0, 0), (0, LANES - N_GROUPS - N_EXPERTS))).astype(BF16)
    wgk = jnp.zeros((depth, 256, 1024), F32)
    wgk = wgk.at[:, :C_RANK, :512].set(w_gk_up_f).at[:, C_RANK:2 * C_RANK, 512:].set(w_gk_up_b).astype(BF16)
    cos, sin = _rope_tables()
    head_id = jnp.arange(512) // HEAD_DIM
    consts = dict(
        bd=(head_id[:, None] == head_id[None, :]).astype(BF16),
        gqa=jnp.tile(q_norm_a, (1, 8)).reshape(depth, 1, 512),
        gka=jnp.tile(k_norm_a, (1, 2)).reshape(depth, 1, LANES),
        gqb=jnp.tile(q_norm_b, (1, 8)).reshape(depth, 1, 512),
        gkb=jnp.tile(k_norm_b, (1, 8)).reshape(depth, 1, 512),
        cos=cos, sin=sin, wgk=wgk,
        bgk=jnp.concatenate([b_gk_f, b_gk_b], axis=-1).reshape(depth, 1, 1024))
    n1 = norm1_g.reshape(depth, 1, d)
    n2 = norm2_g.reshape(depth, 1, d)
    gn = gla_norm_g.reshape(depth, 1, C_DV)
    cond = jnp.zeros((16, d), F32).at[0].set(c_ctx).at[1:1 + nb_s].set(c)
    ctx_row = lambda i, tile: 0
    lat_row = lambda i, tile: 1 + i // (n_s // tile)
    cak = cache_a_k.reshape(nb_s, depth, 1, past, LANES).astype(BF16)
    cav = cache_a_v.reshape(nb_s, depth, 1, past, LANES).astype(BF16)
    cbk = cache_b_k.reshape(nb_s, depth, past, 4, LANES).transpose(0, 1, 3, 2, 4).astype(BF16)
    cbv = cache_b_v.reshape(nb_s, depth, past, 4, LANES).transpose(0, 1, 3, 2, 4).astype(BF16)

    xp = x_prompt.reshape(nb_c * l_c, d)
    xs = x_sample.reshape(nb_s * n_s, d)
    kv_ctx = st_ctx = None
    for l in range(depth):
        mods = _adaln(cond, w_mod, b_mod, l).reshape(16, 1, 6 * d)
        tp = _bias_pair_tiles(rpb_b[l])

        aq, ak, av, bq, bk, bv, cq, ck, cv, cg, la, *kv_ctx = _inproj(
            xp, mods, ctx_row, n1, w_in_b, consts, l, rope=False, tm=TM_PROJ // 2, seq_len=l_c, prev=kv_ctx)
        oa = _attention("a_ctx", aq, ak.reshape(1, -1, LANES), av.reshape(1, -1, LANES), l_c, sink=sink_a, layer=l)
        ob = _attention("b_ctx", bq, bk, bv, l_c)
        oc, *st_ctx = _gla(cq, ck, cv, cg, la, gn, l_c, l, emit_state=True, prev=st_ctx)
        x1, h2, route = _outproj(oa, ob, oc, xp, mods, ctx_row, n2, w_out_b, wr_b, l)
        xp = _combine(x1, *_moe(h2, route, w1, w3, w2, l), route, mods, ctx_row)

        aq, ak, av, bq, bk, bv, cq, ck, cv, cg, la = _inproj(
            xs, mods, lat_row, n1, w_in_b, consts, l, rope=True, tm=TM_PROJ)
        oa = _attention("a_lat", aq, ak.reshape(1, -1, LANES), av.reshape(1, -1, LANES), n_s,
                        prefix=(cak[:, l], cav[:, l]), sink=sink_a, layer=l)
        ob = _attention("b_lat", bq, bk, bv, n_s, prefix=(cbk[:, l], cbv[:, l]), tp=tp)
        oc, = _gla(cq, ck, cv, cg, la, gn, n_s, l, s0=(state_c_fwd, state_c_bwd), emit_state=False)
        x1, h2, route = _outproj(oa, ob, oc, xs, mods, lat_row, n2, w_out_b, wr_b, l)
        xs = _combine(x1, *_moe(h2, route, w1, w3, w2, l), route, mods, lat_row)

    new_kv = [a.reshape(nb_c, depth, l_c, -1, HEAD_DIM) for a in kv_ctx]
    return (xp.reshape(nb_c, l_c, d), xs.reshape(nb_s, n_s, d), *new_kv, *st_ctx)
```

```python
import functools

import jax
import jax.numpy as jnp
from jax import lax
from jax.experimental import pallas as pl
from jax.experimental.pallas import tpu as pltpu

F32 = jnp.float32
BF16 = jnp.bfloat16
I32 = jnp.int32

D_MODEL = 2048
HEAD_DIM = 64
EPS = 1e-6
NEG_INF = -1e30
ROPE_THETA = 10000.0
GRID_W = 64
GRID_ROWS = 32
A_WINDOW = 128
NA_ROWS = 8
NA_COLS = 16
C_HEADS = 4
C_DK = 128
C_DV = 256
C_RANK = 16
C_CHUNK = 64
GLA_BLOCK = 256
N_GROUPS = 4
E_PER_GROUP = 4
N_EXPERTS = 16
D_EXPERT = 512
LANES = 128
N_IN = 5408
N_IN_PAD = 5632
COL_AQ, COL_AKV, COL_BQ, COL_BK, COL_BV = 0, 512, 768, 1280, 1792
COL_CQ, COL_CK, COL_CV, COL_CG, COL_CR = 2304, 2816, 3328, 4352, 5376
VMEM_LIMIT = 56 * 1024 * 1024
TM_PROJ = 512
TM_MOE = 256
TM_COMBINE = 256

_TRANS_B = (((1,), (1,)), ((), ()))
_TRANS_A = (((0,), (0,)), ((), ()))


def _sds(shape, dtype=F32):
    return jax.ShapeDtypeStruct(shape, dtype)


def _params(*sem):
    return pltpu.CompilerParams(dimension_semantics=sem, vmem_limit_bytes=VMEM_LIMIT)


def _resident(shape, index_map):
    return pl.BlockSpec(shape, index_map, pipeline_mode=pl.Buffered(1))


def _adaln_body(c_ref, w_ref, b_ref, o_ref):
    c = c_ref[...]
    s = c * (1.0 / (1.0 + jnp.exp(-c)))
    o_ref[...] = jnp.dot(s.astype(BF16), w_ref[...].astype(BF16), preferred_element_type=F32) + b_ref[...]


def _adaln(cond, w_mod, b_mod, layer):
    d, n = w_mod.shape[1], w_mod.shape[2]
    tn = 2048
    return pl.pallas_call(
        _adaln_body,
        grid=(n // tn,),
        in_specs=[pl.BlockSpec((16, d), lambda j: (0, 0)),
                  pl.BlockSpec((None, d, tn), lambda j: (layer, 0, j)),
                  pl.BlockSpec((None, 1, tn), lambda j: (layer, 0, j))],
        out_specs=pl.BlockSpec((16, tn), lambda j: (0, j)),
        out_shape=_sds((16, n)),
        compiler_params=_params("arbitrary"),
        name="adaln",
    )(cond, w_mod, b_mod.reshape(b_mod.shape[0], 1, n))


def _head_norm(z, gain, bd):
    w = z.shape[1]
    ss = jnp.dot((z * z).astype(BF16), bd[:w, :w], preferred_element_type=F32)
    return z * lax.rsqrt(ss * (1.0 / HEAD_DIM) + EPS) * gain


def _rope(z, cos, sin_signed, lane):
    lower = (lane % 32) < 16
    partner = jnp.where(lower, pltpu.roll(z, LANES - 16, 1), pltpu.roll(z, 16, 1))
    return z * cos + partner * sin_signed


def _inproj_body(x_ref, sh_ref, sc_ref, n1_ref, w_ref, bd_ref, gqa_ref, gka_ref, gqb_ref, gkb_ref,
                 cos_ref, sin_ref, wgk_ref, bgk_ref,
                 *rest, rope, n_prev, seq_len):
    prev_refs = rest[:4] if n_prev else ()
    rest = rest[4 if n_prev else 0:]
    aq_ref, ak_ref, av_ref, bq_ref, bk_ref, bv_ref, cq_ref, ck_ref, cv_ref, cg_ref, la_ref = rest[:11]
    stack_refs = rest[11:]

    def emit(idx, val):
        if not seq_len:
            return
        ref = stack_refs[idx]
        if n_prev:
            ref[:, :n_prev] = prev_refs[idx][...]
        ref[:, n_prev] = val.reshape(val.shape[0] // seq_len, seq_len, val.shape[1])

    x = x_ref[...]
    h = x * lax.rsqrt(jnp.mean(x * x, axis=-1, keepdims=True) + EPS) * n1_ref[...]
    h = h * (1.0 + sc_ref[...]) + sh_ref[...]
    hb = h.astype(BF16)
    bd = bd_ref[...]
    lane = lax.broadcasted_iota(I32, (x.shape[0], LANES), 1)
    upper = lane >= HEAD_DIM

    def proj(c0, n):
        return jnp.dot(hb, w_ref[:, c0:c0 + n], preferred_element_type=F32)

    def rot(s):
        return _rope(s, cos_ref[...], sin_ref[...], lane) if rope else s

    z = _head_norm(proj(COL_AQ, 512), gqa_ref[...], bd)
    for p in range(4):
        s = rot(z[:, LANES * p:LANES * (p + 1)]) * (HEAD_DIM ** -0.5)
        r = pltpu.roll(s, HEAD_DIM, 1)
        if p // 2 == 0:
            e0, e1 = jnp.where(upper, 0.0, s), jnp.where(upper, 0.0, r)
        else:
            e0, e1 = jnp.where(upper, r, 0.0), jnp.where(upper, s, 0.0)
        aq_ref[2 * p] = e0.astype(BF16)
        aq_ref[2 * p + 1] = e1.astype(BF16)
    z = proj(COL_AKV, 256)
    k = rot(_head_norm(z[:, :LANES], gka_ref[...], bd))
    ak_ref[...] = k.astype(BF16)
    av_ref[...] = z[:, LANES:].astype(BF16)
    emit(0, k)
    emit(1, z[:, LANES:])
    z = _head_norm(proj(COL_BQ, 512), gqb_ref[...], bd) * (HEAD_DIM ** -0.5)
    for p in range(4):
        s = z[:, LANES * p:LANES * (p + 1)]
        bq_ref[2 * p] = jnp.where(upper, 0.0, s).astype(BF16)
        bq_ref[2 * p + 1] = jnp.where(upper, s, 0.0).astype(BF16)
    z = _head_norm(proj(COL_BK, 512), gkb_ref[...], bd)
    for p in range(4):
        bk_ref[p] = z[:, LANES * p:LANES * (p + 1)].astype(BF16)
    emit(2, z)
    z = proj(COL_BV, 512)
    for p in range(4):
        bv_ref[p] = z[:, LANES * p:LANES * (p + 1)].astype(BF16)
    emit(3, z)
    cq_ref[...] = (proj(COL_CQ, 512) * (C_DK ** -0.5)).astype(BF16)
    ck_ref[...] = proj(COL_CK, 512).astype(BF16)
    cv_ref[...] = proj(COL_CV, 1024).astype(BF16)
    cg_ref[...] = proj(COL_CG, 1024).astype(BF16)
    pre = jnp.dot(proj(COL_CR, 256).astype(BF16), wgk_ref[...], preferred_element_type=F32) + bgk_ref[...]
    la_ref[...] = (jnp.minimum(pre, 0.0) - jnp.log(1.0 + jnp.exp(-jnp.abs(pre)))) * (1.0 / 16.0)


def _inproj(x, mods, mod_row_of_tile, n1, w_in_b, consts, layer, *, rope, tm, seq_len=0, prev=None):
    t, d = x.shape
    n_prev = 0 if prev is None else prev[0].shape[1]
    seq_tiles = (GRID_W * GRID_ROWS) // tm
    tok = lambda i: (i, 0)
    full = lambda i: (0, 0)
    lay = lambda i: (layer, 0, 0)
    in_specs = [
        pl.BlockSpec((tm, d), tok),
        pl.BlockSpec((None, 1, d), lambda i: (mod_row_of_tile(i, tm), 0, 0)),
        pl.BlockSpec((None, 1, d), lambda i: (mod_row_of_tile(i, tm), 0, 1)),
        pl.BlockSpec((None, 1, d), lay),
        _resident((None, d, N_IN_PAD), lay),
        _resident((512, 512), full),
        pl.BlockSpec((None, 1, 512), lay), pl.BlockSpec((None, 1, LANES), lay),
        pl.BlockSpec((None, 1, 512), lay), pl.BlockSpec((None, 1, 512), lay),
        pl.BlockSpec((tm, LANES), lambda i: (i % seq_tiles, 0)),
        pl.BlockSpec((tm, LANES), lambda i: (i % seq_tiles, 0)),
        _resident((None, 256, 1024), lay),
        pl.BlockSpec((None, 1, 1024), lay),
    ]
    slab8 = pl.BlockSpec((8, tm, LANES), lambda i: (0, i, 0))
    slab4 = pl.BlockSpec((4, tm, LANES), lambda i: (0, i, 0))
    out_specs = [slab8, pl.BlockSpec((tm, LANES), tok), pl.BlockSpec((tm, LANES), tok),
                 slab8, slab4, slab4,
                 pl.BlockSpec((tm, 512), tok), pl.BlockSpec((tm, 512), tok),
                 pl.BlockSpec((tm, 1024), tok), pl.BlockSpec((tm, 1024), tok), pl.BlockSpec((tm, 1024), tok)]
    out_shape = [_sds((8, t, LANES), BF16), _sds((t, LANES), BF16), _sds((t, LANES), BF16),
                 _sds((8, t, LANES), BF16), _sds((4, t, LANES), BF16), _sds((4, t, LANES), BF16),
                 _sds((t, 512), BF16), _sds((t, 512), BF16), _sds((t, 1024), BF16), _sds((t, 1024), BF16),
                 _sds((t, 1024), F32)]
    args = [x, mods, mods, n1, w_in_b, consts["bd"], consts["gqa"], consts["gka"], consts["gqb"], consts["gkb"],
            consts["cos"], consts["sin"], consts["wgk"], consts["bgk"]]
    if seq_len:
        nseq = tm // seq_len
        for j, width in enumerate((LANES, LANES, 512, 512)):
            if n_prev:
                in_specs.append(pl.BlockSpec((nseq, n_prev, seq_len, width), lambda i: (i, 0, 0, 0)))
                args.append(prev[j])
            out_specs.append(pl.BlockSpec((nseq, n_prev + 1, seq_len, width), lambda i: (i, 0, 0, 0)))
            out_shape.append(_sds((t // seq_len, n_prev + 1, seq_len, width)))
    return pl.pallas_call(
        functools.partial(_inproj_body, rope=rope, n_prev=n_prev, seq_len=seq_len),
        grid=(t // tm,), in_specs=in_specs, out_specs=out_specs, out_shape=out_shape,
        compiler_params=_params("arbitrary"), name="inproj",
    )(*args)


def _attn_body(*refs, kind, tq, kwin, n_seq, layer):
    is_a = kind in ("a_ctx", "a_lat")
    prefix = kind in ("a_lat", "b_lat")
    it = iter(refs)
    q_ref, k_ref, v_ref = next(it), next(it), next(it)
    kc_ref = vc_ref = sink_ref = tp_ref = None
    if prefix:
        kc_ref, vc_ref = next(it), next(it)
    if is_a:
        sink_ref = next(it)
    if kind == "b_lat":
        tp_ref = next(it)
    o_ref = next(it)
    i = pl.program_id(1)

    group = 4 if is_a else 1
    if kind == "a_lat":
        ws = jnp.clip(i * tq - A_WINDOW, 0, n_seq - kwin)
        ws = pl.multiple_of(ws, LANES)
        qpos = i * tq + lax.broadcasted_iota(I32, (group * tq, kwin), 0) % tq
        kpos = ws + lax.broadcasted_iota(I32, (group * tq, kwin), 1)
        allowed = jnp.abs(qpos - kpos) <= A_WINDOW
    elif kind == "b_lat":
        r0 = i * (tq // GRID_W)
        k0 = jnp.clip(r0 - NA_ROWS // 2, 0, GRID_ROWS - kwin // GRID_W)
        ws = pl.multiple_of(k0 * GRID_W, GRID_W)
        qrow = r0 + lax.broadcasted_iota(I32, (tq, kwin), 0) // GRID_W
        krow = k0 + lax.broadcasted_iota(I32, (tq, kwin), 1) // GRID_W
        start = jnp.clip(qrow - NA_ROWS // 2, 0, GRID_ROWS - NA_ROWS)
        allowed = (krow >= start) & (krow < start + NA_ROWS)
        row_mask = jnp.where(allowed, 0.0, NEG_INF)
    else:
        ws = 0
    lane = lax.broadcasted_iota(I32, (tq, LANES), 1)
    upper = lane >= HEAD_DIM

    def attend(q, ks, h, sink):
        kl = k_ref[ks, pl.ds(ws, kwin), :].astype(BF16)
        vl = v_ref[ks, pl.ds(ws, kwin), :].astype(BF16)
        s = lax.dot_general(q, kl, _TRANS_B, preferred_element_type=F32)
        if kind == "a_lat":
            s = jnp.where(allowed, s, NEG_INF)
        elif kind == "b_lat":
            rows = []
            for qr in range(tq // GRID_W):
                u0 = k0 - r0 - qr + 15
                rows.append(jnp.concatenate([tp_ref[h, u0 + 2 * m] for m in range(kwin // LANES)], axis=1))
            s = s + jnp.concatenate(rows, axis=0) + row_mask
        m = jnp.max(s, axis=-1, keepdims=True)
        if prefix:
            sc = lax.dot_general(q, kc_ref[ks], _TRANS_B, preferred_element_type=F32)
            m = jnp.maximum(m, jnp.max(sc, axis=-1, keepdims=True))
        if is_a:
            m = jnp.maximum(m, sink)
        e = jnp.exp(s - m)
        den = jnp.sum(e, axis=-1, keepdims=True)
        o = jnp.dot(e.astype(BF16), vl, preferred_element_type=F32)
        if prefix:
            ec = jnp.exp(sc - m)
            den = den + jnp.sum(ec, axis=-1, keepdims=True)
            o = o + jnp.dot(ec.astype(BF16), vc_ref[ks], preferred_element_type=F32)
        if is_a:
            den = den + jnp.exp(sink - m)
        return o * (1.0 / den)

    if is_a:
        head_of_row = lax.broadcasted_iota(I32, (group * tq, 1), 0) // tq
        for g in range(2):
            q = jnp.concatenate([q_ref[group * g + j] for j in range(group)], axis=0)
            sink = jnp.zeros((group * tq, 1), F32)
            for j in range(group):
                sink = jnp.where(head_of_row == j, sink_ref[layer, group * g + j], sink)
            o = attend(q, 0, None, sink)
            for pp in range(2):
                o0 = o[(2 * pp) * tq:(2 * pp + 1) * tq]
                o1 = o[(2 * pp + 1) * tq:(2 * pp + 2) * tq]
                if g == 0:
                    o1 = pltpu.roll(o1, HEAD_DIM, 1)
                else:
                    o0 = pltpu.roll(o0, HEAD_DIM, 1)
                o_ref[2 * g + pp] = jnp.where(upper, o1, o0).astype(o_ref.dtype)
    else:
        def pair(p):
            o0 = attend(q_ref[2 * p], p, 2 * p, None)
            o1 = attend(q_ref[2 * p + 1], p, 2 * p + 1, None)
            o_ref[p] = jnp.where(upper, o1, o0).astype(o_ref.dtype)

        if kind == "b_lat":
            def body(p, c):
                pair(p)
                return c
            lax.fori_loop(0, 4, body, 0, unroll=4)
        else:
            for p in range(4):
                pair(p)


def _attention(kind, q, k, v, n_seq, *, prefix=None, sink=None, tp=None, layer=0):
    t = q.shape[1]
    nb = t // n_seq
    s_k = k.shape[0]
    tq, kwin = {"a_ctx": (n_seq, n_seq), "b_ctx": (n_seq, n_seq),
                "a_lat": (A_WINDOW, 3 * A_WINDOW), "b_lat": (256, 768)}[kind]
    nq = n_seq // tq
    in_specs = [pl.BlockSpec((8, tq, LANES), lambda b, i: (0, b * nq + i, 0)),
                pl.BlockSpec((s_k, n_seq, LANES), lambda b, i: (0, b, 0)),
                pl.BlockSpec((s_k, n_seq, LANES), lambda b, i: (0, b, 0))]
    args = [q, k, v]
    if prefix is not None:
        kc, vc = prefix
        spec = pl.BlockSpec((None,) + kc.shape[1:], lambda b, i: (b, 0, 0, 0))
        in_specs += [spec, spec]
        args += [kc, vc]
    if sink is not None:
        in_specs.append(pl.BlockSpec(memory_space=pltpu.SMEM))
        args.append(sink)
    if tp is not None:
        in_specs.append(_resident(tp.shape, lambda b, i: (0, 0, 0, 0)))
        args.append(tp)
    return pl.pallas_call(
        functools.partial(_attn_body, kind=kind, tq=tq, kwin=kwin, n_seq=n_seq, layer=layer),
        grid=(nb, nq), in_specs=in_specs,
        out_specs=pl.BlockSpec((4, tq, LANES), lambda b, i: (0, b * nq + i, 0)),
        out_shape=_sds((4, t, LANES), BF16),
        compiler_params=_params("arbitrary", "arbitrary"), name="attn_" + kind,
    )(*args)


def _gla_body(*refs, n_seq, init, emit_state, n_prev):
    it = iter(refs)
    q_ref, k_ref, v_ref, g_ref, laf_ref, lab_ref, gn_ref = (next(it) for _ in range(7))
    s0f_ref = s0b_ref = sf_ref = sb_ref = pf_ref = pb_ref = None
    if init:
        s0f_ref, s0b_ref = next(it), next(it)
    if n_prev:
        pf_ref, pb_ref = next(it), next(it)
    o_ref = next(it)
    if emit_state:
        sf_ref, sb_ref = next(it), next(it)
    acc_ref, qdf_ref, qdb_ref, dsf_ref, dsb_ref, decf_ref, decb_ref = (next(it) for _ in range(7))
    nc = n_seq // C_CHUNK
    per = GLA_BLOCK // C_CHUNK
    row = lax.broadcasted_iota(I32, (GLA_BLOCK, C_DK), 0) % C_CHUNK
    ri = lax.broadcasted_iota(I32, (GLA_BLOCK, GLA_BLOCK), 0)
    ci = lax.broadcasted_iota(I32, (GLA_BLOCK, GLA_BLOCK), 1)
    same = (ri // C_CHUNK) == (ci // C_CHUNK)
    zero_chunk = jnp.zeros((C_CHUNK, C_DK), BF16)

    def decay_sums(la, fwd):
        x = la
        for s in (1, 2, 4, 8, 16, 32):
            if fwd:
                x = x + jnp.where(row >= s, pltpu.roll(x, s, 0), 0.0)
            else:
                x = x + jnp.where(row < C_CHUNK - s, pltpu.roll(x, GLA_BLOCK - s, 0), 0.0)
        return x

    def block(u, carry):
        sl = pl.ds(pl.multiple_of(u * GLA_BLOCK, GLA_BLOCK), GLA_BLOCK)
        q = q_ref[sl, :].astype(F32)
        k = k_ref[sl, :].astype(F32)
        v = v_ref[sl, :]
        att = None
        rhs = []
        for fwd, la_ref, qd_ref, dec_ref in ((True, laf_ref, qdf_ref, decf_ref), (False, lab_ref, qdb_ref, decb_ref)):
            b = decay_sums(la_ref[sl, :], fwd)
            b3 = b.reshape(per, C_CHUNK, C_DK)
            bl3 = b3[:, C_CHUNK - 1:C_CHUNK, :] if fwd else b3[:, 0:1, :]
            qd = (q * jnp.exp(b)).astype(BF16)
            ki = (k * jnp.exp(-b)).astype(BF16)
            ke3 = (k.reshape(per, C_CHUNK, C_DK) * jnp.exp(bl3 - b3)).astype(BF16)
            qd_ref[sl, :] = qd
            dec = jnp.exp(bl3)
            for j in range(per):
                dec_ref[u * per + j, 0:1, :] = dec[j]
            a = lax.dot_general(qd, ki, _TRANS_B, preferred_element_type=F32)
            a = jnp.where(same & ((ci <= ri) if fwd else (ci >= ri)), a, 0.0)
            att = a if att is None else att + a
            rhs += [jnp.concatenate([ke3[i] if i == j else zero_chunk for i in range(per)], axis=0)
                    for j in range(per)]
        acc_ref[sl, :] = jnp.dot(att.astype(BF16), v, preferred_element_type=F32)
        ds = lax.dot_general(v, jnp.concatenate(rhs, axis=1), _TRANS_A, preferred_element_type=F32)
        for j in range(per):
            dsf_ref[u * per + j] = ds[:, C_DK * j:C_DK * (j + 1)]
            dsb_ref[u * per + j] = ds[:, C_DK * (per + j):C_DK * (per + j + 1)]
        return carry

    lax.fori_loop(0, n_seq // GLA_BLOCK, block, 0, unroll=min(8, n_seq // GLA_BLOCK))

    def states(ds_ref, dec_ref, s0_ref, s_out_ref, prev_ref, fwd):
        def chunk(j, st):
            c = j if fwd else nc - 1 - j
            inc = ds_ref[c]
            ds_ref[c] = st
            return st * dec_ref[c, 0:1, :] + inc

        st = lax.fori_loop(0, nc, chunk, s0_ref[...].T if init else jnp.zeros((C_DV, C_DK), F32), unroll=4)
        if emit_state:
            if n_prev:
                s_out_ref[:n_prev] = prev_ref[...]
            s_out_ref[n_prev] = st.T

    states(dsf_ref, decf_ref, s0f_ref, sf_ref, pf_ref, True)
    states(dsb_ref, decb_ref, s0b_ref, sb_ref, pb_ref, False)

    def finish(u, carry):
        sl = pl.ds(pl.multiple_of(u * GLA_BLOCK, GLA_BLOCK), GLA_BLOCK)
        inter = []
        for j in range(per):
            c = u * per + j
            rows = pl.ds(pl.multiple_of(c * C_CHUNK, C_CHUNK), C_CHUNK)
            inter.append(
                lax.dot_general(qdf_ref[rows, :], dsf_ref[c].astype(BF16), _TRANS_B, preferred_element_type=F32)
                + lax.dot_general(qdb_ref[rows, :], dsb_ref[c].astype(BF16), _TRANS_B, preferred_element_type=F32))
        tot = acc_ref[sl, :] + jnp.concatenate(inter, axis=0)
        y = tot * lax.rsqrt(jnp.mean(tot * tot, axis=-1, keepdims=True) + EPS) * gn_ref[...]
        g = g_ref[sl, :].astype(F32)
        o_ref[sl, :] = (y * (g * (1.0 / (1.0 + jnp.exp(-g))))).astype(o_ref.dtype)
        return carry

    lax.fori_loop(0, n_seq // GLA_BLOCK, finish, 0, unroll=min(8, n_seq // GLA_BLOCK))


def _gla(cq, ck, cv, cg, la, gn, n_seq, layer, *, s0=None, emit_state, prev=None):
    t = cq.shape[0]
    nb = t // n_seq
    nc = n_seq // C_CHUNK
    tok = lambda b, h: (b, h)
    in_specs = [pl.BlockSpec((n_seq, C_DK), tok), pl.BlockSpec((n_seq, C_DK), tok),
                pl.BlockSpec((n_seq, C_DV), tok), pl.BlockSpec((n_seq, C_DV), tok),
                pl.BlockSpec((n_seq, C_DK), tok), pl.BlockSpec((n_seq, C_DK), lambda b, h: (b, C_HEADS + h)),
                pl.BlockSpec((None, 1, C_DV), lambda b, h: (layer, 0, 0))]
    args = [cq, ck, cv, cg, la, la, gn]
    if s0 is not None:
        spec = pl.BlockSpec((None, None, None, C_DK, C_DV), lambda b, h: (b, layer, h, 0, 0))
        in_specs += [spec, spec]
        args += list(s0)
    n_prev = 0 if prev is None else prev[0].shape[1]
    if n_prev:
        spec = pl.BlockSpec((None, n_prev, None, C_DK, C_DV), lambda b, h: (b, 0, h, 0, 0))
        in_specs += [spec, spec]
        args += list(prev)
    out_specs = [pl.BlockSpec((n_seq, C_DV), tok)]
    out_shape = [_sds((t, C_HEADS * C_DV), BF16)]
    if emit_state:
        spec = pl.BlockSpec((None, n_prev + 1, None, C_DK, C_DV), lambda b, h: (b, 0, h, 0, 0))
        out_specs += [spec, spec]
        out_shape += [_sds((nb, n_prev + 1, C_HEADS, C_DK, C_DV))] * 2
    return pl.pallas_call(
        functools.partial(_gla_body, n_seq=n_seq, init=s0 is not None, emit_state=emit_state, n_prev=n_prev),
        grid=(nb, C_HEADS), in_specs=in_specs, out_specs=out_specs, out_shape=out_shape,
        scratch_shapes=[pltpu.VMEM((n_seq, C_DV), F32),
                        pltpu.VMEM((n_seq, C_DK), BF16), pltpu.VMEM((n_seq, C_DK), BF16),
                        pltpu.VMEM((nc, C_DV, C_DK), F32), pltpu.VMEM((nc, C_DV, C_DK), F32),
                        pltpu.VMEM((nc, 8, C_DK), F32), pltpu.VMEM((nc, 8, C_DK), F32)],
        compiler_params=_params("arbitrary", "arbitrary"), name="gla",
    )(*args)


def _outproj_body(oa_ref, ob_ref, oc_ref, x_ref, g1_ref, sh_ref, sc_ref, n2_ref, w_ref, wr_ref,
                  x1_ref, h2_ref, r_ref):
    mix = jnp.concatenate([oa_ref[p] for p in range(4)] + [ob_ref[p] for p in range(4)] + [oc_ref[...]], axis=1)
    y = jnp.dot(mix, w_ref[...], preferred_element_type=F32)
    x1 = x_ref[...] + g1_ref[...] * y
    x1_ref[...] = x1
    h = x1 * lax.rsqrt(jnp.mean(x1 * x1, axis=-1, keepdims=True) + EPS) * n2_ref[...]
    h = h * (1.0 + sc_ref[...]) + sh_ref[...]
    h2_ref[...] = h
    lg = jnp.dot(h.astype(BF16), wr_ref[...], preferred_element_type=F32)
    lane = lax.broadcasted_iota(I32, lg.shape, 1)
    big = jnp.int32(LANES)
    is_g = lane < N_GROUPS
    gmax = jnp.max(jnp.where(is_g, lg, -jnp.inf), axis=-1, keepdims=True)
    gidx = jnp.min(jnp.where(is_g & (lg == gmax), lane, big), axis=-1, keepdims=True)
    gw = 1.0 / jnp.sum(jnp.where(is_g, jnp.exp(lg - gmax), 0.0), axis=-1, keepdims=True)
    lo = N_GROUPS + E_PER_GROUP * gidx
    sel = (lane >= lo) & (lane < lo + E_PER_GROUP)
    v1 = jnp.max(jnp.where(sel, lg, -jnp.inf), axis=-1, keepdims=True)
    i1 = jnp.min(jnp.where(sel & (lg == v1), lane, big), axis=-1, keepdims=True)
    sel2 = sel & (lane != i1)
    v2 = jnp.max(jnp.where(sel2, lg, -jnp.inf), axis=-1, keepdims=True)
    i2 = jnp.min(jnp.where(sel2 & (lg == v2), lane, big), axis=-1, keepdims=True)
    e21 = jnp.exp(v2 - v1)
    w1 = gw / (1.0 + e21)
    w2 = gw * e21 / (1.0 + e21)
    out = jnp.where(lane == 0, (i1 - N_GROUPS).astype(F32),
                    jnp.where(lane == 1, (i2 - N_GROUPS).astype(F32),
                              jnp.where(lane == 2, w1, jnp.where(lane == 3, w2, 0.0))))
    r_ref[...] = out


def _outproj(oa, ob, oc, x, mods, mod_row_of_tile, n2, w_out_b, wr_b, layer):
    t, d = x.shape
    tm = TM_PROJ
    tok = lambda i: (i, 0)
    lay = lambda i: (layer, 0, 0)
    mod = lambda k: pl.BlockSpec((None, 1, d), lambda i: (mod_row_of_tile(i, tm), 0, k))
    slab4 = pl.BlockSpec((4, tm, LANES), lambda i: (0, i, 0))
    return pl.pallas_call(
        _outproj_body,
        grid=(t // tm,),
        in_specs=[slab4, slab4, pl.BlockSpec((tm, 1024), tok), pl.BlockSpec((tm, d), tok),
                  mod(2), mod(3), mod(4), pl.BlockSpec((None, 1, d), lay),
                  _resident((None, d, d), lay), _resident((None, d, LANES), lay)],
        out_specs=[pl.BlockSpec((tm, d), tok), pl.BlockSpec((tm, d), tok), pl.BlockSpec((tm, LANES), tok)],
        out_shape=[_sds((t, d)), _sds((t, d)), _sds((t, LANES))],
        compiler_params=_params("arbitrary"), name="outproj",
    )(oa, ob, oc, x, mods, mods, mods, n2, w_out_b, wr_b)


def _moe_body(tile_ref, exp_ref, lo_ref, hi_ref, src0_ref, srcn_ref, h_ref, w1_ref, w3_ref, w2_ref,
              y_ref, xbuf, gsem, w1s, w3s, w2s, *, n_steps):
    tm = xbuf.shape[1]
    s = pl.program_id(0)
    slot = s % 2
    prev = jnp.maximum(s - 1, 0)

    def gather(idx_ref, b):
        for j in range(tm):
            pltpu.make_async_copy(h_ref.at[pl.ds(idx_ref[0, 0, j], 1)], xbuf.at[b, pl.ds(j, 1)], gsem.at[b]).start()

    def gathered(b):
        return pltpu.make_async_copy(h_ref.at[pl.ds(0, tm)], xbuf.at[b], gsem.at[b])

    @pl.when(s == 0)
    def _():
        gather(src0_ref, 0)

    @pl.when((s == 0) | (exp_ref[s] != exp_ref[prev]))
    def _():
        w1s[...] = w1_ref[...].astype(BF16)
        w3s[...] = w3_ref[...].astype(BF16)
        w2s[...] = w2_ref[...].astype(BF16)

    gathered(slot).wait()
    gather(srcn_ref, 1 - slot)
    x = xbuf[slot].astype(BF16)
    h1 = jnp.dot(x, w1s[...], preferred_element_type=F32)
    h3 = jnp.dot(x, w3s[...], preferred_element_type=F32)
    hid = (h1 * (1.0 / (1.0 + jnp.exp(-h1))) * h3).astype(BF16)
    y = jnp.dot(hid, w2s[...], preferred_element_type=F32)
    first_visit = (s == 0) | (tile_ref[s] != tile_ref[prev])

    @pl.when(first_visit)
    def _():
        y_ref[...] = y

    @pl.when(jnp.logical_not(first_visit))
    def _():
        row = lax.broadcasted_iota(I32, (tm, 1), 0)
        y_ref[...] = jnp.where((row >= lo_ref[s]) & (row < hi_ref[s]), y, y_ref[...])

    @pl.when(s == n_steps - 1)
    def _():
        gathered(1 - slot).wait()


def _small_take(table, idx):
    n = table.shape[0]
    hit = idx[:, None] == jnp.arange(n, dtype=I32)[None, :]
    return jnp.sum(jnp.where(hit, table[None, :], 0), axis=1)


def _moe(h2, route, w1, w3, w2, layer):
    t, d = h2.shape
    tm = TM_MOE
    n_pairs = 2 * t
    n_tiles = n_pairs // tm
    n_steps = n_tiles + N_EXPERTS - 1
    ef = route[:, :2].astype(I32).T.reshape(-1)
    order = jnp.argsort(ef, stable=True).astype(I32)
    experts = jnp.arange(N_EXPERTS, dtype=I32)
    counts = jnp.sum((ef[:, None] == experts[None, :]).astype(I32), axis=0)
    cend = jnp.cumsum(counts)
    cstart = cend - counts
    first_row = jnp.arange(n_tiles, dtype=I32) * tm
    e_first = jnp.sum((cend[None, :] <= first_row[:, None]).astype(I32), axis=1)
    e_last = jnp.sum((cend[None, :] <= (first_row + tm - 1)[:, None]).astype(I32), axis=1)
    n_sub = e_last - e_first + 1
    base = jnp.cumsum(n_sub) - n_sub
    step = jnp.arange(n_steps, dtype=I32)
    tile_of = jnp.clip(jnp.sum((base[None, :] <= step[:, None]).astype(I32), axis=1) - 1, 0, n_tiles - 1)
    exp_of = jnp.clip(_small_take(e_first, tile_of) + step - _small_take(base, tile_of), 0, N_EXPERTS - 1)
    used = step < jnp.sum(n_sub)
    row0 = tile_of * tm
    lo = jnp.where(used, jnp.clip(_small_take(cstart, exp_of) - row0, 0, tm), 0)
    hi = jnp.where(used, jnp.clip(_small_take(cend, exp_of) - row0, 0, tm), 0)
    src = (order % t).reshape(n_tiles, 1, tm)

    def smem(f):
        return pl.BlockSpec((1, 1, tm), lambda s, tl, ex, lo_, hi_: (tl[f(s)], 0, 0), memory_space=pltpu.SMEM)

    wspec = lambda shape: pl.BlockSpec((None, None) + shape, lambda s, tl, ex, lo_, hi_: (layer, ex[s], 0, 0))
    grid_spec = pltpu.PrefetchScalarGridSpec(
        num_scalar_prefetch=4, grid=(n_steps,),
        in_specs=[smem(lambda s: s), smem(lambda s: jnp.minimum(s + 1, n_steps - 1)),
                  pl.BlockSpec(memory_space=pl.ANY),
                  wspec((d, D_EXPERT)), wspec((d, D_EXPERT)), wspec((D_EXPERT, d))],
        out_specs=pl.BlockSpec((tm, d), lambda s, tl, ex, lo_, hi_: (tl[s], 0)),
        scratch_shapes=[pltpu.VMEM((2, tm, d), F32), pltpu.SemaphoreType.DMA((2,)),
                        pltpu.VMEM((d, D_EXPERT), BF16), pltpu.VMEM((d, D_EXPERT), BF16),
                        pltpu.VMEM((D_EXPERT, d), BF16)])
    ys = pl.pallas_call(
        functools.partial(_moe_body, n_steps=n_steps),
        grid_spec=grid_spec, out_shape=_sds((n_pairs, d)),
        compiler_params=_params("arbitrary"), name="moe",
    )(tile_of, exp_of, lo, hi, src, src, h2, w1, w3, w2)
    return ys, jnp.argsort(order).astype(I32)


def _combine_body(posc_ref, posn_ref, x1_ref, r_ref, g2_ref, ys_ref, o_ref, ybuf, sem, *, n_tiles):
    tm = x1_ref.shape[0]
    i = pl.program_id(0)
    slot = i % 2

    def gather(idx_ref, b):
        for j in range(2 * tm):
            pltpu.make_async_copy(ys_ref.at[pl.ds(idx_ref[0, 0, j], 1)], ybuf.at[b, pl.ds(j, 1)], sem.at[b]).start()

    @pl.when(i == 0)
    def _():
        gather(posc_ref, 0)

    @pl.when(i + 1 < n_tiles)
    def _():
        gather(posn_ref, 1 - slot)

    pltpu.make_async_copy(ys_ref.at[pl.ds(0, 2 * tm)], ybuf.at[slot], sem.at[slot]).wait()
    r = r_ref[...]
    lane = lax.broadcasted_iota(I32, r.shape, 1)
    w1 = jnp.sum(jnp.where(lane == 2, r, 0.0), axis=-1, keepdims=True)
    w2 = jnp.sum(jnp.where(lane == 3, r, 0.0), axis=-1, keepdims=True)
    o_ref[...] = x1_ref[...] + g2_ref[...] * (w1 * ybuf[slot, :tm] + w2 * ybuf[slot, tm:])


def _combine(x1, ys, pos, route, mods, mod_row_of_tile):
    t, d = x1.shape
    tm = TM_COMBINE
    nt = t // tm
    tok = lambda i: (i, 0)
    idx = pos.reshape(2, nt, tm).transpose(1, 0, 2).reshape(nt, 1, 2 * tm)
    smem = lambda f: pl.BlockSpec((1, 1, 2 * tm), lambda i: (f(i), 0, 0), memory_space=pltpu.SMEM)
    return pl.pallas_call(
        functools.partial(_combine_body, n_tiles=nt),
        grid=(nt,),
        in_specs=[smem(lambda i: i), smem(lambda i: jnp.minimum(i + 1, nt - 1)),
                  pl.BlockSpec((tm, d), tok), pl.BlockSpec((tm, LANES), tok),
                  pl.BlockSpec((None, 1, d), lambda i: (mod_row_of_tile(i, tm), 0, 5)),
                  pl.BlockSpec(memory_space=pl.ANY)],
        out_specs=pl.BlockSpec((tm, d), tok), out_shape=_sds((t, d)),
        scratch_shapes=[pltpu.VMEM((2, 2 * tm, d), F32), pltpu.SemaphoreType.DMA((2,))],
        compiler_params=_params("arbitrary"), name="combine",
    )(idx, idx, x1, route, mods, ys)


def _rope_tables():
    t = jnp.arange(GRID_W * GRID_ROWS)
    pairs = HEAD_DIM // 4
    inv = 1.0 / (ROPE_THETA ** (jnp.arange(pairs, dtype=F32) * 2.0 / (HEAD_DIM // 2)))
    ang_r = (t // GRID_W).astype(F32)[:, None] * inv
    ang_c = (t % GRID_W).astype(F32)[:, None] * inv
    cos = jnp.concatenate([jnp.cos(ang_r)] * 2 + [jnp.cos(ang_c)] * 2, axis=1)
    sin = jnp.concatenate([-jnp.sin(ang_r), jnp.sin(ang_r), -jnp.sin(ang_c), jnp.sin(ang_c)], axis=1)
    return jnp.tile(cos, (1, 2)), jnp.tile(sin, (1, 2))


def _bias_pair_tiles(rpb_l):
    qc = jnp.arange(GRID_W)[:, None]
    kc = jnp.arange(GRID_W)[None, :]
    start_c = jnp.clip(qc - NA_COLS // 2, 0, GRID_W - NA_COLS)
    col_ok = (kc >= start_c) & (kc < start_c + NA_COLS)
    dc = jnp.clip(kc - qc + (NA_COLS - 1), 0, 2 * NA_COLS - 2)
    d = jnp.arange(-8, 23)
    rows = jnp.stack([rpb_l[:, min(max(dd, 0), 2 * NA_ROWS - 2)] for dd in range(-8, 23)], axis=1)
    onehot = (dc[None] == jnp.arange(2 * NA_COLS - 1)[:, None, None]).astype(F32)
    tile = jnp.einsum("hdx,xqk->hdqk", rows, onehot, precision=lax.Precision.HIGHEST)
    ok = col_ok[None, None] & ((d >= 0) & (d <= 2 * NA_ROWS - 2))[None, :, None, None]
    tile = jnp.where(ok, tile, NEG_INF).astype(F32)
    return jnp.concatenate([tile[:, :-1], tile[:, 1:]], axis=-1)


def kernel(x_prompt, x_sample, c, cache_a_k, cache_a_v, cache_b_k, cache_b_v, state_c_fwd, state_c_bwd, c_ctx, norm1_g, norm2_g, w_mod, b_mod, w_in, q_norm_a, k_norm_a, sink_a, q_norm_b, k_norm_b, rpb_b, w_gk_up_f, b_gk_f, w_gk_up_b, b_gk_b, gla_norm_g, w_out, w_group, w_router, w1, w3, w2):
    depth = w_in.shape[0]
    nb_c, l_c, d = x_prompt.shape
    nb_s, n_s, _ = x_sample.shape
    past = cache_a_k.shape[2]
    tm = TM_PROJ

    w_in_b = jnp.pad(w_in, ((0, 0), (0, 0), (0, N_IN_PAD - N_IN))).astype(BF16)
    w_out_b = w_out.astype(BF16)
    wr_b = jnp.pad(jnp.concatenate([w_group, w_router], axis=-1),
                   ((0, 0), (0, 0), (0, LANES - N_GROUPS - N_EXPERTS))).astype(BF16)
    wgk = jnp.zeros((depth, 256, 1024), F32)
    wgk = wgk.at[:, :C_RANK, :512].set(w_gk_up_f).at[:, C_RANK:2 * C_RANK, 512:].set(w_gk_up_b).astype(BF16)
    cos, sin = _rope_tables()
    head_id = jnp.arange(512) // HEAD_DIM
    consts = dict(
        bd=(head_id[:, None] == head_id[None, :]).astype(BF16),
        gqa=jnp.tile(q_norm_a, (1, 8)).reshape(depth, 1, 512),
        gka=jnp.tile(k_norm_a, (1, 2)).reshape(depth, 1, LANES),
        gqb=jnp.tile(q_norm_b, (1, 8)).reshape(depth, 1, 512),
        gkb=jnp.tile(k_norm_b, (1, 8)).reshape(depth, 1, 512),
        cos=cos, sin=sin, wgk=wgk,
        bgk=jnp.concatenate([b_gk_f, b_gk_b], axis=-1).reshape(depth, 1, 1024))
    n1 = norm1_g.reshape(depth, 1, d)
    n2 = norm2_g.reshape(depth, 1, d)
    gn = gla_norm_g.reshape(depth, 1, C_DV)
    cond = jnp.zeros((16, d), F32).at[0].set(c_ctx).at[1:1 + nb_s].set(c)
    ctx_row = lambda i, tile: 0
    lat_row = lambda i, tile: 1 + i // (n_s // tile)
    cak = cache_a_k.reshape(nb_s, depth, 1, past, LANES).astype(BF16)
    cav = cache_a_v.reshape(nb_s, depth, 1, past, LANES).astype(BF16)
    cbk = cache_b_k.reshape(nb_s, depth, past, 4, LANES).transpose(0, 1, 3, 2, 4).astype(BF16)
    cbv = cache_b_v.reshape(nb_s, depth, past, 4, LANES).transpose(0, 1, 3, 2, 4).astype(BF16)

    xp = x_prompt.reshape(nb_c * l_c, d)
    xs = x_sample.reshape(nb_s * n_s, d)
    kv_ctx = st_ctx = None
    for l in range(depth):
        mods = _adaln(cond, w_mod, b_mod, l).reshape(16, 1, 6 * d)
        tp = _bias_pair_tiles(rpb_b[l])

        aq, ak, av, bq, bk, bv, cq, ck, cv, cg, la, *kv_ctx = _inproj(
            xp, mods, ctx_row, n1, w_in_b, consts, l, rope=False, tm=TM_PROJ // 2, seq_len=l_c, prev=kv_ctx)
        oa = _attention("a_ctx", aq, ak.reshape(1, -1, LANES), av.reshape(1, -1, LANES), l_c, sink=sink_a, layer=l)
        ob = _attention("b_ctx", bq, bk, bv, l_c)
        oc, *st_ctx = _gla(cq, ck, cv, cg, la, gn, l_c, l, emit_state=True, prev=st_ctx)
        x1, h2, route = _outproj(oa, ob, oc, xp, mods, ctx_row, n2, w_out_b, wr_b, l)
        xp = _combine(x1, *_moe(h2, route, w1, w3, w2, l), route, mods, ctx_row)

        aq, ak, av, bq, bk, bv, cq, ck, cv, cg, la = _inproj(
            xs, mods, lat_row, n1, w_in_b, consts, l, rope=True, tm=TM_PROJ)
        oa = _attention("a_lat", aq, ak.reshape(1, -1, LANES), av.reshape(1, -1, LANES), n_s,
                        prefix=(cak[:, l], cav[:, l]), sink=sink_a, layer=l)
        ob = _attention("b_lat", bq, bk, bv, n_s, prefix=(cbk[:, l], cbv[:, l]), tp=tp)
        oc, = _gla(cq, ck, cv, cg, la, gn, n_s, l, s0=(state_c_fwd, state_c_bwd), emit_state=False)
        x1, h2, route = _outproj(oa, ob, oc, xs, mods, lat_row, n2, w_out_b, wr_b, l)
        xs = _combine(x1, *_moe(h2, route, w1, w3, w2, l), route, mods, lat_row)

    new_kv = [a.reshape(nb_c, depth, l_c, -1, HEAD_DIM) for a in kv_ctx]
    return (xp.reshape(nb_c, l_c, d), xs.reshape(nb_s, n_s, d), *new_kv, *st_ctx)
```

```python
import functools

import jax
import jax.numpy as jnp
from jax import lax
from jax.experimental import pallas as pl
from jax.experimental.pallas import tpu as pltpu

F32 = jnp.float32
BF16 = jnp.bfloat16
I32 = jnp.int32

D_MODEL = 2048
HEAD_DIM = 64
EPS = 1e-6
NEG_INF = -1e30
ROPE_THETA = 10000.0
GRID_W = 64
GRID_ROWS = 32
A_WINDOW = 128
NA_ROWS = 8
NA_COLS = 16
C_HEADS = 4
C_DK = 128
C_DV = 256
C_RANK = 16
C_CHUNK = 64
GLA_BLOCK = 256
N_GROUPS = 4
E_PER_GROUP = 4
N_EXPERTS = 16
D_EXPERT = 512
LANES = 128
N_IN = 5408
N_IN_PAD = 5632
COL_AQ, COL_AKV, COL_BQ, COL_BK, COL_BV = 0, 512, 768, 1280, 1792
COL_CQ, COL_CK, COL_CV, COL_CG, COL_CR = 2304, 2816, 3328, 4352, 5376
VMEM_LIMIT = 56 * 1024 * 1024
TM_PROJ = 512
TM_MOE = 256
TM_COMBINE = 256

_TRANS_B = (((1,), (1,)), ((), ()))
_TRANS_A = (((0,), (0,)), ((), ()))


def _sds(shape, dtype=F32):
    return jax.ShapeDtypeStruct(shape, dtype)


def _params(*sem):
    return pltpu.CompilerParams(dimension_semantics=sem, vmem_limit_bytes=VMEM_LIMIT)


def _resident(shape, index_map):
    return pl.BlockSpec(shape, index_map, pipeline_mode=pl.Buffered(1))


def _adaln_body(c_ref, w_ref, b_ref, o_ref):
    c = c_ref[...]
    s = c * (1.0 / (1.0 + jnp.exp(-c)))
    o_ref[...] = jnp.dot(s.astype(BF16), w_ref[...].astype(BF16), preferred_element_type=F32) + b_ref[...]


def _adaln(cond, w_mod, b_mod, layer):
    d, n = w_mod.shape[1], w_mod.shape[2]
    tn = 2048
    return pl.pallas_call(
        _adaln_body,
        grid=(n // tn,),
        in_specs=[pl.BlockSpec((16, d), lambda j: (0, 0)),
                  pl.BlockSpec((None, d, tn), lambda j: (layer, 0, j)),
                  pl.BlockSpec((None, 1, tn), lambda j: (layer, 0, j))],
        out_specs=pl.BlockSpec((16, tn), lambda j: (0, j)),
        out_shape=_sds((16, n)),
        compiler_params=_params("arbitrary"),
        name="adaln",
    )(cond, w_mod, b_mod.reshape(b_mod.shape[0], 1, n))


def _head_norm(z, gain, bd):
    w = z.shape[1]
    ss = jnp.dot((z * z).astype(BF16), bd[:w, :w], preferred_element_type=F32)
    return z * lax.rsqrt(ss * (1.0 / HEAD_DIM) + EPS) * gain


def _rope(z, cos, sin_signed, lane):
    lower = (lane % 32) < 16
    partner = jnp.where(lower, pltpu.roll(z, LANES - 16, 1), pltpu.roll(z, 16, 1))
    return z * cos + partner * sin_signed


def _inproj_body(x_ref, sh_ref, sc_ref, n1_ref, w_ref, bd_ref, gqa_ref, gka_ref, gqb_ref, gkb_ref,
                 cos_ref, sin_ref, wgk_ref, bgk_ref,
                 *rest, rope, n_prev, seq_len):
    prev_refs = rest[:4] if n_prev else ()
    rest = rest[4 if n_prev else 0:]
    aq_ref, ak_ref, av_ref, bq_ref, bk_ref, bv_ref, cq_ref, ck_ref, cv_ref, cg_ref, la_ref = rest[:11]
    stack_refs = rest[11:]

    def emit(idx, val):
        if not seq_len:
            return
        ref = stack_refs[idx]
        if n_prev:
            ref[:, :n_prev] = prev_refs[idx][...]
        ref[:, n_prev] = val.reshape(val.shape[0] // seq_len, seq_len, val.shape[1])

    x = x_ref[...]
    h = x * lax.rsqrt(jnp.mean(x * x, axis=-1, keepdims=True) + EPS) * n1_ref[...]
    h = h * (1.0 + sc_ref[...]) + sh_ref[...]
    hb = h.astype(BF16)
    bd = bd_ref[...]
    lane = lax.broadcasted_iota(I32, (x.shape[0], LANES), 1)
    upper = lane >= HEAD_DIM

    def proj(c0, n):
        return jnp.dot(hb, w_ref[:, c0:c0 + n], preferred_element_type=F32)

    def rot(s):
        return _rope(s, cos_ref[...], sin_ref[...], lane) if rope else s

    z = _head_norm(proj(COL_AQ, 512), gqa_ref[...], bd)
    for p in range(4):
        s = rot(z[:, LANES * p:LANES * (p + 1)]) * (HEAD_DIM ** -0.5)
        r = pltpu.roll(s, HEAD_DIM, 1)
        if p // 2 == 0:
            e0, e1 = jnp.where(upper, 0.0, s), jnp.where(upper, 0.0, r)
        else:
            e0, e1 = jnp.where(upper, r, 0.0), jnp.where(upper, s, 0.0)
        aq_ref[2 * p] = e0.astype(BF16)
        aq_ref[2 * p + 1] = e1.astype(BF16)
    z = proj(COL_AKV, 256)
    k = rot(_head_norm(z[:, :LANES], gka_ref[...], bd))
    ak_ref[...] = k.astype(BF16)
    av_ref[...] = z[:, LANES:].astype(BF16)
    emit(0, k)
    emit(1, z[:, LANES:])
    z = _head_norm(proj(COL_BQ, 512), gqb_ref[...], bd) * (HEAD_DIM ** -0.5)
    for p in range(4):
        s = z[:, LANES * p:LANES * (p + 1)]
        bq_ref[2 * p] = jnp.where(upper, 0.0, s).astype(BF16)
        bq_ref[2 * p + 1] = jnp.where(upper, s, 0.0).astype(BF16)
    z = _head_norm(proj(COL_BK, 512), gkb_ref[...], bd)
    for p in range(4):
        bk_ref[p] = z[:, LANES * p:LANES * (p + 1)].astype(BF16)
    emit(2, z)
    z = proj(COL_BV, 512)
    for p in range(4):
        bv_ref[p] = z[:, LANES * p:LANES * (p + 1)].astype(BF16)
    emit(3, z)
    cq_ref[...] = (proj(COL_CQ, 512) * (C_DK ** -0.5)).astype(BF16)
    ck_ref[...] = proj(COL_CK, 512).astype(BF16)
    cv_ref[...] = proj(COL_CV, 1024).astype(BF16)
    cg_ref[...] = proj(COL_CG, 1024).astype(BF16)
    pre = jnp.dot(proj(COL_CR, 256).astype(BF16), wgk_ref[...], preferred_element_type=F32) + bgk_ref[...]
    la_ref[...] = (jnp.minimum(pre, 0.0) - jnp.log(1.0 + jnp.exp(-jnp.abs(pre)))) * (1.0 / 16.0)


def _inproj(x, mods, mod_row_of_tile, n1, w_in_b, consts, layer, *, rope, tm, seq_len=0, prev=None):
    t, d = x.shape
    n_prev = 0 if prev is None else prev[0].shape[1]
    seq_tiles = (GRID_W * GRID_ROWS) // tm
    tok = lambda i: (i, 0)
    full = lambda i: (0, 0)
    lay = lambda i: (layer, 0, 0)
    in_specs = [
        pl.BlockSpec((tm, d), tok),
        pl.BlockSpec((None, 1, d), lambda i: (mod_row_of_tile(i, tm), 0, 0)),
        pl.BlockSpec((None, 1, d), lambda i: (mod_row_of_tile(i, tm), 0, 1)),
        pl.BlockSpec((None, 1, d), lay),
        _resident((None, d, N_IN_PAD), lay),
        _resident((512, 512), full),
        pl.BlockSpec((None, 1, 512), lay), pl.BlockSpec((None, 1, LANES), lay),
        pl.BlockSpec((None, 1, 512), lay), pl.BlockSpec((None, 1, 512), lay),
        pl.BlockSpec((tm, LANES), lambda i: (i % seq_tiles, 0)),
        pl.BlockSpec((tm, LANES), lambda i: (i % seq_tiles, 0)),
        _resident((None, 256, 1024), lay),
        pl.BlockSpec((None, 1, 1024), lay),
    ]
    slab8 = pl.BlockSpec((8, tm, LANES), lambda i: (0, i, 0))
    slab4 = pl.BlockSpec((4, tm, LANES), lambda i: (0, i, 0))
    out_specs = [slab8, pl.BlockSpec((tm, LANES), tok), pl.BlockSpec((tm, LANES), tok),
                 slab8, slab4, slab4,
                 pl.BlockSpec((tm, 512), tok), pl.BlockSpec((tm, 512), tok),
                 pl.BlockSpec((tm, 1024), tok), pl.BlockSpec((tm, 1024), tok), pl.BlockSpec((tm, 1024), tok)]
    out_shape = [_sds((8, t, LANES), BF16), _sds((t, LANES), BF16), _sds((t, LANES), BF16),
                 _sds((8, t, LANES), BF16), _sds((4, t, LANES), BF16), _sds((4, t, LANES), BF16),
                 _sds((t, 512), BF16), _sds((t, 512), BF16), _sds((t, 1024), BF16), _sds((t, 1024), BF16),
                 _sds((t, 1024), F32)]
    args = [x, mods, mods, n1, w_in_b, consts["bd"], consts["gqa"], consts["gka"], consts["gqb"], consts["gkb"],
            consts["cos"], consts["sin"], consts["wgk"], consts["bgk"]]
    if seq_len:
        nseq = tm // seq_len
        for j, width in enumerate((LANES, LANES, 512, 512)):
            if n_prev:
                in_specs.append(pl.BlockSpec((nseq, n_prev, seq_len, width), lambda i: (i, 0, 0, 0)))
                args.append(prev[j])
            out_specs.append(pl.BlockSpec((nseq, n_prev + 1, seq_len, width), lambda i: (i, 0, 0, 0)))
            out_shape.append(_sds((t // seq_len, n_prev + 1, seq_len, width)))
    return pl.pallas_call(
        functools.partial(_inproj_body, rope=rope, n_prev=n_prev, seq_len=seq_len),
        grid=(t // tm,), in_specs=in_specs, out_specs=out_specs, out_shape=out_shape,
        compiler_params=_params("arbitrary"), name="inproj",
    )(*args)


def _attn_body(*refs, kind, tq, kwin, n_seq, layer):
    is_a = kind in ("a_ctx", "a_lat")
    prefix = kind in ("a_lat", "b_lat")
    it = iter(refs)
    q_ref, k_ref, v_ref = next(it), next(it), next(it)
    kc_ref = vc_ref = sink_ref = tp_ref = None
    if prefix:
        kc_ref, vc_ref = next(it), next(it)
    if is_a:
        sink_ref = next(it)
    if kind == "b_lat":
        tp_ref = next(it)
    o_ref = next(it)
    i = pl.program_id(1)

    group = 4 if is_a else 1
    if kind == "a_lat":
        ws = jnp.clip(i * tq - A_WINDOW, 0, n_seq - kwin)
        ws = pl.multiple_of(ws, LANES)
        qpos = i * tq + lax.broadcasted_iota(I32, (group * tq, kwin), 0) % tq
        kpos = ws + lax.broadcasted_iota(I32, (group * tq, kwin), 1)
        allowed = jnp.abs(qpos - kpos) <= A_WINDOW
    elif kind == "b_lat":
        r0 = i * (tq // GRID_W)
        k0 = jnp.clip(r0 - NA_ROWS // 2, 0, GRID_ROWS - kwin // GRID_W)
        ws = pl.multiple_of(k0 * GRID_W, GRID_W)
        qrow = r0 + lax.broadcasted_iota(I32, (tq, kwin), 0) // GRID_W
        krow = k0 + lax.broadcasted_iota(I32, (tq, kwin), 1) // GRID_W
        start = jnp.clip(qrow - NA_ROWS // 2, 0, GRID_ROWS - NA_ROWS)
        allowed = (krow >= start) & (krow < start + NA_ROWS)
        row_mask = jnp.where(allowed, 0.0, NEG_INF)
    else:
        ws = 0
    lane = lax.broadcasted_iota(I32, (tq, LANES), 1)
    upper = lane >= HEAD_DIM

    def attend(q, ks, h, sink):
        kl = k_ref[ks, pl.ds(ws, kwin), :].astype(BF16)
        vl = v_ref[ks, pl.ds(ws, kwin), :].astype(BF16)
        s = lax.dot_general(q, kl, _TRANS_B, preferred_element_type=F32)
        if kind == "a_lat":
            s = jnp.where(allowed, s, NEG_INF)
        elif kind == "b_lat":
            rows = []
            for qr in range(tq // GRID_W):
                u0 = k0 - r0 - qr + 15
                rows.append(jnp.concatenate([tp_ref[h, u0 + 2 * m] for m in range(kwin // LANES)], axis=1))
            s = s + jnp.concatenate(rows, axis=0) + row_mask
        m = jnp.max(s, axis=-1, keepdims=True)
        if prefix:
            sc = lax.dot_general(q, kc_ref[ks], _TRANS_B, preferred_element_type=F32)
            m = jnp.maximum(m, jnp.max(sc, axis=-1, keepdims=True))
        if is_a:
            m = jnp.maximum(m, sink)
        e = jnp.exp(s - m)
        den = jnp.sum(e, axis=-1, keepdims=True)
        o = jnp.dot(e.astype(BF16), vl, preferred_element_type=F32)
        if prefix:
            ec = jnp.exp(sc - m)
            den = den + jnp.sum(ec, axis=-1, keepdims=True)
            o = o + jnp.dot(ec.astype(BF16), vc_ref[ks], preferred_element_type=F32)
        if is_a:
            den = den + jnp.exp(sink - m)
        return o * (1.0 / den)

    if is_a:
        head_of_row = lax.broadcasted_iota(I32, (group * tq, 1), 0) // tq
        for g in range(2):
            q = jnp.concatenate([q_ref[group * g + j] for j in range(group)], axis=0)
            sink = jnp.zeros((group * tq, 1), F32)
            for j in range(group):
                sink = jnp.where(head_of_row == j, sink_ref[layer, group * g + j], sink)
            o = attend(q, 0, None, sink)
            for pp in range(2):
                o0 = o[(2 * pp) * tq:(2 * pp + 1) * tq]
                o1 = o[(2 * pp + 1) * tq:(2 * pp + 2) * tq]
                if g == 0:
                    o1 = pltpu.roll(o1, HEAD_DIM, 1)
                else:
                    o0 = pltpu.roll(o0, HEAD_DIM, 1)
                o_ref[2 * g + pp] = jnp.where(upper, o1, o0).astype(o_ref.dtype)
    else:
        def pair(p):
            o0 = attend(q_ref[2 * p], p, 2 * p, None)
            o1 = attend(q_ref[2 * p + 1], p, 2 * p + 1, None)
            o_ref[p] = jnp.where(upper, o1, o0).astype(o_ref.dtype)

        if kind == "b_lat":
            def body(p, c):
                pair(p)
                return c
            lax.fori_loop(0, 4, body, 0, unroll=4)
        else:
            for p in range(4):
                pair(p)


def _attention(kind, q, k, v, n_seq, *, prefix=None, sink=None, tp=None, layer=0):
    t = q.shape[1]
    nb = t // n_seq
    s_k = k.shape[0]
    tq, kwin = {"a_ctx": (n_seq, n_seq), "b_ctx": (n_seq, n_seq),
                "a_lat": (A_WINDOW, 3 * A_WINDOW), "b_lat": (256, 768)}[kind]
    nq = n_seq // tq
    in_specs = [pl.BlockSpec((8, tq, LANES), lambda b, i: (0, b * nq + i, 0)),
                pl.BlockSpec((s_k, n_seq, LANES), lambda b, i: (0, b, 0)),
                pl.BlockSpec((s_k, n_seq, LANES), lambda b, i: (0, b, 0))]
    args = [q, k, v]
    if prefix is not None:
        kc, vc = prefix
        spec = pl.BlockSpec((None,) + kc.shape[1:], lambda b, i: (b, 0, 0, 0))
        in_specs += [spec, spec]
        args += [kc, vc]
    if sink is not None:
        in_specs.append(pl.BlockSpec(memory_space=pltpu.SMEM))
        args.append(sink)
    if tp is not None:
        in_specs.append(_resident(tp.shape, lambda b, i: (0, 0, 0, 0)))
        args.append(tp)
    return pl.pallas_call(
        functools.partial(_attn_body, kind=kind, tq=tq, kwin=kwin, n_seq=n_seq, layer=layer),
        grid=(nb, nq), in_specs=in_specs,
        out_specs=pl.BlockSpec((4, tq, LANES), lambda b, i: (0, b * nq + i, 0)),
        out_shape=_sds((4, t, LANES), BF16),
        compiler_params=_params("arbitrary", "arbitrary"), name="attn_" + kind,
    )(*args)


def _gla_body(*refs, n_seq, init, emit_state, n_prev, hps):
    it = iter(refs)
    q, k, v, g, laf, lab, gn_ref = (next(it) for _ in range(7))
    s0f = s0b = sf = sb = pf = pb = None
    if init:
        s0f, s0b = next(it), next(it)
    if n_prev:
        pf, pb = next(it), next(it)
    o = next(it)
    if emit_state:
        sf, sb = next(it), next(it)
    scratch = [next(it) for _ in range(7)]
    for h in range(hps):
        dk = (slice(None), pl.ds(h * C_DK, C_DK))
        dv = (slice(None), pl.ds(h * C_DV, C_DV))
        pick = lambda r, idx: None if r is None else r.at[idx]
        _gla_head(q.at[dk], k.at[dk], v.at[dv], g.at[dv], laf.at[dk], lab.at[dk], gn_ref,
                  pick(s0f, h), pick(s0b, h), pick(pf, (slice(None), h)), pick(pb, (slice(None), h)),
                  o.at[dv], pick(sf, (slice(None), h)), pick(sb, (slice(None), h)),
                  *[r.at[h] for r in scratch],
                  n_seq=n_seq, init=init, emit_state=emit_state, n_prev=n_prev)


def _gla_head(q_ref, k_ref, v_ref, g_ref, laf_ref, lab_ref, gn_ref, s0f_ref, s0b_ref, pf_ref, pb_ref,
              o_ref, sf_ref, sb_ref, acc_ref, qdf_ref, qdb_ref, dsf_ref, dsb_ref, decf_ref, decb_ref,
              *, n_seq, init, emit_state, n_prev):
    nc = n_seq // C_CHUNK
    per = GLA_BLOCK // C_CHUNK
    row = lax.broadcasted_iota(I32, (GLA_BLOCK, C_DK), 0) % C_CHUNK
    ri = lax.broadcasted_iota(I32, (GLA_BLOCK, GLA_BLOCK), 0)
    ci = lax.broadcasted_iota(I32, (GLA_BLOCK, GLA_BLOCK), 1)
    same = (ri // C_CHUNK) == (ci // C_CHUNK)
    zero_chunk = jnp.zeros((C_CHUNK, C_DK), BF16)

    def decay_sums(la, fwd):
        x = la
        for s in (1, 2, 4, 8, 16, 32):
            if fwd:
                x = x + jnp.where(row >= s, pltpu.roll(x, s, 0), 0.0)
            else:
                x = x + jnp.where(row < C_CHUNK - s, pltpu.roll(x, GLA_BLOCK - s, 0), 0.0)
        return x

    def block(u, carry):
        sl = pl.ds(pl.multiple_of(u * GLA_BLOCK, GLA_BLOCK), GLA_BLOCK)
        q = q_ref[sl, :].astype(F32)
        k = k_ref[sl, :].astype(F32)
        v = v_ref[sl, :]
        att = None
        rhs = []
        for fwd, la_ref, qd_ref, dec_ref in ((True, laf_ref, qdf_ref, decf_ref), (False, lab_ref, qdb_ref, decb_ref)):
            b = decay_sums(la_ref[sl, :], fwd)
            b3 = b.reshape(per, C_CHUNK, C_DK)
            bl3 = b3[:, C_CHUNK - 1:C_CHUNK, :] if fwd else b3[:, 0:1, :]
            qd = (q * jnp.exp(b)).astype(BF16)
            ki = (k * jnp.exp(-b)).astype(BF16)
            ke3 = (k.reshape(per, C_CHUNK, C_DK) * jnp.exp(bl3 - b3)).astype(BF16)
            qd_ref[sl, :] = qd
            dec = jnp.exp(bl3)
            for j in range(per):
                dec_ref[u * per + j, 0:1, :] = dec[j]
            a = lax.dot_general(qd, ki, _TRANS_B, preferred_element_type=F32)
            a = jnp.where(same & ((ci <= ri) if fwd else (ci >= ri)), a, 0.0)
            att = a if att is None else att + a
            rhs += [jnp.concatenate([ke3[i] if i == j else zero_chunk for i in range(per)], axis=0)
                    for j in range(per)]
        acc_ref[sl, :] = jnp.dot(att.astype(BF16), v, preferred_element_type=F32)
        ds = lax.dot_general(v, jnp.concatenate(rhs, axis=1), _TRANS_A, preferred_element_type=F32)
        for j in range(per):
            dsf_ref[u * per + j] = ds[:, C_DK * j:C_DK * (j + 1)]
            dsb_ref[u * per + j] = ds[:, C_DK * (per + j):C_DK * (per + j + 1)]
        return carry

    lax.fori_loop(0, n_seq // GLA_BLOCK, block, 0, unroll=min(8, n_seq // GLA_BLOCK))

    def states(ds_ref, dec_ref, s0_ref, s_out_ref, prev_ref, fwd):
        def chunk(j, st):
            c = j if fwd else nc - 1 - j
            inc = ds_ref[c]
            ds_ref[c] = st
            return st * dec_ref[c, 0:1, :] + inc

        st = lax.fori_loop(0, nc, chunk, s0_ref[...].T if init else jnp.zeros((C_DV, C_DK), F32), unroll=4)
        if emit_state:
            if n_prev:
                s_out_ref[:n_prev] = prev_ref[...]
            s_out_ref[n_prev] = st.T

    states(dsf_ref, decf_ref, s0f_ref, sf_ref, pf_ref, True)
    states(dsb_ref, decb_ref, s0b_ref, sb_ref, pb_ref, False)

    def finish(u, carry):
        sl = pl.ds(pl.multiple_of(u * GLA_BLOCK, GLA_BLOCK), GLA_BLOCK)
        inter = []
        for j in range(per):
            c = u * per + j
            rows = pl.ds(pl.multiple_of(c * C_CHUNK, C_CHUNK), C_CHUNK)
            inter.append(
                lax.dot_general(qdf_ref[rows, :], dsf_ref[c].astype(BF16), _TRANS_B, preferred_element_type=F32)
                + lax.dot_general(qdb_ref[rows, :], dsb_ref[c].astype(BF16), _TRANS_B, preferred_element_type=F32))
        tot = acc_ref[sl, :] + jnp.concatenate(inter, axis=0)
        y = tot * lax.rsqrt(jnp.mean(tot * tot, axis=-1, keepdims=True) + EPS) * gn_ref[...]
        g = g_ref[sl, :].astype(F32)
        o_ref[sl, :] = (y * (g * (1.0 / (1.0 + jnp.exp(-g))))).astype(o_ref.dtype)
        return carry

    lax.fori_loop(0, n_seq // GLA_BLOCK, finish, 0, unroll=min(8, n_seq // GLA_BLOCK))


def _gla(cq, ck, cv, cg, la, gn, n_seq, layer, *, s0=None, emit_state, prev=None):
    t = cq.shape[0]
    nb = t // n_seq
    nc = n_seq // C_CHUNK
    hps = C_HEADS if n_seq <= GLA_BLOCK else 1
    tok = lambda b, h: (b, h)
    wk, wv = hps * C_DK, hps * C_DV
    in_specs = [pl.BlockSpec((n_seq, wk), tok), pl.BlockSpec((n_seq, wk), tok),
                pl.BlockSpec((n_seq, wv), tok), pl.BlockSpec((n_seq, wv), tok),
                pl.BlockSpec((n_seq, wk), tok), pl.BlockSpec((n_seq, wk), lambda b, h: (b, C_HEADS // hps + h)),
                pl.BlockSpec((None, 1, C_DV), lambda b, h: (layer, 0, 0))]
    args = [cq, ck, cv, cg, la, la, gn]
    if s0 is not None:
        spec = pl.BlockSpec((None, None, hps, C_DK, C_DV), lambda b, h: (b, layer, h, 0, 0))
        in_specs += [spec, spec]
        args += list(s0)
    n_prev = 0 if prev is None else prev[0].shape[1]
    if n_prev:
        spec = pl.BlockSpec((None, n_prev, hps, C_DK, C_DV), lambda b, h: (b, 0, h, 0, 0))
        in_specs += [spec, spec]
        args += list(prev)
    out_specs = [pl.BlockSpec((n_seq, wv), tok)]
    out_shape = [_sds((t, C_HEADS * C_DV), BF16)]
    if emit_state:
        spec = pl.BlockSpec((None, n_prev + 1, hps, C_DK, C_DV), lambda b, h: (b, 0, h, 0, 0))
        out_specs += [spec, spec]
        out_shape += [_sds((nb, n_prev + 1, C_HEADS, C_DK, C_DV))] * 2
    return pl.pallas_call(
        functools.partial(_gla_body, n_seq=n_seq, init=s0 is not None, emit_state=emit_state, n_prev=n_prev,
                          hps=hps),
        grid=(nb, C_HEADS // hps), in_specs=in_specs, out_specs=out_specs, out_shape=out_shape,
        scratch_shapes=[pltpu.VMEM((hps, n_seq, C_DV), F32),
                        pltpu.VMEM((hps, n_seq, C_DK), BF16), pltpu.VMEM((hps, n_seq, C_DK), BF16),
                        pltpu.VMEM((hps, nc, C_DV, C_DK), F32), pltpu.VMEM((hps, nc, C_DV, C_DK), F32),
                        pltpu.VMEM((hps, nc, 8, C_DK), F32), pltpu.VMEM((hps, nc, 8, C_DK), F32)],
        compiler_params=_params("arbitrary", "arbitrary"), name="gla",
    )(*args)


def _outproj_body(oa_ref, ob_ref, oc_ref, x_ref, g1_ref, sh_ref, sc_ref, n2_ref, w_ref, wr_ref,
                  x1_ref, h2_ref, r_ref):
    mix = jnp.concatenate([oa_ref[p] for p in range(4)] + [ob_ref[p] for p in range(4)] + [oc_ref[...]], axis=1)
    y = jnp.dot(mix, w_ref[...], preferred_element_type=F32)
    x1 = x_ref[...] + g1_ref[...] * y
    x1_ref[...] = x1
    h = x1 * lax.rsqrt(jnp.mean(x1 * x1, axis=-1, keepdims=True) + EPS) * n2_ref[...]
    h = h * (1.0 + sc_ref[...]) + sh_ref[...]
    h2_ref[...] = h
    lg = jnp.dot(h.astype(BF16), wr_ref[...], preferred_element_type=F32)
    lane = lax.broadcasted_iota(I32, lg.shape, 1)
    big = jnp.int32(LANES)
    is_g = lane < N_GROUPS
    gmax = jnp.max(jnp.where(is_g, lg, -jnp.inf), axis=-1, keepdims=True)
    gidx = jnp.min(jnp.where(is_g & (lg == gmax), lane, big), axis=-1, keepdims=True)
    gw = 1.0 / jnp.sum(jnp.where(is_g, jnp.exp(lg - gmax), 0.0), axis=-1, keepdims=True)
    lo = N_GROUPS + E_PER_GROUP * gidx
    sel = (lane >= lo) & (lane < lo + E_PER_GROUP)
    v1 = jnp.max(jnp.where(sel, lg, -jnp.inf), axis=-1, keepdims=True)
    i1 = jnp.min(jnp.where(sel & (lg == v1), lane, big), axis=-1, keepdims=True)
    sel2 = sel & (lane != i1)
    v2 = jnp.max(jnp.where(sel2, lg, -jnp.inf), axis=-1, keepdims=True)
    i2 = jnp.min(jnp.where(sel2 & (lg == v2), lane, big), axis=-1, keepdims=True)
    e21 = jnp.exp(v2 - v1)
    w1 = gw / (1.0 + e21)
    w2 = gw * e21 / (1.0 + e21)
    out = jnp.where(lane == 0, (i1 - N_GROUPS).astype(F32),
                    jnp.where(lane == 1, (i2 - N_GROUPS).astype(F32),
                              jnp.where(lane == 2, w1, jnp.where(lane == 3, w2, 0.0))))
    r_ref[...] = out


def _outproj(oa, ob, oc, x, mods, mod_row_of_tile, n2, w_out_b, wr_b, layer):
    t, d = x.shape
    tm = TM_PROJ
    tok = lambda i: (i, 0)
    lay = lambda i: (layer, 0, 0)
    mod = lambda k: pl.BlockSpec((None, 1, d), lambda i: (mod_row_of_tile(i, tm), 0, k))
    slab4 = pl.BlockSpec((4, tm, LANES), lambda i: (0, i, 0))
    return pl.pallas_call(
        _outproj_body,
        grid=(t // tm,),
        in_specs=[slab4, slab4, pl.BlockSpec((tm, 1024), tok), pl.BlockSpec((tm, d), tok),
                  mod(2), mod(3), mod(4), pl.BlockSpec((None, 1, d), lay),
                  _resident((None, d, d), lay), _resident((None, d, LANES), lay)],
        out_specs=[pl.BlockSpec((tm, d), tok), pl.BlockSpec((tm, d), tok), pl.BlockSpec((tm, LANES), tok)],
        out_shape=[_sds((t, d)), _sds((t, d)), _sds((t, LANES))],
        compiler_params=_params("arbitrary"), name="outproj",
    )(oa, ob, oc, x, mods, mods, mods, n2, w_out_b, wr_b)


def _moe_body(tile_ref, exp_ref, lo_ref, hi_ref, src0_ref, srcn_ref, h_ref, w1_ref, w3_ref, w2_ref,
              y_ref, xbuf, gsem, w1s, w3s, w2s, *, n_steps):
    tm = xbuf.shape[1]
    s = pl.program_id(0)
    slot = s % 2
    prev = jnp.maximum(s - 1, 0)

    def gather(idx_ref, b):
        for j in range(tm):
            pltpu.make_async_copy(h_ref.at[pl.ds(idx_ref[0, 0, j], 1)], xbuf.at[b, pl.ds(j, 1)], gsem.at[b]).start()

    def gathered(b):
        return pltpu.make_async_copy(h_ref.at[pl.ds(0, tm)], xbuf.at[b], gsem.at[b])

    @pl.when(s == 0)
    def _():
        gather(src0_ref, 0)

    @pl.when((s == 0) | (exp_ref[s] != exp_ref[prev]))
    def _():
        w1s[...] = w1_ref[...].astype(BF16)
        w3s[...] = w3_ref[...].astype(BF16)
        w2s[...] = w2_ref[...].astype(BF16)

    gathered(slot).wait()
    gather(srcn_ref, 1 - slot)
    x = xbuf[slot].astype(BF16)
    h1 = jnp.dot(x, w1s[...], preferred_element_type=F32)
    h3 = jnp.dot(x, w3s[...], preferred_element_type=F32)
    hid = (h1 * (1.0 / (1.0 + jnp.exp(-h1))) * h3).astype(BF16)
    y = jnp.dot(hid, w2s[...], preferred_element_type=F32)
    first_visit = (s == 0) | (tile_ref[s] != tile_ref[prev])

    @pl.when(first_visit)
    def _():
        y_ref[...] = y

    @pl.when(jnp.logical_not(first_visit))
    def _():
        row = lax.broadcasted_iota(I32, (tm, 1), 0)
        y_ref[...] = jnp.where((row >= lo_ref[s]) & (row < hi_ref[s]), y, y_ref[...])

    @pl.when(s == n_steps - 1)
    def _():
        gathered(1 - slot).wait()


def _small_take(table, idx):
    n = table.shape[0]
    hit = idx[:, None] == jnp.arange(n, dtype=I32)[None, :]
    return jnp.sum(jnp.where(hit, table[None, :], 0), axis=1)


def _moe(h2, route, w1, w3, w2, layer):
    t, d = h2.shape
    tm = TM_MOE
    n_pairs = 2 * t
    n_tiles = n_pairs // tm
    n_steps = n_tiles + N_EXPERTS - 1
    ef = route[:, :2].astype(I32).T.reshape(-1)
    order = jnp.argsort(ef, stable=True).astype(I32)
    experts = jnp.arange(N_EXPERTS, dtype=I32)
    counts = jnp.sum((ef[:, None] == experts[None, :]).astype(I32), axis=0)
    cend = jnp.cumsum(counts)
    cstart = cend - counts
    first_row = jnp.arange(n_tiles, dtype=I32) * tm
    e_first = jnp.sum((cend[None, :] <= first_row[:, None]).astype(I32), axis=1)
    e_last = jnp.sum((cend[None, :] <= (first_row + tm - 1)[:, None]).astype(I32), axis=1)
    n_sub = e_last - e_first + 1
    base = jnp.cumsum(n_sub) - n_sub
    step = jnp.arange(n_steps, dtype=I32)
    tile_of = jnp.clip(jnp.sum((base[None, :] <= step[:, None]).astype(I32), axis=1) - 1, 0, n_tiles - 1)
    exp_of = jnp.clip(_small_take(e_first, tile_of) + step - _small_take(base, tile_of), 0, N_EXPERTS - 1)
    used = step < jnp.sum(n_sub)
    row0 = tile_of * tm
    lo = jnp.where(used, jnp.clip(_small_take(cstart, exp_of) - row0, 0, tm), 0)
    hi = jnp.where(used, jnp.clip(_small_take(cend, exp_of) - row0, 0, tm), 0)
    src = (order % t).reshape(n_tiles, 1, tm)

    def smem(f):
        return pl.BlockSpec((1, 1, tm), lambda s, tl, ex, lo_, hi_: (tl[f(s)], 0, 0), memory_space=pltpu.SMEM)

    wspec = lambda shape: pl.BlockSpec((None, None) + shape, lambda s, tl, ex, lo_, hi_: (layer, ex[s], 0, 0))
    grid_spec = pltpu.PrefetchScalarGridSpec(
        num_scalar_prefetch=4, grid=(n_steps,),
        in_specs=[smem(lambda s: s), smem(lambda s: jnp.minimum(s + 1, n_steps - 1)),
                  pl.BlockSpec(memory_space=pl.ANY),
                  wspec((d, D_EXPERT)), wspec((d, D_EXPERT)), wspec((D_EXPERT, d))],
        out_specs=pl.BlockSpec((tm, d), lambda s, tl, ex, lo_, hi_: (tl[s], 0)),
        scratch_shapes=[pltpu.VMEM((2, tm, d), F32), pltpu.SemaphoreType.DMA((2,)),
                        pltpu.VMEM((d, D_EXPERT), BF16), pltpu.VMEM((d, D_EXPERT), BF16),
                        pltpu.VMEM((D_EXPERT, d), BF16)])
    ys = pl.pallas_call(
        functools.partial(_moe_body, n_steps=n_steps),
        grid_spec=grid_spec, out_shape=_sds((n_pairs, d)),
        compiler_params=_params("arbitrary"), name="moe",
    )(tile_of, exp_of, lo, hi, src, src, h2, w1, w3, w2)
    return ys, jnp.argsort(order).astype(I32)


def _combine_body(posc_ref, posn_ref, x1_ref, r_ref, g2_ref, ys_ref, o_ref, ybuf, sem, *, n_tiles):
    tm = x1_ref.shape[0]
    i = pl.program_id(0)
    slot = i % 2

    def gather(idx_ref, b):
        for j in range(2 * tm):
            pltpu.make_async_copy(ys_ref.at[pl.ds(idx_ref[0, 0, j], 1)], ybuf.at[b, pl.ds(j, 1)], sem.at[b]).start()

    @pl.when(i == 0)
    def _():
        gather(posc_ref, 0)

    @pl.when(i + 1 < n_tiles)
    def _():
        gather(posn_ref, 1 - slot)

    pltpu.make_async_copy(ys_ref.at[pl.ds(0, 2 * tm)], ybuf.at[slot], sem.at[slot]).wait()
    r = r_ref[...]
    lane = lax.broadcasted_iota(I32, r.shape, 1)
    w1 = jnp.sum(jnp.where(lane == 2, r, 0.0), axis=-1, keepdims=True)
    w2 = jnp.sum(jnp.where(lane == 3, r, 0.0), axis=-1, keepdims=True)
    o_ref[...] = x1_ref[...] + g2_ref[...] * (w1 * ybuf[slot, :tm] + w2 * ybuf[slot, tm:])


def _combine(x1, ys, pos, route, mods, mod_row_of_tile):
    t, d = x1.shape
    tm = TM_COMBINE
    nt = t // tm
    tok = lambda i: (i, 0)
    idx = pos.reshape(2, nt, tm).transpose(1, 0, 2).reshape(nt, 1, 2 * tm)
    smem = lambda f: pl.BlockSpec((1, 1, 2 * tm), lambda i: (f(i), 0, 0), memory_space=pltpu.SMEM)
    return pl.pallas_call(
        functools.partial(_combine_body, n_tiles=nt),
        grid=(nt,),
        in_specs=[smem(lambda i: i), smem(lambda i: jnp.minimum(i + 1, nt - 1)),
                  pl.BlockSpec((tm, d), tok), pl.BlockSpec((tm, LANES), tok),
                  pl.BlockSpec((None, 1, d), lambda i: (mod_row_of_tile(i, tm), 0, 5)),
                  pl.BlockSpec(memory_space=pl.ANY)],
        out_specs=pl.BlockSpec((tm, d), tok), out_shape=_sds((t, d)),
        scratch_shapes=[pltpu.VMEM((2, 2 * tm, d), F32), pltpu.SemaphoreType.DMA((2,))],
        compiler_params=_params("arbitrary"), name="combine",
    )(idx, idx, x1, route, mods, ys)


def _rope_tables():
    t = jnp.arange(GRID_W * GRID_ROWS)
    pairs = HEAD_DIM // 4
    inv = 1.0 / (ROPE_THETA ** (jnp.arange(pairs, dtype=F32) * 2.0 / (HEAD_DIM // 2)))
    ang_r = (t // GRID_W).astype(F32)[:, None] * inv
    ang_c = (t % GRID_W).astype(F32)[:, None] * inv
    cos = jnp.concatenate([jnp.cos(ang_r)] * 2 + [jnp.cos(ang_c)] * 2, axis=1)
    sin = jnp.concatenate([-jnp.sin(ang_r), jnp.sin(ang_r), -jnp.sin(ang_c), jnp.sin(ang_c)], axis=1)
    return jnp.tile(cos, (1, 2)), jnp.tile(sin, (1, 2))


def _bias_pair_tiles(rpb_l):
    qc = jnp.arange(GRID_W)[:, None]
    kc = jnp.arange(GRID_W)[None, :]
    start_c = jnp.clip(qc - NA_COLS // 2, 0, GRID_W - NA_COLS)
    col_ok = (kc >= start_c) & (kc < start_c + NA_COLS)
    dc = jnp.clip(kc - qc + (NA_COLS - 1), 0, 2 * NA_COLS - 2)
    d = jnp.arange(-8, 23)
    rows = jnp.stack([rpb_l[:, min(max(dd, 0), 2 * NA_ROWS - 2)] for dd in range(-8, 23)], axis=1)
    onehot = (dc[None] == jnp.arange(2 * NA_COLS - 1)[:, None, None]).astype(F32)
    tile = jnp.einsum("hdx,xqk->hdqk", rows, onehot, precision=lax.Precision.HIGHEST)
    ok = col_ok[None, None] & ((d >= 0) & (d <= 2 * NA_ROWS - 2))[None, :, None, None]
    tile = jnp.where(ok, tile, NEG_INF).astype(F32)
    return jnp.concatenate([tile[:, :-1], tile[:, 1:]], axis=-1)


def kernel(x_prompt, x_sample, c, cache_a_k, cache_a_v, cache_b_k, cache_b_v, state_c_fwd, state_c_bwd, c_ctx, norm1_g, norm2_g, w_mod, b_mod, w_in, q_norm_a, k_norm_a, sink_a, q_norm_b, k_norm_b, rpb_b, w_gk_up_f, b_gk_f, w_gk_up_b, b_gk_b, gla_norm_g, w_out, w_group, w_router, w1, w3, w2):
    depth = w_in.shape[0]
    nb_c, l_c, d = x_prompt.shape
    nb_s, n_s, _ = x_sample.shape
    past = cache_a_k.shape[2]
    tm = TM_PROJ

    w_in_b = jnp.pad(w_in, ((0, 0), (0, 0), (0, N_IN_PAD - N_IN))).astype(BF16)
    w_out_b = w_out.astype(BF16)
    wr_b = jnp.pad(jnp.concatenate([w_group, w_router], axis=-1),
                   ((0, 0), (0, 0), (0, LANES - N_GROUPS - N_EXPERTS))).astype(BF16)
    wgk = jnp.zeros((depth, 256, 1024), F32)
    wgk = wgk.at[:, :C_RANK, :512].set(w_gk_up_f).at[:, C_RANK:2 * C_RANK, 512:].set(w_gk_up_b).astype(BF16)
    cos, sin = _rope_tables()
    head_id = jnp.arange(512) // HEAD_DIM
    consts = dict(
        bd=(head_id[:, None] == head_id[None, :]).astype(BF16),
        gqa=jnp.tile(q_norm_a, (1, 8)).reshape(depth, 1, 512),
        gka=jnp.tile(k_norm_a, (1, 2)).reshape(depth, 1, LANES),
        gqb=jnp.tile(q_norm_b, (1, 8)).reshape(depth, 1, 512),
        gkb=jnp.tile(k_norm_b, (1, 8)).reshape(depth, 1, 512),
        cos=cos, sin=sin, wgk=wgk,
        bgk=jnp.concatenate([b_gk_f, b_gk_b], axis=-1).reshape(depth, 1, 1024))
    n1 = norm1_g.reshape(depth, 1, d)
    n2 = norm2_g.reshape(depth, 1, d)
    gn = gla_norm_g.reshape(depth, 1, C_DV)
    cond = jnp.zeros((16, d), F32).at[0].set(c_ctx).at[1:1 + nb_s].set(c)
    ctx_row = lambda i, tile: 0
    lat_row = lambda i, tile: 1 + i // (n_s // tile)
    cak = cache_a_k.reshape(nb_s, depth, 1, past, LANES).astype(BF16)
    cav = cache_a_v.reshape(nb_s, depth, 1, past, LANES).astype(BF16)
    cbk = cache_b_k.reshape(nb_s, depth, past, 4, LANES).transpose(0, 1, 3, 2, 4).astype(BF16)
    cbv = cache_b_v.reshape(nb_s, depth, past, 4, LANES).transpose(0, 1, 3, 2, 4).astype(BF16)

    xp = x_prompt.reshape(nb_c * l_c, d)
    xs = x_sample.reshape(nb_s * n_s, d)
    kv_ctx = st_ctx = None
    for l in range(depth):
        mods = _adaln(cond, w_mod, b_mod, l).reshape(16, 1, 6 * d)
        tp = _bias_pair_tiles(rpb_b[l])

        aq, ak, av, bq, bk, bv, cq, ck, cv, cg, la, *kv_ctx = _inproj(
            xp, mods, ctx_row, n1, w_in_b, consts, l, rope=False, tm=TM_PROJ // 2, seq_len=l_c, prev=kv_ctx)
        oa = _attention("a_ctx", aq, ak.reshape(1, -1, LANES), av.reshape(1, -1, LANES), l_c, sink=sink_a, layer=l)
        ob = _attention("b_ctx", bq, bk, bv, l_c)
        oc, *st_ctx = _gla(cq, ck, cv, cg, la, gn, l_c, l, emit_state=True, prev=st_ctx)
        x1, h2, route = _outproj(oa, ob, oc, xp, mods, ctx_row, n2, w_out_b, wr_b, l)
        xp = _combine(x1, *_moe(h2, route, w1, w3, w2, l), route, mods, ctx_row)

        aq, ak, av, bq, bk, bv, cq, ck, cv, cg, la = _inproj(
            xs, mods, lat_row, n1, w_in_b, consts, l, rope=True, tm=TM_PROJ)
        oa = _attention("a_lat", aq, ak.reshape(1, -1, LANES), av.reshape(1, -1, LANES), n_s,
                        prefix=(cak[:, l], cav[:, l]), sink=sink_a, layer=l)
        ob = _attention("b_lat", bq, bk, bv, n_s, prefix=(cbk[:, l], cbv[:, l]), tp=tp)
        oc, = _gla(cq, ck, cv, cg, la, gn, n_s, l, s0=(state_c_fwd, state_c_bwd), emit_state=False)
        x1, h2, route = _outproj(oa, ob, oc, xs, mods, lat_row, n2, w_out_b, wr_b, l)
        xs = _combine(x1, *_moe(h2, route, w1, w3, w2, l), route, mods, lat_row)

    new_kv = [a.reshape(nb_c, depth, l_c, -1, HEAD_DIM) for a in kv_ctx]
    return (xp.reshape(nb_c, l_c, d), xs.reshape(nb_s, n_s, d), *new_kv, *st_ctx)
```

```python
import functools

import jax
import jax.numpy as jnp
from jax import lax
from jax.experimental import pallas as pl
from jax.experimental.pallas import tpu as pltpu

F32 = jnp.float32
BF16 = jnp.bfloat16
I32 = jnp.int32

D_MODEL = 2048
HEAD_DIM = 64
EPS = 1e-6
NEG_INF = -1e30
ROPE_THETA = 10000.0
GRID_W = 64
GRID_ROWS = 32
A_WINDOW = 128
NA_ROWS = 8
NA_COLS = 16
C_HEADS = 4
C_DK = 128
C_DV = 256
C_RANK = 16
C_CHUNK = 64
GLA_BLOCK = 256
N_GROUPS = 4
E_PER_GROUP = 4
N_EXPERTS = 16
D_EXPERT = 512
LANES = 128
N_IN = 5408
N_IN_PAD = 5632
COL_AQ, COL_AKV, COL_BQ, COL_BK, COL_BV = 0, 512, 768, 1280, 1792
COL_CQ, COL_CK, COL_CV, COL_CG, COL_CR = 2304, 2816, 3328, 4352, 5376
VMEM_LIMIT = 56 * 1024 * 1024
TM_PROJ = 512
TM_MOE = 256
TM_COMBINE = 256

_TRANS_B = (((1,), (1,)), ((), ()))
_TRANS_A = (((0,), (0,)), ((), ()))


def _sds(shape, dtype=F32):
    return jax.ShapeDtypeStruct(shape, dtype)


def _params(*sem):
    return pltpu.CompilerParams(dimension_semantics=sem, vmem_limit_bytes=VMEM_LIMIT)


def _resident(shape, index_map):
    return pl.BlockSpec(shape, index_map, pipeline_mode=pl.Buffered(1))


def _adaln_body(c_ref, w_ref, b_ref, o_ref):
    c = c_ref[...]
    s = c * (1.0 / (1.0 + jnp.exp(-c)))
    o_ref[...] = jnp.dot(s.astype(BF16), w_ref[...].astype(BF16), preferred_element_type=F32) + b_ref[...]


def _adaln(cond, w_mod, b_mod, layer):
    d, n = w_mod.shape[1], w_mod.shape[2]
    tn = 2048
    return pl.pallas_call(
        _adaln_body,
        grid=(n // tn,),
        in_specs=[pl.BlockSpec((16, d), lambda j: (0, 0)),
                  pl.BlockSpec((None, d, tn), lambda j: (layer, 0, j)),
                  pl.BlockSpec((None, 1, tn), lambda j: (layer, 0, j))],
        out_specs=pl.BlockSpec((16, tn), lambda j: (0, j)),
        out_shape=_sds((16, n)),
        compiler_params=_params("arbitrary"),
        name="adaln",
    )(cond, w_mod, b_mod.reshape(b_mod.shape[0], 1, n))


def _head_norm(z, gain, bd):
    w = z.shape[1]
    ss = jnp.dot((z * z).astype(BF16), bd[:w, :w], preferred_element_type=F32)
    return z * lax.rsqrt(ss * (1.0 / HEAD_DIM) + EPS) * gain


def _rope(z, cos, sin_signed, lane):
    lower = (lane % 32) < 16
    partner = jnp.where(lower, pltpu.roll(z, LANES - 16, 1), pltpu.roll(z, 16, 1))
    return z * cos + partner * sin_signed


def _inproj_body(x_ref, sh_ref, sc_ref, n1_ref, w_ref, bd_ref, gqa_ref, gka_ref, gqb_ref, gkb_ref,
                 cos_ref, sin_ref, wgk_ref, bgk_ref,
                 *rest, rope, n_prev, seq_len):
    prev_refs = rest[:4] if n_prev else ()
    rest = rest[4 if n_prev else 0:]
    aq_ref, ak_ref, av_ref, bq_ref, bk_ref, bv_ref, cq_ref, ck_ref, cv_ref, cg_ref, la_ref = rest[:11]
    stack_refs = rest[11:]

    def emit(idx, val):
        if not seq_len:
            return
        ref = stack_refs[idx]
        if n_prev:
            ref[:, :n_prev] = prev_refs[idx][...]
        ref[:, n_prev] = val.reshape(val.shape[0] // seq_len, seq_len, val.shape[1])

    x = x_ref[...]
    h = x * lax.rsqrt(jnp.mean(x * x, axis=-1, keepdims=True) + EPS) * n1_ref[...]
    h = h * (1.0 + sc_ref[...]) + sh_ref[...]
    hb = h.astype(BF16)
    bd = bd_ref[...]
    lane = lax.broadcasted_iota(I32, (x.shape[0], LANES), 1)
    upper = lane >= HEAD_DIM

    def proj(c0, n):
        return jnp.dot(hb, w_ref[:, c0:c0 + n], preferred_element_type=F32)

    def rot(s):
        return _rope(s, cos_ref[...], sin_ref[...], lane) if rope else s

    z = _head_norm(proj(COL_AQ, 512), gqa_ref[...], bd)
    for p in range(4):
        s = rot(z[:, LANES * p:LANES * (p + 1)]) * (HEAD_DIM ** -0.5)
        r = pltpu.roll(s, HEAD_DIM, 1)
        if p // 2 == 0:
            e0, e1 = jnp.where(upper, 0.0, s), jnp.where(upper, 0.0, r)
        else:
            e0, e1 = jnp.where(upper, r, 0.0), jnp.where(upper, s, 0.0)
        aq_ref[2 * p] = e0.astype(BF16)
        aq_ref[2 * p + 1] = e1.astype(BF16)
    z = proj(COL_AKV, 256)
    k = rot(_head_norm(z[:, :LANES], gka_ref[...], bd))
    ak_ref[...] = k.astype(BF16)
    av_ref[...] = z[:, LANES:].astype(BF16)
    emit(0, k)
    emit(1, z[:, LANES:])
    z = _head_norm(proj(COL_BQ, 512), gqb_ref[...], bd) * (HEAD_DIM ** -0.5)
    for p in range(4):
        s = z[:, LANES * p:LANES * (p + 1)]
        bq_ref[2 * p] = jnp.where(upper, 0.0, s).astype(BF16)
        bq_ref[2 * p + 1] = jnp.where(upper, s, 0.0).astype(BF16)
    z = _head_norm(proj(COL_BK, 512), gkb_ref[...], bd)
    for p in range(4):
        bk_ref[p] = z[:, LANES * p:LANES * (p + 1)].astype(BF16)
    emit(2, z)
    z = proj(COL_BV, 512)
    for p in range(4):
        bv_ref[p] = z[:, LANES * p:LANES * (p + 1)].astype(BF16)
    emit(3, z)
    cq_ref[...] = (proj(COL_CQ, 512) * (C_DK ** -0.5)).astype(BF16)
    ck_ref[...] = proj(COL_CK, 512).astype(BF16)
    cv_ref[...] = proj(COL_CV, 1024).astype(BF16)
    cg_ref[...] = proj(COL_CG, 1024).astype(BF16)
    pre = jnp.dot(proj(COL_CR, 256).astype(BF16), wgk_ref[...], preferred_element_type=F32) + bgk_ref[...]
    la_ref[...] = (jnp.minimum(pre, 0.0) - jnp.log(1.0 + jnp.exp(-jnp.abs(pre)))) * (1.0 / 16.0)


def _inproj(x, mods, mod_row_of_tile, n1, w_in_b, consts, layer, *, rope, tm, seq_len=0, prev=None):
    t, d = x.shape
    n_prev = 0 if prev is None else prev[0].shape[1]
    seq_tiles = (GRID_W * GRID_ROWS) // tm
    tok = lambda i: (i, 0)
    full = lambda i: (0, 0)
    lay = lambda i: (layer, 0, 0)
    in_specs = [
        pl.BlockSpec((tm, d), tok),
        pl.BlockSpec((None, 1, d), lambda i: (mod_row_of_tile(i, tm), 0, 0)),
        pl.BlockSpec((None, 1, d), lambda i: (mod_row_of_tile(i, tm), 0, 1)),
        pl.BlockSpec((None, 1, d), lay),
        _resident((None, d, N_IN_PAD), lay),
        _resident((512, 512), full),
        pl.BlockSpec((None, 1, 512), lay), pl.BlockSpec((None, 1, LANES), lay),
        pl.BlockSpec((None, 1, 512), lay), pl.BlockSpec((None, 1, 512), lay),
        pl.BlockSpec((tm, LANES), lambda i: (i % seq_tiles, 0)),
        pl.BlockSpec((tm, LANES), lambda i: (i % seq_tiles, 0)),
        _resident((None, 256, 1024), lay),
        pl.BlockSpec((None, 1, 1024), lay),
    ]
    slab8 = pl.BlockSpec((8, tm, LANES), lambda i: (0, i, 0))
    slab4 = pl.BlockSpec((4, tm, LANES), lambda i: (0, i, 0))
    out_specs = [slab8, pl.BlockSpec((tm, LANES), tok), pl.BlockSpec((tm, LANES), tok),
                 slab8, slab4, slab4,
                 pl.BlockSpec((tm, 512), tok), pl.BlockSpec((tm, 512), tok),
                 pl.BlockSpec((tm, 1024), tok), pl.BlockSpec((tm, 1024), tok), pl.BlockSpec((tm, 1024), tok)]
    out_shape = [_sds((8, t, LANES), BF16), _sds((t, LANES), BF16), _sds((t, LANES), BF16),
                 _sds((8, t, LANES), BF16), _sds((4, t, LANES), BF16), _sds((4, t, LANES), BF16),
                 _sds((t, 512), BF16), _sds((t, 512), BF16), _sds((t, 1024), BF16), _sds((t, 1024), BF16),
                 _sds((t, 1024), F32)]
    args = [x, mods, mods, n1, w_in_b, consts["bd"], consts["gqa"], consts["gka"], consts["gqb"], consts["gkb"],
            consts["cos"], consts["sin"], consts["wgk"], consts["bgk"]]
    if seq_len:
        nseq = tm // seq_len
        for j, width in enumerate((LANES, LANES, 512, 512)):
            if n_prev:
                in_specs.append(pl.BlockSpec((nseq, n_prev, seq_len, width), lambda i: (i, 0, 0, 0)))
                args.append(prev[j])
            out_specs.append(pl.BlockSpec((nseq, n_prev + 1, seq_len, width), lambda i: (i, 0, 0, 0)))
            out_shape.append(_sds((t // seq_len, n_prev + 1, seq_len, width)))
    return pl.pallas_call(
        functools.partial(_inproj_body, rope=rope, n_prev=n_prev, seq_len=seq_len),
        grid=(t // tm,), in_specs=in_specs, out_specs=out_specs, out_shape=out_shape,
        compiler_params=_params("arbitrary"), name="inproj",
    )(*args)


def _attn_body(*refs, kind, tq, kwin, n_seq, layer):
    is_a = kind in ("a_ctx", "a_lat")
    prefix = kind in ("a_lat", "b_lat")
    it = iter(refs)
    q_ref, k_ref, v_ref = next(it), next(it), next(it)
    kc_ref = vc_ref = sink_ref = tp_ref = None
    if prefix:
        kc_ref, vc_ref = next(it), next(it)
    if is_a:
        sink_ref = next(it)
    if kind == "b_lat":
        tp_ref = next(it)
    o_ref = next(it)
    i = pl.program_id(1)

    group = 4 if is_a else 1
    if kind == "a_lat":
        ws = jnp.clip(i * tq - A_WINDOW, 0, n_seq - kwin)
        ws = pl.multiple_of(ws, LANES)
        qpos = i * tq + lax.broadcasted_iota(I32, (group * tq, kwin), 0) % tq
        kpos = ws + lax.broadcasted_iota(I32, (group * tq, kwin), 1)
        allowed = jnp.abs(qpos - kpos) <= A_WINDOW
    elif kind == "b_lat":
        r0 = i * (tq // GRID_W)
        k0 = jnp.clip(r0 - NA_ROWS // 2, 0, GRID_ROWS - kwin // GRID_W)
        ws = pl.multiple_of(k0 * GRID_W, GRID_W)
        qrow = r0 + lax.broadcasted_iota(I32, (tq, kwin), 0) // GRID_W
        krow = k0 + lax.broadcasted_iota(I32, (tq, kwin), 1) // GRID_W
        start = jnp.clip(qrow - NA_ROWS // 2, 0, GRID_ROWS - NA_ROWS)
        allowed = (krow >= start) & (krow < start + NA_ROWS)
        row_mask = jnp.where(allowed, 0.0, NEG_INF)
    else:
        ws = 0
    lane = lax.broadcasted_iota(I32, (tq, LANES), 1)
    upper = lane >= HEAD_DIM

    def attend(q, ks, h, sink):
        kl = k_ref[ks, pl.ds(ws, kwin), :].astype(BF16)
        vl = v_ref[ks, pl.ds(ws, kwin), :].astype(BF16)
        s = lax.dot_general(q, kl, _TRANS_B, preferred_element_type=F32)
        if kind == "a_lat":
            s = jnp.where(allowed, s, NEG_INF)
        elif kind == "b_lat":
            rows = []
            for qr in range(tq // GRID_W):
                u0 = k0 - r0 - qr + 15
                rows.append(jnp.concatenate([tp_ref[h, u0 + 2 * m] for m in range(kwin // LANES)], axis=1))
            s = s + jnp.concatenate(rows, axis=0) + row_mask
        m = jnp.max(s, axis=-1, keepdims=True)
        if prefix:
            sc = lax.dot_general(q, kc_ref[ks], _TRANS_B, preferred_element_type=F32)
            m = jnp.maximum(m, jnp.max(sc, axis=-1, keepdims=True))
        if is_a:
            m = jnp.maximum(m, sink)
        e = jnp.exp(s - m)
        den = jnp.sum(e, axis=-1, keepdims=True)
        o = jnp.dot(e.astype(BF16), vl, preferred_element_type=F32)
        if prefix:
            ec = jnp.exp(sc - m)
            den = den + jnp.sum(ec, axis=-1, keepdims=True)
            o = o + jnp.dot(ec.astype(BF16), vc_ref[ks], preferred_element_type=F32)
        if is_a:
            den = den + jnp.exp(sink - m)
        return o * (1.0 / den)

    if is_a:
        head_of_row = lax.broadcasted_iota(I32, (group * tq, 1), 0) // tq
        for g in range(2):
            q = jnp.concatenate([q_ref[group * g + j] for j in range(group)], axis=0)
            sink = jnp.zeros((group * tq, 1), F32)
            for j in range(group):
                sink = jnp.where(head_of_row == j, sink_ref[layer, group * g + j], sink)
            o = attend(q, 0, None, sink)
            for pp in range(2):
                o0 = o[(2 * pp) * tq:(2 * pp + 1) * tq]
                o1 = o[(2 * pp + 1) * tq:(2 * pp + 2) * tq]
                if g == 0:
                    o1 = pltpu.roll(o1, HEAD_DIM, 1)
                else:
                    o0 = pltpu.roll(o0, HEAD_DIM, 1)
                o_ref[2 * g + pp] = jnp.where(upper, o1, o0).astype(o_ref.dtype)
    else:
        def pair(p):
            o0 = attend(q_ref[2 * p], p, 2 * p, None)
            o1 = attend(q_ref[2 * p + 1], p, 2 * p + 1, None)
            o_ref[p] = jnp.where(upper, o1, o0).astype(o_ref.dtype)

        if kind == "b_lat":
            def body(p, c):
                pair(p)
                return c
            lax.fori_loop(0, 4, body, 0, unroll=4)
        else:
            for p in range(4):
                pair(p)


def _attention(kind, q, k, v, n_seq, *, prefix=None, sink=None, tp=None, layer=0):
    t = q.shape[1]
    nb = t // n_seq
    s_k = k.shape[0]
    tq, kwin = {"a_ctx": (n_seq, n_seq), "b_ctx": (n_seq, n_seq),
                "a_lat": (A_WINDOW, 3 * A_WINDOW), "b_lat": (256, 768)}[kind]
    nq = n_seq // tq
    in_specs = [pl.BlockSpec((8, tq, LANES), lambda b, i: (0, b * nq + i, 0)),
                pl.BlockSpec((s_k, n_seq, LANES), lambda b, i: (0, b, 0)),
                pl.BlockSpec((s_k, n_seq, LANES), lambda b, i: (0, b, 0))]
    args = [q, k, v]
    if prefix is not None:
        kc, vc = prefix
        spec = pl.BlockSpec((None,) + kc.shape[1:], lambda b, i: (b, 0, 0, 0))
        in_specs += [spec, spec]
        args += [kc, vc]
    if sink is not None:
        in_specs.append(pl.BlockSpec(memory_space=pltpu.SMEM))
        args.append(sink)
    if tp is not None:
        in_specs.append(_resident(tp.shape, lambda b, i: (0, 0, 0, 0)))
        args.append(tp)
    return pl.pallas_call(
        functools.partial(_attn_body, kind=kind, tq=tq, kwin=kwin, n_seq=n_seq, layer=layer),
        grid=(nb, nq), in_specs=in_specs,
        out_specs=pl.BlockSpec((4, tq, LANES), lambda b, i: (0, b * nq + i, 0)),
        out_shape=_sds((4, t, LANES), BF16),
        compiler_params=_params("arbitrary", "arbitrary"), name="attn_" + kind,
    )(*args)


def _gla_body(*refs, n_seq, init, emit_state, n_prev, hps):
    it = iter(refs)
    q, k, v, g, laf, lab, gn_ref = (next(it) for _ in range(7))
    s0f = s0b = sf = sb = pf = pb = None
    if init:
        s0f, s0b = next(it), next(it)
    if n_prev:
        pf, pb = next(it), next(it)
    o = next(it)
    if emit_state:
        sf, sb = next(it), next(it)
    scratch = [next(it) for _ in range(7)]
    for h in range(hps):
        dk = (slice(None), pl.ds(h * C_DK, C_DK))
        dv = (slice(None), pl.ds(h * C_DV, C_DV))
        pick = lambda r, idx: None if r is None else r.at[idx]
        _gla_head(q.at[dk], k.at[dk], v.at[dv], g.at[dv], laf.at[dk], lab.at[dk], gn_ref,
                  pick(s0f, h), pick(s0b, h), pick(pf, (slice(None), h)), pick(pb, (slice(None), h)),
                  o.at[dv], pick(sf, (slice(None), h)), pick(sb, (slice(None), h)),
                  *[r.at[h] for r in scratch],
                  n_seq=n_seq, init=init, emit_state=emit_state, n_prev=n_prev)


def _gla_head(q_ref, k_ref, v_ref, g_ref, laf_ref, lab_ref, gn_ref, s0f_ref, s0b_ref, pf_ref, pb_ref,
              o_ref, sf_ref, sb_ref, acc_ref, qdf_ref, qdb_ref, dsf_ref, dsb_ref, decf_ref, decb_ref,
              *, n_seq, init, emit_state, n_prev):
    nc = n_seq // C_CHUNK
    per = GLA_BLOCK // C_CHUNK
    row = lax.broadcasted_iota(I32, (GLA_BLOCK, C_DK), 0) % C_CHUNK
    ri = lax.broadcasted_iota(I32, (GLA_BLOCK, GLA_BLOCK), 0)
    ci = lax.broadcasted_iota(I32, (GLA_BLOCK, GLA_BLOCK), 1)
    same = (ri // C_CHUNK) == (ci // C_CHUNK)
    zero_chunk = jnp.zeros((C_CHUNK, C_DK), BF16)

    def decay_sums(la, fwd):
        x = la
        for s in (1, 2, 4, 8, 16, 32):
            if fwd:
                x = x + jnp.where(row >= s, pltpu.roll(x, s, 0), 0.0)
            else:
                x = x + jnp.where(row < C_CHUNK - s, pltpu.roll(x, GLA_BLOCK - s, 0), 0.0)
        return x

    def block(u, carry):
        sl = pl.ds(pl.multiple_of(u * GLA_BLOCK, GLA_BLOCK), GLA_BLOCK)
        q = q_ref[sl, :].astype(F32)
        k = k_ref[sl, :].astype(F32)
        v = v_ref[sl, :]
        att = None
        rhs = []
        for fwd, la_ref, qd_ref, dec_ref in ((True, laf_ref, qdf_ref, decf_ref), (False, lab_ref, qdb_ref, decb_ref)):
            b = decay_sums(la_ref[sl, :], fwd)
            b3 = b.reshape(per, C_CHUNK, C_DK)
            bl3 = b3[:, C_CHUNK - 1:C_CHUNK, :] if fwd else b3[:, 0:1, :]
            qd = (q * jnp.exp(b)).astype(BF16)
            ki = (k * jnp.exp(-b)).astype(BF16)
            ke3 = (k.reshape(per, C_CHUNK, C_DK) * jnp.exp(bl3 - b3)).astype(BF16)
            qd_ref[sl, :] = qd
            dec = jnp.exp(bl3)
            for j in range(per):
                dec_ref[u * per + j, 0:1, :] = dec[j]
            a = lax.dot_general(qd, ki, _TRANS_B, preferred_element_type=F32)
            a = jnp.where(same & ((ci <= ri) if fwd else (ci >= ri)), a, 0.0)
            att = a if att is None else att + a
            rhs += [jnp.concatenate([ke3[i] if i == j else zero_chunk for i in range(per)], axis=0)
                    for j in range(per)]
        acc_ref[sl, :] = jnp.dot(att.astype(BF16), v, preferred_element_type=F32)
        ds = lax.dot_general(v, jnp.concatenate(rhs, axis=1), _TRANS_A, preferred_element_type=F32)
        for j in range(per):
            dsf_ref[u * per + j] = ds[:, C_DK * j:C_DK * (j + 1)]
            dsb_ref[u * per + j] = ds[:, C_DK * (per + j):C_DK * (per + j + 1)]
        return carry

    lax.fori_loop(0, n_seq // GLA_BLOCK, block, 0, unroll=min(8, n_seq // GLA_BLOCK))

    def states(ds_ref, dec_ref, s0_ref, s_out_ref, prev_ref, fwd):
        def chunk(j, st):
            c = j if fwd else nc - 1 - j
            inc = ds_ref[c]
            ds_ref[c] = st
            return st * dec_ref[c, 0:1, :] + inc

        st = lax.fori_loop(0, nc, chunk, s0_ref[...].T if init else jnp.zeros((C_DV, C_DK), F32), unroll=4)
        if emit_state:
            if n_prev:
                s_out_ref[:n_prev] = prev_ref[...]
            s_out_ref[n_prev] = st.T

    states(dsf_ref, decf_ref, s0f_ref, sf_ref, pf_ref, True)
    states(dsb_ref, decb_ref, s0b_ref, sb_ref, pb_ref, False)

    def finish(u, carry):
        sl = pl.ds(pl.multiple_of(u * GLA_BLOCK, GLA_BLOCK), GLA_BLOCK)
        inter = []
        for j in range(per):
            c = u * per + j
            rows = pl.ds(pl.multiple_of(c * C_CHUNK, C_CHUNK), C_CHUNK)
            inter.append(
                lax.dot_general(qdf_ref[rows, :], dsf_ref[c].astype(BF16), _TRANS_B, preferred_element_type=F32)
                + lax.dot_general(qdb_ref[rows, :], dsb_ref[c].astype(BF16), _TRANS_B, preferred_element_type=F32))
        tot = acc_ref[sl, :] + jnp.concatenate(inter, axis=0)
        y = tot * lax.rsqrt(jnp.mean(tot * tot, axis=-1, keepdims=True) + EPS) * gn_ref[...]
        g = g_ref[sl, :].astype(F32)
        o_ref[sl, :] = (y * (g * (1.0 / (1.0 + jnp.exp(-g))))).astype(o_ref.dtype)
        return carry

    lax.fori_loop(0, n_seq // GLA_BLOCK, finish, 0, unroll=min(8, n_seq // GLA_BLOCK))


def _gla(cq, ck, cv, cg, la, gn, n_seq, layer, *, s0=None, emit_state, prev=None):
    t = cq.shape[0]
    nb = t // n_seq
    nc = n_seq // C_CHUNK
    hps = C_HEADS if n_seq <= GLA_BLOCK else 1
    tok = lambda b, h: (b, h)
    wk, wv = hps * C_DK, hps * C_DV
    in_specs = [pl.BlockSpec((n_seq, wk), tok), pl.BlockSpec((n_seq, wk), tok),
                pl.BlockSpec((n_seq, wv), tok), pl.BlockSpec((n_seq, wv), tok),
                pl.BlockSpec((n_seq, wk), tok), pl.BlockSpec((n_seq, wk), lambda b, h: (b, C_HEADS // hps + h)),
                pl.BlockSpec((None, 1, C_DV), lambda b, h: (layer, 0, 0))]
    args = [cq, ck, cv, cg, la, la, gn]
    if s0 is not None:
        spec = pl.BlockSpec((None, None, hps, C_DK, C_DV), lambda b, h: (b, layer, h, 0, 0))
        in_specs += [spec, spec]
        args += list(s0)
    n_prev = 0 if prev is None else prev[0].shape[1]
    if n_prev:
        spec = pl.BlockSpec((None, n_prev, hps, C_DK, C_DV), lambda b, h: (b, 0, h, 0, 0))
        in_specs += [spec, spec]
        args += list(prev)
    out_specs = [pl.BlockSpec((n_seq, wv), tok)]
    out_shape = [_sds((t, C_HEADS * C_DV), BF16)]
    if emit_state:
        spec = pl.BlockSpec((None, n_prev + 1, hps, C_DK, C_DV), lambda b, h: (b, 0, h, 0, 0))
        out_specs += [spec, spec]
        out_shape += [_sds((nb, n_prev + 1, C_HEADS, C_DK, C_DV))] * 2
    return pl.pallas_call(
        functools.partial(_gla_body, n_seq=n_seq, init=s0 is not None, emit_state=emit_state, n_prev=n_prev,
                          hps=hps),
        grid=(nb, C_HEADS // hps), in_specs=in_specs, out_specs=out_specs, out_shape=out_shape,
        scratch_shapes=[pltpu.VMEM((hps, n_seq, C_DV), F32),
                        pltpu.VMEM((hps, n_seq, C_DK), BF16), pltpu.VMEM((hps, n_seq, C_DK), BF16),
                        pltpu.VMEM((hps, nc, C_DV, C_DK), F32), pltpu.VMEM((hps, nc, C_DV, C_DK), F32),
                        pltpu.VMEM((hps, nc, 8, C_DK), F32), pltpu.VMEM((hps, nc, 8, C_DK), F32)],
        compiler_params=_params("arbitrary", "arbitrary"), name="gla",
    )(*args)


def _outproj_body(oa_ref, ob_ref, oc_ref, x_ref, g1_ref, sh_ref, sc_ref, n2_ref, w_ref, wr_ref,
                  x1_ref, h2_ref, r_ref):
    mix = jnp.concatenate([oa_ref[p] for p in range(4)] + [ob_ref[p] for p in range(4)] + [oc_ref[...]], axis=1)
    y = jnp.dot(mix, w_ref[...], preferred_element_type=F32)
    x1 = x_ref[...] + g1_ref[...] * y
    x1_ref[...] = x1
    h = x1 * lax.rsqrt(jnp.mean(x1 * x1, axis=-1, keepdims=True) + EPS) * n2_ref[...]
    h = h * (1.0 + sc_ref[...]) + sh_ref[...]
    h2_ref[...] = h
    lg = jnp.dot(h.astype(BF16), wr_ref[...], preferred_element_type=F32)
    lane = lax.broadcasted_iota(I32, lg.shape, 1)
    big = jnp.int32(LANES)
    is_g = lane < N_GROUPS
    gmax = jnp.max(jnp.where(is_g, lg, -jnp.inf), axis=-1, keepdims=True)
    gidx = jnp.min(jnp.where(is_g & (lg == gmax), lane, big), axis=-1, keepdims=True)
    gw = 1.0 / jnp.sum(jnp.where(is_g, jnp.exp(lg - gmax), 0.0), axis=-1, keepdims=True)
    lo = N_GROUPS + E_PER_GROUP * gidx
    sel = (lane >= lo) & (lane < lo + E_PER_GROUP)
    v1 = jnp.max(jnp.where(sel, lg, -jnp.inf), axis=-1, keepdims=True)
    i1 = jnp.min(jnp.where(sel & (lg == v1), lane, big), axis=-1, keepdims=True)
    sel2 = sel & (lane != i1)
    v2 = jnp.max(jnp.where(sel2, lg, -jnp.inf), axis=-1, keepdims=True)
    i2 = jnp.min(jnp.where(sel2 & (lg == v2), lane, big), axis=-1, keepdims=True)
    e21 = jnp.exp(v2 - v1)
    w1 = gw / (1.0 + e21)
    w2 = gw * e21 / (1.0 + e21)
    out = jnp.where(lane == 0, (i1 - N_GROUPS).astype(F32),
                    jnp.where(lane == 1, (i2 - N_GROUPS).astype(F32),
                              jnp.where(lane == 2, w1, jnp.where(lane == 3, w2, 0.0))))
    r_ref[...] = out


def _outproj(oa, ob, oc, x, mods, mod_row_of_tile, n2, w_out_b, wr_b, layer):
    t, d = x.shape
    tm = TM_PROJ
    tok = lambda i: (i, 0)
    lay = lambda i: (layer, 0, 0)
    mod = lambda k: pl.BlockSpec((None, 1, d), lambda i: (mod_row_of_tile(i, tm), 0, k))
    slab4 = pl.BlockSpec((4, tm, LANES), lambda i: (0, i, 0))
    return pl.pallas_call(
        _outproj_body,
        grid=(t // tm,),
        in_specs=[slab4, slab4, pl.BlockSpec((tm, 1024), tok), pl.BlockSpec((tm, d), tok),
                  mod(2), mod(3), mod(4), pl.BlockSpec((None, 1, d), lay),
                  _resident((None, d, d), lay), _resident((None, d, LANES), lay)],
        out_specs=[pl.BlockSpec((tm, d), tok), pl.BlockSpec((tm, d), tok), pl.BlockSpec((tm, LANES), tok)],
        out_shape=[_sds((t, d)), _sds((t, d)), _sds((t, LANES))],
        compiler_params=_params("arbitrary"), name="outproj",
    )(oa, ob, oc, x, mods, mods, mods, n2, w_out_b, wr_b)


def _moe_body(tile_ref, exp_ref, lo_ref, hi_ref, src0_ref, srcn_ref, h_ref, w1_ref, w3_ref, w2_ref,
              y_ref, xbuf, gsem, w1s, w3s, w2s, *, n_steps):
    tm = xbuf.shape[1]
    s = pl.program_id(0)
    slot = s % 2
    prev = jnp.maximum(s - 1, 0)

    def gather(idx_ref, b):
        for j in range(tm):
            pltpu.make_async_copy(h_ref.at[pl.ds(idx_ref[0, 0, j], 1)], xbuf.at[b, pl.ds(j, 1)],
                                  gsem.at[b]).start(priority=j % 2)

    def gathered(b):
        return pltpu.make_async_copy(h_ref.at[pl.ds(0, tm)], xbuf.at[b], gsem.at[b])

    @pl.when(s == 0)
    def _():
        gather(src0_ref, 0)

    @pl.when((s == 0) | (exp_ref[s] != exp_ref[prev]))
    def _():
        w1s[...] = w1_ref[...].astype(BF16)
        w3s[...] = w3_ref[...].astype(BF16)
        w2s[...] = w2_ref[...].astype(BF16)

    gathered(slot).wait()
    gather(srcn_ref, 1 - slot)
    x = xbuf[slot].astype(BF16)
    h1 = jnp.dot(x, w1s[...], preferred_element_type=F32)
    h3 = jnp.dot(x, w3s[...], preferred_element_type=F32)
    hid = (h1 * (1.0 / (1.0 + jnp.exp(-h1))) * h3).astype(BF16)
    y = jnp.dot(hid, w2s[...], preferred_element_type=F32)
    first_visit = (s == 0) | (tile_ref[s] != tile_ref[prev])

    @pl.when(first_visit)
    def _():
        y_ref[...] = y

    @pl.when(jnp.logical_not(first_visit))
    def _():
        row = lax.broadcasted_iota(I32, (tm, 1), 0)
        y_ref[...] = jnp.where((row >= lo_ref[s]) & (row < hi_ref[s]), y, y_ref[...])

    @pl.when(s == n_steps - 1)
    def _():
        gathered(1 - slot).wait()


def _small_take(table, idx):
    n = table.shape[0]
    hit = idx[:, None] == jnp.arange(n, dtype=I32)[None, :]
    return jnp.sum(jnp.where(hit, table[None, :], 0), axis=1)


def _moe(h2, route, w1, w3, w2, layer):
    t, d = h2.shape
    tm = TM_MOE
    n_pairs = 2 * t
    n_tiles = n_pairs // tm
    n_steps = n_tiles + N_EXPERTS - 1
    ef = route[:, :2].astype(I32).T.reshape(-1)
    order = jnp.argsort(ef, stable=True).astype(I32)
    experts = jnp.arange(N_EXPERTS, dtype=I32)
    counts = jnp.sum((ef[:, None] == experts[None, :]).astype(I32), axis=0)
    cend = jnp.cumsum(counts)
    cstart = cend - counts
    first_row = jnp.arange(n_tiles, dtype=I32) * tm
    e_first = jnp.sum((cend[None, :] <= first_row[:, None]).astype(I32), axis=1)
    e_last = jnp.sum((cend[None, :] <= (first_row + tm - 1)[:, None]).astype(I32), axis=1)
    n_sub = e_last - e_first + 1
    base = jnp.cumsum(n_sub) - n_sub
    step = jnp.arange(n_steps, dtype=I32)
    tile_of = jnp.clip(jnp.sum((base[None, :] <= step[:, None]).astype(I32), axis=1) - 1, 0, n_tiles - 1)
    exp_of = jnp.clip(_small_take(e_first, tile_of) + step - _small_take(base, tile_of), 0, N_EXPERTS - 1)
    used = step < jnp.sum(n_sub)
    row0 = tile_of * tm
    lo = jnp.where(used, jnp.clip(_small_take(cstart, exp_of) - row0, 0, tm), 0)
    hi = jnp.where(used, jnp.clip(_small_take(cend, exp_of) - row0, 0, tm), 0)
    src = (order % t).reshape(n_tiles, 1, tm)

    def smem(f):
        return pl.BlockSpec((1, 1, tm), lambda s, tl, ex, lo_, hi_: (tl[f(s)], 0, 0), memory_space=pltpu.SMEM)

    wspec = lambda shape: pl.BlockSpec((None, None) + shape, lambda s, tl, ex, lo_, hi_: (layer, ex[s], 0, 0))
    grid_spec = pltpu.PrefetchScalarGridSpec(
        num_scalar_prefetch=4, grid=(n_steps,),
        in_specs=[smem(lambda s: s), smem(lambda s: jnp.minimum(s + 1, n_steps - 1)),
                  pl.BlockSpec(memory_space=pl.ANY),
                  wspec((d, D_EXPERT)), wspec((d, D_EXPERT)), wspec((D_EXPERT, d))],
        out_specs=pl.BlockSpec((tm, d), lambda s, tl, ex, lo_, hi_: (tl[s], 0)),
        scratch_shapes=[pltpu.VMEM((2, tm, d), F32), pltpu.SemaphoreType.DMA((2,)),
                        pltpu.VMEM((d, D_EXPERT), BF16), pltpu.VMEM((d, D_EXPERT), BF16),
                        pltpu.VMEM((D_EXPERT, d), BF16)])
    ys = pl.pallas_call(
        functools.partial(_moe_body, n_steps=n_steps),
        grid_spec=grid_spec, out_shape=_sds((n_pairs, d)),
        compiler_params=_params("arbitrary"), name="moe",
    )(tile_of, exp_of, lo, hi, src, src, h2, w1, w3, w2)
    return ys, jnp.argsort(order).astype(I32)


def _combine_body(posc_ref, posn_ref, x1_ref, r_ref, g2_ref, ys_ref, o_ref, ybuf, sem, *, n_tiles):
    tm = x1_ref.shape[0]
    i = pl.program_id(0)
    slot = i % 2

    def gather(idx_ref, b):
        for j in range(2 * tm):
            pltpu.make_async_copy(ys_ref.at[pl.ds(idx_ref[0, 0, j], 1)], ybuf.at[b, pl.ds(j, 1)],
                                  sem.at[b]).start(priority=j % 2)

    @pl.when(i == 0)
    def _():
        gather(posc_ref, 0)

    @pl.when(i + 1 < n_tiles)
    def _():
        gather(posn_ref, 1 - slot)

    pltpu.make_async_copy(ys_ref.at[pl.ds(0, 2 * tm)], ybuf.at[slot], sem.at[slot]).wait()
    r = r_ref[...]
    lane = lax.broadcasted_iota(I32, r.shape, 1)
    w1 = jnp.sum(jnp.where(lane == 2, r, 0.0), axis=-1, keepdims=True)
    w2 = jnp.sum(jnp.where(lane == 3, r, 0.0), axis=-1, keepdims=True)
    o_ref[...] = x1_ref[...] + g2_ref[...] * (w1 * ybuf[slot, :tm] + w2 * ybuf[slot, tm:])


def _combine(x1, ys, pos, route, mods, mod_row_of_tile):
    t, d = x1.shape
    tm = TM_COMBINE
    nt = t // tm
    tok = lambda i: (i, 0)
    idx = pos.reshape(2, nt, tm).transpose(1, 0, 2).reshape(nt, 1, 2 * tm)
    smem = lambda f: pl.BlockSpec((1, 1, 2 * tm), lambda i: (f(i), 0, 0), memory_space=pltpu.SMEM)
    return pl.pallas_call(
        functools.partial(_combine_body, n_tiles=nt),
        grid=(nt,),
        in_specs=[smem(lambda i: i), smem(lambda i: jnp.minimum(i + 1, nt - 1)),
                  pl.BlockSpec((tm, d), tok), pl.BlockSpec((tm, LANES), tok),
                  pl.BlockSpec((None, 1, d), lambda i: (mod_row_of_tile(i, tm), 0, 5)),
                  pl.BlockSpec(memory_space=pl.ANY)],
        out_specs=pl.BlockSpec((tm, d), tok), out_shape=_sds((t, d)),
        scratch_shapes=[pltpu.VMEM((2, 2 * tm, d), F32), pltpu.SemaphoreType.DMA((2,))],
        compiler_params=_params("arbitrary"), name="combine",
    )(idx, idx, x1, route, mods, ys)


def _rope_tables():
    t = jnp.arange(GRID_W * GRID_ROWS)
    pairs = HEAD_DIM // 4
    inv = 1.0 / (ROPE_THETA ** (jnp.arange(pairs, dtype=F32) * 2.0 / (HEAD_DIM // 2)))
    ang_r = (t // GRID_W).astype(F32)[:, None] * inv
    ang_c = (t % GRID_W).astype(F32)[:, None] * inv
    cos = jnp.concatenate([jnp.cos(ang_r)] * 2 + [jnp.cos(ang_c)] * 2, axis=1)
    sin = jnp.concatenate([-jnp.sin(ang_r), jnp.sin(ang_r), -jnp.sin(ang_c), jnp.sin(ang_c)], axis=1)
    return jnp.tile(cos, (1, 2)), jnp.tile(sin, (1, 2))


def _bias_pair_tiles(rpb_l):
    qc = jnp.arange(GRID_W)[:, None]
    kc = jnp.arange(GRID_W)[None, :]
    start_c = jnp.clip(qc - NA_COLS // 2, 0, GRID_W - NA_COLS)
    col_ok = (kc >= start_c) & (kc < start_c + NA_COLS)
    dc = jnp.clip(kc - qc + (NA_COLS - 1), 0, 2 * NA_COLS - 2)
    d = jnp.arange(-8, 23)
    rows = jnp.stack([rpb_l[:, min(max(dd, 0), 2 * NA_ROWS - 2)] for dd in range(-8, 23)], axis=1)
    onehot = (dc[None] == jnp.arange(2 * NA_COLS - 1)[:, None, None]).astype(F32)
    tile = jnp.einsum("hdx,xqk->hdqk", rows, onehot, precision=lax.Precision.HIGHEST)
    ok = col_ok[None, None] & ((d >= 0) & (d <= 2 * NA_ROWS - 2))[None, :, None, None]
    tile = jnp.where(ok, tile, NEG_INF).astype(F32)
    return jnp.concatenate([tile[:, :-1], tile[:, 1:]], axis=-1)


def kernel(x_prompt, x_sample, c, cache_a_k, cache_a_v, cache_b_k, cache_b_v, state_c_fwd, state_c_bwd, c_ctx, norm1_g, norm2_g, w_mod, b_mod, w_in, q_norm_a, k_norm_a, sink_a, q_norm_b, k_norm_b, rpb_b, w_gk_up_f, b_gk_f, w_gk_up_b, b_gk_b, gla_norm_g, w_out, w_group, w_router, w1, w3, w2):
    depth = w_in.shape[0]
    nb_c, l_c, d = x_prompt.shape
    nb_s, n_s, _ = x_sample.shape
    past = cache_a_k.shape[2]
    tm = TM_PROJ

    w_in_b = jnp.pad(w_in, ((0, 0), (0, 0), (0, N_IN_PAD - N_IN))).astype(BF16)
    w_out_b = w_out.astype(BF16)
    wr_b = jnp.pad(jnp.concatenate([w_group, w_router], axis=-1),
                   ((0, 0), (0, 0), (0, LANES - N_GROUPS - N_EXPERTS))).astype(BF16)
    wgk = jnp.zeros((depth, 256, 1024), F32)
    wgk = wgk.at[:, :C_RANK, :512].set(w_gk_up_f).at[:, C_RANK:2 * C_RANK, 512:].set(w_gk_up_b).astype(BF16)
    cos, sin = _rope_tables()
    head_id = jnp.arange(512) // HEAD_DIM
    consts = dict(
        bd=(head_id[:, None] == head_id[None, :]).astype(BF16),
        gqa=jnp.tile(q_norm_a, (1, 8)).reshape(depth, 1, 512),
        gka=jnp.tile(k_norm_a, (1, 2)).reshape(depth, 1, LANES),
        gqb=jnp.tile(q_norm_b, (1, 8)).reshape(depth, 1, 512),
        gkb=jnp.tile(k_norm_b, (1, 8)).reshape(depth, 1, 512),
        cos=cos, sin=sin, wgk=wgk,
        bgk=jnp.concatenate([b_gk_f, b_gk_b], axis=-1).reshape(depth, 1, 1024))
    n1 = norm1_g.reshape(depth, 1, d)
    n2 = norm2_g.reshape(depth, 1, d)
    gn = gla_norm_g.reshape(depth, 1, C_DV)
    cond = jnp.zeros((16, d), F32).at[0].set(c_ctx).at[1:1 + nb_s].set(c)
    ctx_row = lambda i, tile: 0
    lat_row = lambda i, tile: 1 + i // (n_s // tile)
    cak = cache_a_k.reshape(nb_s, depth, 1, past, LANES).astype(BF16)
    cav = cache_a_v.reshape(nb_s, depth, 1, past, LANES).astype(BF16)
    cbk = cache_b_k.reshape(nb_s, depth, past, 4, LANES).transpose(0, 1, 3, 2, 4).astype(BF16)
    cbv = cache_b_v.reshape(nb_s, depth, past, 4, LANES).transpose(0, 1, 3, 2, 4).astype(BF16)

    xp = x_prompt.reshape(nb_c * l_c, d)
    xs = x_sample.reshape(nb_s * n_s, d)
    kv_ctx = st_ctx = None
    for l in range(depth):
        mods = _adaln(cond, w_mod, b_mod, l).reshape(16, 1, 6 * d)
        tp = _bias_pair_tiles(rpb_b[l])

        aq, ak, av, bq, bk, bv, cq, ck, cv, cg, la, *kv_ctx = _inproj(
            xp, mods, ctx_row, n1, w_in_b, consts, l, rope=False, tm=TM_PROJ // 2, seq_len=l_c, prev=kv_ctx)
        oa = _attention("a_ctx", aq, ak.reshape(1, -1, LANES), av.reshape(1, -1, LANES), l_c, sink=sink_a, layer=l)
        ob = _attention("b_ctx", bq, bk, bv, l_c)
        oc, *st_ctx = _gla(cq, ck, cv, cg, la, gn, l_c, l, emit_state=True, prev=st_ctx)
        x1, h2, route = _outproj(oa, ob, oc, xp, mods, ctx_row, n2, w_out_b, wr_b, l)
        xp = _combine(x1, *_moe(h2, route, w1, w3, w2, l), route, mods, ctx_row)

        aq, ak, av, bq, bk, bv, cq, ck, cv, cg, la = _inproj(
            xs, mods, lat_row, n1, w_in_b, consts, l, rope=True, tm=TM_PROJ)
        oa = _attention("a_lat", aq, ak.reshape(1, -1, LANES), av.reshape(1, -1, LANES), n_s,
                        prefix=(cak[:, l], cav[:, l]), sink=sink_a, layer=l)
        ob = _attention("b_lat", bq, bk, bv, n_s, prefix=(cbk[:, l], cbv[:, l]), tp=tp)
        oc, = _gla(cq, ck, cv, cg, la, gn, n_s, l, s0=(state_c_fwd, state_c_bwd), emit_state=False)
        x1, h2, route = _outproj(oa, ob, oc, xs, mods, lat_row, n2, w_out_b, wr_b, l)
        xs = _combine(x1, *_moe(h2, route, w1, w3, w2, l), route, mods, lat_row)

    new_kv = [a.reshape(nb_c, depth, l_c, -1, HEAD_DIM) for a in kv_ctx]
    return (xp.reshape(nb_c, l_c, d), xs.reshape(nb_s, n_s, d), *new_kv, *st_ctx)
```
